```python
import jax, jax.numpy as jnp
from jax import lax
import numpy as np

D_MODEL = 1024
BATCH = 4
SEQ = 8192
DEPTH = 2
DEC_BATCH = 32
DEC_SEQ = 64
PAST_LEN = 2048

CHUNK = 64
N_MIXERS = 2
N_GLA = (DEPTH + 1) // 2
N_MLSTM = DEPTH // 2
PLE_DIM = 256
EPS = 1e-6
F32 = jnp.float32

GLA_HEADS = 4
GLA_DK = D_MODEL // 2
GLA_DV = D_MODEL
GLA_HK = GLA_DK // GLA_HEADS
GLA_HV = GLA_DV // GLA_HEADS
GLA_GATE_RANK = 16
GLA_GATE_NORMALIZER = 16.0
GLA_IN = 2 * GLA_DK + 2 * GLA_DV + GLA_GATE_RANK

ML_HEADS = 4
ML_INNER = 2 * D_MODEL
ML_HD = ML_INNER // ML_HEADS
ML_CONV = 4
ML_QKV_BLOCK = 4

PEER_HEADS = 8
PEER_NKEYS = 128
PEER_NEXPERTS = PEER_NKEYS * PEER_NKEYS
PEER_DKEY = 128
PEER_HALF = PEER_DKEY // 2
PEER_TOPK = 16
PEER_TOKEN_BLOCK = 128

kernel_name = "gla_mlstm_peer_streaming_step"


def rmsnorm(x, g):
    xf = x.astype(F32)
    y = xf * lax.rsqrt(jnp.mean(xf * xf, axis=-1, keepdims=True) + EPS)
    return (y * g.astype(F32)).astype(x.dtype)


def to_chunks(a, L):
    B, T, H = a.shape[:3]
    a = a.reshape((B, T // L, L, H) + a.shape[3:])
    return jnp.moveaxis(jnp.moveaxis(a, 1, 0), 3, 2)


def from_chunks(o):
    N, B, H, L, d = o.shape
    return o.transpose(1, 0, 3, 2, 4).reshape(B, N * L, H, d)


def gla_scan(q, k, v, lg, S0):
    T = q.shape[1]
    L = min(CHUNK, T)
    causal = jnp.tril(jnp.ones((L, L), dtype=bool))

    def step(S, c):
        qc, kc, vc, gc = c
        b = jnp.cumsum(gc, axis=2)
        inter = jnp.einsum('bhtd,bhde->bhte', qc * jnp.exp(b), S)
        diff = jnp.where(causal[:, :, None], b[:, :, :, None, :] - b[:, :, None, :, :], -jnp.inf)
        att = jnp.einsum('bhtsd,bhsd->bhts', qc[:, :, :, None, :] * jnp.exp(diff), kc)
        intra = jnp.einsum('bhts,bhse->bhte', att, vc)
        bL = b[:, :, -1:, :]
        S_new = jnp.exp(bL[:, :, 0, :])[..., None] * S + jnp.einsum('bhsd,bhse->bhde', kc * jnp.exp(bL - b), vc)
        return S_new, inter + intra

    xs = (to_chunks(q, L), to_chunks(k, L), to_chunks(v, L), to_chunks(lg, L))
    S, o = lax.scan(step, S0, xs)
    return from_chunks(o), S


def gla_mixer(xn, S0, w_in, w_gate, b_gate, g_norm, w_out):
    B, T, _ = xn.shape
    proj = xn @ w_in
    q, k, v, r, gl = jnp.split(proj, [GLA_DK, 2 * GLA_DK, 2 * GLA_DK + GLA_DV, 2 * GLA_DK + 2 * GLA_DV], axis=-1)
    lg = jax.nn.log_sigmoid((gl @ w_gate + b_gate).astype(F32)) / GLA_GATE_NORMALIZER
    hk = lambda a: a.reshape(B, T, GLA_HEADS, GLA_HK).astype(F32)
    o, S = gla_scan(hk(q) * (GLA_HK ** -0.5), hk(k), v.reshape(B, T, GLA_HEADS, GLA_HV).astype(F32), hk(lg),
                    S0.astype(F32))
    o = o * lax.rsqrt(jnp.mean(o * o, axis=-1, keepdims=True) + EPS) * g_norm.astype(F32)
    o = o.reshape(B, T, GLA_DV).astype(xn.dtype) * jax.nn.silu(r)
    return o @ w_out, S.astype(xn.dtype)


def causal_conv(x, buf, w, b):
    T = x.shape[1]
    xp = jnp.concatenate([buf.astype(x.dtype), x], axis=1)
    y = b
    for j in range(ML_CONV):
        y = y + w[j] * xp[:, j:j + T]
    return y, xp[:, -(ML_CONV - 1):]


def blockdiag(x, w):
    B, T, C = x.shape
    xb = x.reshape(B, T, C // ML_QKV_BLOCK, ML_QKV_BLOCK)
    return jnp.einsum('btgi,gio->btgo', xb, w).reshape(B, T, C)


def mlstm_scan(q, k, v, ig, lf, C0, n0, m0):
    T = q.shape[1]
    L = min(CHUNK, T)
    causal = jnp.tril(jnp.ones((L, L), dtype=bool))

    def step(carry, c):
        C, n, m = carry
        qc, kc, vc, ic, fc = c
        F = jnp.cumsum(fc, axis=-1)
        logD = jnp.where(causal, F[..., :, None] - F[..., None, :] + ic[..., None, :], -jnp.inf)
        inter = F + m[..., None]
        mt = jnp.maximum(inter, jnp.max(logD, axis=-1))
        sc = jnp.einsum('bhtd,bhsd->bhts', qc, kc) * jnp.exp(logD - mt[..., None])
        a = jnp.exp(inter - mt)
        num = a[..., None] * jnp.einsum('bhtd,bhde->bhte', qc, C) + jnp.einsum('bhts,bhse->bhte', sc, vc)
        den = a * jnp.einsum('bhtd,bhd->bht', qc, n) + jnp.sum(sc, axis=-1)
        h = num / jnp.maximum(jnp.abs(den), jnp.exp(-mt))[..., None]
        m_new = mt[..., -1]
        w_end = jnp.exp(F[..., -1:] - F + ic - m_new[..., None])
        a_end = jnp.exp(F[..., -1] + m - m_new)
        kw = kc * w_end[..., None]
        C_new = a_end[..., None, None] * C + jnp.einsum('bhsd,bhse->bhde', kw, vc)
        n_new = a_end[..., None] * n + jnp.sum(kw, axis=2)
        return (C_new, n_new, m_new), h

    xs = (to_chunks(q, L), to_chunks(k, L), to_chunks(v, L), to_chunks(ig, L), to_chunks(lf, L))
    (C, n, m), h = lax.scan(step, (C0, n0, m0), xs)
    return from_chunks(h), C, n, m


def mlstm_mixer(xn, C0, n0, m0, buf, w_up, conv_w, conv_b, w_q, w_k, w_v, w_i, b_i, w_f, b_f, g_norm, skip, w_down):
    B, T, _ = xn.shape
    xm, z = jnp.split(xn @ w_up, 2, axis=-1)
    xc, new_buf = causal_conv(xm, buf, conv_w, conv_b)
    xc = jax.nn.silu(xc)
    q = blockdiag(xc, w_q)
    k = blockdiag(xc, w_k)
    v = blockdiag(xm, w_v)
    qkv = jnp.concatenate([q, k, v], axis=-1)
    ig = (qkv @ w_i + b_i).astype(F32)
    lf = jax.nn.log_sigmoid((qkv @ w_f + b_f).astype(F32))
    hd = lambda a: a.reshape(B, T, ML_HEADS, ML_HD).astype(F32)
    h, C, n, m = mlstm_scan(hd(q), hd(k) * (ML_HD ** -0.5), hd(v), ig, lf,
                            C0.astype(F32), n0.astype(F32), m0.astype(F32))
    mu = jnp.mean(h, axis=-1, keepdims=True)
    var = jnp.mean(jnp.square(h - mu), axis=-1, keepdims=True)
    h = ((h - mu) * lax.rsqrt(var + EPS)).reshape(B, T, ML_INNER) * g_norm.astype(F32)
    h = (h.astype(xn.dtype) + skip * xc) * jax.nn.silu(z)
    dt = xn.dtype
    return h @ w_down, C.astype(dt), n.astype(dt), m.astype(dt), new_buf.astype(dt)


def peer(xn, w_query, keys, u, v):
    B, T, D = xn.shape
    ntok = B * T
    nblk = -(-ntok // PEER_TOKEN_BLOCK)
    pad = nblk * PEER_TOKEN_BLOCK - ntok
    xt = jnp.pad(xn.reshape(ntok, D), ((0, pad), (0, 0))).reshape(nblk, PEER_TOKEN_BLOCK, D)
    keys32 = keys.astype(F32)

    def block(xb):
        P = xb.shape[0]
        qh = (xb @ w_query).reshape(P, PEER_HEADS, 2, PEER_HALF).astype(F32)
        s = jnp.einsum('phcd,chkd->phck', qh, keys32)
        top_s, top_i = lax.top_k(s, PEER_TOPK)
        cand_s = top_s[:, :, 0, :, None] + top_s[:, :, 1, None, :]
        cand_i = top_i[:, :, 0, :, None] * PEER_NKEYS + top_i[:, :, 1, None, :]
        cand_s = cand_s.reshape(P, PEER_HEADS, PEER_TOPK * PEER_TOPK)
        cand_i = cand_i.reshape(P, PEER_HEADS, PEER_TOPK * PEER_TOPK)
        best_s, best_j = lax.top_k(cand_s, PEER_TOPK)
        idx = jnp.take_along_axis(cand_i, best_j, axis=-1).reshape(P, PEER_HEADS * PEER_TOPK)
        g = jax.nn.softmax(best_s, axis=-1).reshape(P, PEER_HEADS * PEER_TOPK)
        act = jax.nn.gelu(jnp.einsum('pd,ped->pe', xb, u[idx]).astype(F32), approximate=False)
        coef = (g * act).astype(xb.dtype)
        return jnp.einsum('pe,ped->pd', coef, v[idx])

    out = lax.map(block, xt)
    return out.reshape(nblk * PEER_TOKEN_BLOCK, D)[:ntok].reshape(B, T, D)


def run_group(x, p, S_gla, C_ml, n_ml, m_ml, buf_ml, W):
    h = x
    new_S, new_C, new_n, new_m, new_buf = [], [], [], [], []
    for i in range(DEPTH):
        xn = rmsnorm(h, W['norm_mix'][i])
        j = i // N_MIXERS
        if i % N_MIXERS == 0:
            o, S = gla_mixer(xn, S_gla[j], W['w_gla_in'][j], W['w_gla_gate'][j], W['b_gla_gate'][j],
                             W['g_gla_norm'][j], W['w_gla_out'][j])
            new_S.append(S)
        else:
            o, C, n, m, buf = mlstm_mixer(xn, C_ml[j], n_ml[j], m_ml[j], buf_ml[j],
                                          W['w_ml_up'][j], W['ml_conv_w'][j], W['ml_conv_b'][j],
                                          W['w_ml_q'][j], W['w_ml_k'][j], W['w_ml_v'][j],
                                          W['w_ml_igate'][j], W['b_ml_igate'][j],
                                          W['w_ml_fgate'][j], W['b_ml_fgate'][j],
                                          W['g_ml_norm'][j], W['ml_skip'][j], W['w_ml_down'][j])
            new_C.append(C); new_n.append(n); new_m.append(m); new_buf.append(buf)
        h = h + o
        h = h + peer(rmsnorm(h, W['norm_ffn'][i]), W['w_peer_query'][i], W['peer_keys'][i],
                     W['peer_u'][i], W['peer_v'][i])
        gate = jax.nn.sigmoid(rmsnorm(h, W['norm_ple'][i]) @ W['w_ple_gate'][i])
        h = h + gate * (p[i].astype(h.dtype) @ W['w_ple_proj'][i])
    y = rmsnorm(h, W['norm_final'])
    return (y, jnp.stack(new_S), jnp.stack(new_C), jnp.stack(new_n), jnp.stack(new_m), jnp.stack(new_buf))


def setup_inputs(seed: int = 0) -> dict:
    key = jax.random.key(seed)
    ks = iter(jax.random.split(key, 64))
    nrm = lambda shape, scale: jax.random.normal(next(ks), shape, F32) * scale
    gain = lambda shape: 1.0 + 0.01 * jax.random.normal(next(ks), shape, F32)
    D, DI = D_MODEL, ML_INNER
    d = {}
    d['x_prompt'] = nrm((BATCH, SEQ, D), 1.0)
    d['x_sample'] = nrm((DEC_BATCH, DEC_SEQ, D), 1.0)
    d['state_gla_S'] = nrm((N_GLA, DEC_BATCH, GLA_HEADS, GLA_HK, GLA_HV), 0.3)
    d['state_mlstm_C'] = nrm((N_MLSTM, DEC_BATCH, ML_HEADS, ML_HD, ML_HD), 0.1)
    d['state_mlstm_n'] = nrm((N_MLSTM, DEC_BATCH, ML_HEADS, ML_HD), 0.1)
    d['state_mlstm_m'] = nrm((N_MLSTM, DEC_BATCH, ML_HEADS), 1.0)
    d['state_mlstm_conv'] = nrm((N_MLSTM, DEC_BATCH, ML_CONV - 1, DI), 1.0)
    d['p_prompt'] = nrm((DEPTH, BATCH, SEQ, PLE_DIM), 1.0)
    d['p_sample'] = nrm((DEPTH, DEC_BATCH, DEC_SEQ, PLE_DIM), 1.0)
    d['w_gla_in'] = nrm((N_GLA, D, GLA_IN), D ** -0.5)
    d['w_gla_gate'] = nrm((N_GLA, GLA_GATE_RANK, GLA_DK), GLA_GATE_RANK ** -0.5)
    d['b_gla_gate'] = nrm((N_GLA, GLA_DK), 0.1)
    d['g_gla_norm'] = gain((N_GLA, GLA_HV))
    d['w_gla_out'] = nrm((N_GLA, GLA_DV, D), GLA_DV ** -0.5)
    d['w_ml_up'] = nrm((N_MLSTM, D, 2 * DI), D ** -0.5)
    d['ml_conv_w'] = nrm((N_MLSTM, ML_CONV, DI), ML_CONV ** -0.5)
    d['ml_conv_b'] = nrm((N_MLSTM, DI), 0.01)
    d['w_ml_q'] = nrm((N_MLSTM, DI // ML_QKV_BLOCK, ML_QKV_BLOCK, ML_QKV_BLOCK), ML_QKV_BLOCK ** -0.5)
    d['w_ml_k'] = nrm((N_MLSTM, DI // ML_QKV_BLOCK, ML_QKV_BLOCK, ML_QKV_BLOCK), ML_QKV_BLOCK ** -0.5)
    d['w_ml_v'] = nrm((N_MLSTM, DI // ML_QKV_BLOCK, ML_QKV_BLOCK, ML_QKV_BLOCK), ML_QKV_BLOCK ** -0.5)
    d['w_ml_igate'] = nrm((N_MLSTM, 3 * DI, ML_HEADS), (3 * DI) ** -0.5)
    d['b_ml_igate'] = nrm((N_MLSTM, ML_HEADS), 0.1)
    d['w_ml_fgate'] = nrm((N_MLSTM, 3 * DI, ML_HEADS), (3 * DI) ** -0.5)
    d['b_ml_fgate'] = jnp.linspace(3.0, 6.0, ML_HEADS, dtype=F32)[None, :] + nrm((N_MLSTM, ML_HEADS), 0.1)
    d['g_ml_norm'] = gain((N_MLSTM, DI))
    d['ml_skip'] = gain((N_MLSTM, DI))
    d['w_ml_down'] = nrm((N_MLSTM, DI, D), DI ** -0.5)
    d['w_peer_query'] = nrm((DEPTH, D, PEER_HEADS * PEER_DKEY), D ** -0.5)
    d['peer_keys'] = nrm((DEPTH, 2, PEER_HEADS, PEER_NKEYS, PEER_HALF), PEER_HALF ** -0.5)
    d['peer_u'] = nrm((DEPTH, PEER_NEXPERTS, D), D ** -0.5)
    d['peer_v'] = nrm((DEPTH, PEER_NEXPERTS, D), PEER_HEADS ** -0.5)
    d['norm_mix'] = gain((DEPTH, D))
    d['norm_ffn'] = gain((DEPTH, D))
    d['norm_ple'] = gain((DEPTH, D))
    d['w_ple_gate'] = nrm((DEPTH, D, D), D ** -0.5)
    d['w_ple_proj'] = nrm((DEPTH, PLE_DIM, D), PLE_DIM ** -0.5)
    d['norm_final'] = gain((D,))
    return d


def reference(x_prompt, x_sample, state_gla_S, state_mlstm_C, state_mlstm_n, state_mlstm_m, state_mlstm_conv,
              p_prompt, p_sample, w_gla_in, w_gla_gate, b_gla_gate, g_gla_norm, w_gla_out,
              w_ml_up, ml_conv_w, ml_conv_b, w_ml_q, w_ml_k, w_ml_v, w_ml_igate, b_ml_igate,
              w_ml_fgate, b_ml_fgate, g_ml_norm, ml_skip, w_ml_down,
              w_peer_query, peer_keys, peer_u, peer_v, norm_mix, norm_ffn, norm_ple,
              w_ple_gate, w_ple_proj, norm_final):
    W = dict(w_gla_in=w_gla_in, w_gla_gate=w_gla_gate, b_gla_gate=b_gla_gate, g_gla_norm=g_gla_norm,
             w_gla_out=w_gla_out, w_ml_up=w_ml_up, ml_conv_w=ml_conv_w, ml_conv_b=ml_conv_b,
             w_ml_q=w_ml_q, w_ml_k=w_ml_k, w_ml_v=w_ml_v, w_ml_igate=w_ml_igate, b_ml_igate=b_ml_igate,
             w_ml_fgate=w_ml_fgate, b_ml_fgate=b_ml_fgate, g_ml_norm=g_ml_norm, ml_skip=ml_skip,
             w_ml_down=w_ml_down, w_peer_query=w_peer_query, peer_keys=peer_keys, peer_u=peer_u,
             peer_v=peer_v, norm_mix=norm_mix, norm_ffn=norm_ffn, norm_ple=norm_ple,
             w_ple_gate=w_ple_gate, w_ple_proj=w_ple_proj, norm_final=norm_final)
    dt = x_prompt.dtype
    S0 = jnp.zeros((N_GLA, BATCH, GLA_HEADS, GLA_HK, GLA_HV), dt)
    C0 = jnp.zeros((N_MLSTM, BATCH, ML_HEADS, ML_HD, ML_HD), dt)
    n0 = jnp.zeros((N_MLSTM, BATCH, ML_HEADS, ML_HD), dt)
    m0 = jnp.zeros((N_MLSTM, BATCH, ML_HEADS), dt)
    b0 = jnp.zeros((N_MLSTM, BATCH, ML_CONV - 1, ML_INNER), dt)
    y_prompt, S_p, C_p, n_p, m_p, buf_p = run_group(x_prompt, p_prompt, S0, C0, n0, m0, b0, W)
    y_sample, S_s, C_s, n_s, m_s, buf_s = run_group(x_sample, p_sample, state_gla_S, state_mlstm_C,
                                                     state_mlstm_n, state_mlstm_m, state_mlstm_conv, W)
    return (y_prompt, y_sample, S_p, S_s, C_p, C_s, n_p, n_s, m_p, m_s, buf_p, buf_s)
```

```python
import functools

import jax
import jax.numpy as jnp
from jax import lax
from jax.experimental import pallas as pl
from jax.experimental.pallas import tpu as pltpu

F32 = jnp.float32
BF16 = jnp.bfloat16
EPS = 1e-6
CHUNK = 64
HIGHEST = lax.Precision.HIGHEST
NEG_INF = float("-inf")

GLA_HEADS = 4
GLA_GATE_NORMALIZER = 16.0
ML_HEADS = 4
ML_CONV = 4
ML_QKV_BLOCK = 4
PEER_HEADS = 8
PEER_NKEYS = 128
PEER_HALF = 64
PEER_TOPK = 16

VMEM_LIMIT_BYTES = 52 * 1024 * 1024
MXU_TILE = 256
PEER_EXPERT_TILE = 256


def _cparams(sem):
    return pltpu.CompilerParams(dimension_semantics=sem, vmem_limit_bytes=VMEM_LIMIT_BYTES)


def _pick_block(n, candidates):
    for c in candidates:
        if n % c == 0:
            return c
    raise ValueError(f"no block size in {candidates} divides {n}")


def _rms(x, g):
    ms = jnp.mean(x * x, axis=-1, keepdims=True)
    return x * lax.rsqrt(ms + EPS) * g


def _log_sigmoid(x):
    return jnp.minimum(x, 0.0) - jnp.log1p(jnp.exp(-jnp.abs(x)))


def _silu(x):
    return x * jax.nn.sigmoid(x)


def _gelu(x):
    return 0.5 * x * (1.0 + lax.erf(x * 0.7071067811865476))


def _dot(a, b):
    return jnp.dot(a, b, preferred_element_type=F32)


def _dot_nt(a, b):
    return lax.dot_general(a, b, (((1,), (1,)), ((), ())), preferred_element_type=F32)


def _dot_tn(a, b):
    return lax.dot_general(a, b, (((0,), (0,)), ((), ())), preferred_element_type=F32)


def _tok(tb, d):
    return pl.BlockSpec((tb, d), lambda i: (i, 0))


def _full(shape):
    nd = len(shape)
    return pl.BlockSpec(shape, lambda *_: (0,) * nd)


def _gla_pre_kernel(h_ref, g_ref, w_ref, wgl_ref, wgate_ref, bgate_ref,
                    q_ref, k_ref, v_ref, r_ref, lg_ref, *, dk, dv, hk):
    xn = _rms(h_ref[...], g_ref[...]).astype(BF16)
    proj = _dot(xn, w_ref[...])
    q_ref[...] = proj[:, :dk] * (hk ** -0.5)
    k_ref[...] = proj[:, dk:2 * dk]
    v_ref[...] = proj[:, 2 * dk:2 * dk + dv].astype(BF16)
    r_ref[...] = proj[:, 2 * dk + dv:]
    gl = _dot(xn, wgl_ref[...])
    gate = _dot(gl.astype(BF16), wgate_ref[...]) + bgate_ref[...]
    lg_ref[...] = _log_sigmoid(gate) * (1.0 / GLA_GATE_NORMALIZER)


def _gla_pre(h, g, w_qkvr, w_gl, w_gate, b_gate, dk, dv):
    n, d = h.shape
    tb = _pick_block(n, (512, 256, 128, 64))
    hk = dk // GLA_HEADS
    outs = (jax.ShapeDtypeStruct((n, dk), F32), jax.ShapeDtypeStruct((n, dk), F32),
            jax.ShapeDtypeStruct((n, dv), BF16), jax.ShapeDtypeStruct((n, dv), F32),
            jax.ShapeDtypeStruct((n, dk), F32))
    return pl.pallas_call(
        functools.partial(_gla_pre_kernel, dk=dk, dv=dv, hk=hk),
        grid=(n // tb,),
        in_specs=[_tok(tb, d), _full(g.shape), _full(w_qkvr.shape), _full(w_gl.shape),
                  _full(w_gate.shape), _full(b_gate.shape)],
        out_specs=(_tok(tb, dk), _tok(tb, dk), _tok(tb, dv), _tok(tb, dv), _tok(tb, dk)),
        out_shape=outs,
        compiler_params=_cparams(("parallel",)),
        name="gla_pre",
    )(h, g, w_qkvr, w_gl, w_gate, b_gate)


GLA_SUB = 16


def _gla_scan_kernel(*refs, nchunks, zero_init):
    if zero_init:
        q_ref, k_ref, v_ref, lg_ref, gn_ref, o_ref, sout_ref, st_ref = refs
        s0_ref = None
    else:
        q_ref, k_ref, v_ref, lg_ref, gn_ref, s0_ref, o_ref, sout_ref, st_ref = refs
    t = pl.program_id(2)

    @pl.when(t == 0)
    def _():
        if zero_init:
            st_ref[...] = jnp.zeros_like(st_ref)
        else:
            st_ref[...] = s0_ref[...].T

    row = lax.broadcasted_iota(jnp.int32, (CHUNK, CHUNK), 0)
    col = lax.broadcasted_iota(jnp.int32, (CHUNK, CHUNK), 1)
    tril = (row >= col).astype(F32)
    gn = gn_ref[...]
    nsub = CHUNK // GLA_SUB

    for c in range(nchunks):
        sl = pl.ds(c * CHUNK, CHUNK)
        lg = lg_ref[sl, :]
        q = q_ref[sl, :]
        k = k_ref[sl, :]
        v = v_ref[sl, :]
        st = st_ref[...]
        b = jnp.dot(tril, lg, precision=HIGHEST, preferred_element_type=F32)
        b_last = b[CHUNK - 1:CHUNK, :]
        inter = _dot_nt((q * jnp.exp(b)).astype(BF16), st.astype(BF16))
        intra_parts = []
        for i in range(nsub):
            lo, hi = i * GLA_SUB, (i + 1) * GLA_SUB
            b_ref_row = b[lo:lo + 1, :]
            qe = (q[lo:hi, :] * jnp.exp(b[lo:hi, :] - b_ref_row)).astype(BF16)
            ke = (k[:hi, :] * jnp.exp(b_ref_row - b[:hi, :])).astype(BF16)
            att = _dot_nt(qe, ke)
            r_i = lax.broadcasted_iota(jnp.int32, (GLA_SUB, hi), 0) + lo
            c_i = lax.broadcasted_iota(jnp.int32, (GLA_SUB, hi), 1)
            att = jnp.where(r_i >= c_i, att, 0.0)
            intra_parts.append(_dot(att.astype(BF16), v[:hi, :]))
        o = inter + jnp.concatenate(intra_parts, axis=0)
        kd = (k * jnp.exp(b_last - b)).astype(BF16)
        st_ref[...] = st * jnp.exp(b_last) + _dot_tn(v, kd)
        o = o * lax.rsqrt(jnp.mean(o * o, axis=-1, keepdims=True) + EPS) * gn
        o_ref[sl, :] = o

    @pl.when(t == pl.num_programs(2) - 1)
    def _():
        sout_ref[...] = st_ref[...].T


def _gla_scan(q, k, v, lg, gn, s0, nseq, seqlen):
    n, dk = q.shape
    dv = v.shape[1]
    hk, hv = dk // GLA_HEADS, dv // GLA_HEADS
    tt = _pick_block(seqlen, (512, 256, 128, 64))
    nt = seqlen // tt
    zero_init = s0 is None
    tmap = lambda s, h, t: (s * nt + t, h)
    smap = lambda s, h, t: (s, h, 0, 0)
    in_specs = [pl.BlockSpec((tt, hk), tmap), pl.BlockSpec((tt, hk), tmap),
                pl.BlockSpec((tt, hv), tmap), pl.BlockSpec((tt, hk), tmap),
                pl.BlockSpec((1, hv), lambda s, h, t: (0, 0))]
    args = [q, k, v, lg, gn]
    if not zero_init:
        in_specs.append(pl.BlockSpec((None, None, hk, hv), smap))
        args.append(s0)
    return pl.pallas_call(
        functools.partial(_gla_scan_kernel, nchunks=tt // CHUNK, zero_init=zero_init),
        grid=(nseq, GLA_HEADS, nt),
        in_specs=in_specs,
        out_specs=(pl.BlockSpec((tt, hv), tmap), pl.BlockSpec((None, None, hk, hv), smap)),
        out_shape=(jax.ShapeDtypeStruct((n, dv), F32),
                   jax.ShapeDtypeStruct((nseq, GLA_HEADS, hk, hv), F32)),
        scratch_shapes=[pltpu.VMEM((hv, hk), F32)],
        compiler_params=_cparams(("parallel", "parallel", "arbitrary")),
        name="gla_scan",
    )(*args)


def _peer_pre_tail(h1, gffn_ref, wqt_ref, xt_ref, qt_ref):
    xn = _rms(h1, gffn_ref[...])
    xt = xn.T.astype(BF16)
    xt_ref[...] = xt
    qt_ref[...] = _dot(wqt_ref[...], xt).astype(BF16)


def _gla_post_kernel(o_ref, r_ref, h_ref, wout_ref, gffn_ref, wqt_ref, h1_ref, xt_ref, qt_ref):
    y = (o_ref[...] * _silu(r_ref[...])).astype(BF16)
    h1 = h_ref[...] + _dot(y, wout_ref[...])
    h1_ref[...] = h1
    _peer_pre_tail(h1, gffn_ref, wqt_ref, xt_ref, qt_ref)


def _gla_post(o, r, h, w_out, g_ffn, w_qt):
    n, d = h.shape
    tb = _pick_block(n, (512, 256, 128))
    dq = w_qt.shape[0]
    return pl.pallas_call(
        _gla_post_kernel,
        grid=(n // tb,),
        in_specs=[_tok(tb, o.shape[1]), _tok(tb, r.shape[1]), _tok(tb, d), _full(w_out.shape),
                  _full(g_ffn.shape), _full(w_qt.shape)],
        out_specs=(_tok(tb, d), pl.BlockSpec((d, tb), lambda i: (0, i)),
                   pl.BlockSpec((dq, tb), lambda i: (0, i))),
        out_shape=(jax.ShapeDtypeStruct((n, d), F32), jax.ShapeDtypeStruct((d, n), BF16),
                   jax.ShapeDtypeStruct((dq, n), BF16)),
        compiler_params=_cparams(("parallel",)),
        name="gla_post",
    )(o, r, h, w_out, g_ffn, w_qt)


def _ml_pre_kernel(h_ref, g_ref, w_ref, xm_ref, z_ref, *, di):
    xn = _rms(h_ref[...], g_ref[...]).astype(BF16)
    up = _dot(xn, w_ref[...])
    xm_ref[...] = up[:, :di]
    z_ref[...] = up[:, di:]


def _ml_pre(h, g, w_up):
    n, d = h.shape
    di = w_up.shape[1] // 2
    tb = _pick_block(n, (512, 256, 128, 64))
    return pl.pallas_call(
        functools.partial(_ml_pre_kernel, di=di),
        grid=(n // tb,),
        in_specs=[_tok(tb, d), _full(g.shape), _full(w_up.shape)],
        out_specs=(_tok(tb, di), _tok(tb, di)),
        out_shape=(jax.ShapeDtypeStruct((n, di), F32), jax.ShapeDtypeStruct((n, di), F32)),
        compiler_params=_cparams(("parallel",)),
        name="ml_pre",
    )(h, g, w_up)


HALO = 8


def _ml_qkv_kernel(xm_ref, prev_ref, buf_ref, cw_ref, cb_ref, wq_ref, wk_ref, wv_ref, wif_ref, bif_ref,
                   q_ref, k_ref, v_ref, xc_ref, gates_ref, xp_scr, *, tt, di, hd):
    t = pl.program_id(1)
    x = xm_ref[...]
    halo = jnp.where(t == 0, buf_ref[...], prev_ref[...])
    xp_scr[0:HALO, :] = halo
    xp_scr[HALO:HALO + tt, :] = x
    y = cb_ref[...]
    for j in range(ML_CONV):
        y = y + cw_ref[j:j + 1, :] * xp_scr[pl.ds(HALO - (ML_CONV - 1) + j, tt), :]
    xc = _silu(y)
    xc_ref[...] = xc
    xcb = xc.astype(BF16)
    xmb = x.astype(BF16)
    gates = jnp.zeros((tt, gates_ref.shape[1]), F32) + bif_ref[...]
    ntile = di // MXU_TILE
    for j in range(ntile):
        cs = slice(j * MXU_TILE, (j + 1) * MXU_TILE)
        qj = _dot(xcb[:, cs], wq_ref[j])
        kj = _dot(xcb[:, cs], wk_ref[j])
        vj = _dot(xmb[:, cs], wv_ref[j])
        q_ref[:, cs] = qj.astype(BF16)
        k_ref[:, cs] = kj * (hd ** -0.5)
        v_ref[:, cs] = vj.astype(BF16)
        gates = gates + _dot(qj.astype(BF16), wif_ref[j * MXU_TILE:(j + 1) * MXU_TILE, :])
        gates = gates + _dot(kj.astype(BF16), wif_ref[di + j * MXU_TILE:di + (j + 1) * MXU_TILE, :])
        gates = gates + _dot(vj.astype(BF16), wif_ref[2 * di + j * MXU_TILE:2 * di + (j + 1) * MXU_TILE, :])
    lane = lax.broadcasted_iota(jnp.int32, gates.shape, 1)
    is_f = (lane >= ML_HEADS) & (lane < 2 * ML_HEADS)
    gates_ref[...] = jnp.where(is_f, _log_sigmoid(gates), gates)


def _ml_qkv(xm, buf, conv_w, conv_b, wq, wk, wv, wif, bif, nseq, seqlen):
    n, di = xm.shape
    hd = di // ML_HEADS
    tt = _pick_block(seqlen, (256, 128, 64))
    nt = seqlen // tt
    tmap = lambda s, t: (s * nt + t, 0)
    prev_map = lambda s, t: (jnp.maximum((s * nt + t) * (tt // HALO) - 1, 0), 0)
    ng = wif.shape[1]
    return pl.pallas_call(
        functools.partial(_ml_qkv_kernel, tt=tt, di=di, hd=hd),
        grid=(nseq, nt),
        in_specs=[pl.BlockSpec((tt, di), tmap), pl.BlockSpec((HALO, di), prev_map),
                  pl.BlockSpec((None, HALO, di), lambda s, t: (s, 0, 0)),
                  _full(conv_w.shape), _full(conv_b.shape), _full(wq.shape), _full(wk.shape),
                  _full(wv.shape), _full(wif.shape), _full(bif.shape)],
        out_specs=(pl.BlockSpec((tt, di), tmap), pl.BlockSpec((tt, di), tmap),
                   pl.BlockSpec((tt, di), tmap), pl.BlockSpec((tt, di), tmap),
                   pl.BlockSpec((tt, ng), tmap)),
        out_shape=(jax.ShapeDtypeStruct((n, di), BF16), jax.ShapeDtypeStruct((n, di), F32),
                   jax.ShapeDtypeStruct((n, di), BF16), jax.ShapeDtypeStruct((n, di), F32),
                   jax.ShapeDtypeStruct((n, ng), F32)),
        scratch_shapes=[pltpu.VMEM((HALO + tt, di), F32)],
        compiler_params=_cparams(("parallel", "arbitrary")),
        name="ml_qkv",
    )(xm, xm, buf, conv_w, conv_b, wq, wk, wv, wif, bif)


def _lane_pick(x, idx):
    lane = lax.broadcasted_iota(jnp.int32, x.shape, 1)
    return jnp.sum(jnp.where(lane == idx, x, 0.0), axis=-1, keepdims=True)


def _ml_scan_kernel(*refs, nchunks, zero_init):
    if zero_init:
        q_ref, k_ref, v_ref, g_ref, h_ref, cout_ref, nout_ref, mout_ref, c_scr, n_scr, m_scr = refs
    else:
        (q_ref, k_ref, v_ref, g_ref, c0_ref, n0_ref, m0_ref,
         h_ref, cout_ref, nout_ref, mout_ref, c_scr, n_scr, m_scr) = refs
    head = pl.program_id(1)
    t = pl.program_id(2)

    @pl.when(t == 0)
    def _():
        if zero_init:
            c_scr[...] = jnp.zeros_like(c_scr)
            n_scr[...] = jnp.zeros_like(n_scr)
            m_scr[...] = jnp.zeros_like(m_scr)
        else:
            c_scr[...] = c0_ref[...]
            n_scr[...] = n0_ref[...]
            m_scr[...] = m0_ref[...]

    row = lax.broadcasted_iota(jnp.int32, (CHUNK, CHUNK), 0)
    col = lax.broadcasted_iota(jnp.int32, (CHUNK, CHUNK), 1)
    causal = row >= col
    tril = causal.astype(F32)
    eye = row == col
    ones = jnp.ones((CHUNK, CHUNK), F32)

    for c in range(nchunks):
        sl = pl.ds(c * CHUNK, CHUNK)
        q = q_ref[sl, :]
        kf = k_ref[sl, :]
        v = v_ref[sl, :]
        g = g_ref[sl, :]
        cmat = c_scr[...]
        nvec = n_scr[...]
        m = m_scr[0:1, 0:1]
        gcum = jnp.dot(tril, g, precision=HIGHEST, preferred_element_type=F32)
        i_col = _lane_pick(g, head)
        f_col = _lane_pick(gcum, head + ML_HEADS)
        w_row = jnp.dot(ones, jnp.where(eye, i_col - f_col, 0.0), precision=HIGHEST,
                        preferred_element_type=F32)
        log_d = jnp.where(causal, f_col + w_row, NEG_INF)
        inter = f_col + m
        mt = jnp.maximum(inter, jnp.max(log_d, axis=-1, keepdims=True))
        sc = _dot_nt(q, kf.astype(BF16)) * jnp.exp(log_d - mt)
        a = jnp.exp(inter - mt)
        num = a * _dot(q, cmat.astype(BF16)) + _dot(sc.astype(BF16), v)
        qn = jnp.sum(q.astype(F32) * nvec, axis=-1, keepdims=True)
        den = a * qn + jnp.sum(sc, axis=-1, keepdims=True)
        hh = num / jnp.maximum(jnp.abs(den), jnp.exp(-mt))
        m_new = mt[CHUNK - 1:CHUNK, :]
        f_last = f_col[CHUNK - 1:CHUNK, :]
        w_end = jnp.exp(f_last - f_col + i_col - m_new)
        a_end = jnp.exp(f_last + m - m_new)
        kw = kf * w_end
        c_scr[...] = a_end * cmat + _dot_tn(kw.astype(BF16), v)
        n_scr[...] = a_end * nvec + jnp.sum(kw, axis=0, keepdims=True)
        m_scr[...] = jnp.broadcast_to(m_new, m_scr.shape)
        mu = jnp.mean(hh, axis=-1, keepdims=True)
        var = jnp.mean(jnp.square(hh - mu), axis=-1, keepdims=True)
        h_ref[sl, :] = (hh - mu) * lax.rsqrt(var + EPS)

    @pl.when(t == pl.num_programs(2) - 1)
    def _():
        cout_ref[...] = c_scr[...]
        nout_ref[...] = n_scr[...]
        mout_ref[...] = m_scr[...]


M_LANES = 128


def _ml_scan(q, k, v, gates, state, nseq, seqlen):
    n, di = q.shape
    hd = di // ML_HEADS
    tt = _pick_block(seqlen, (256, 128, 64))
    nt = seqlen // tt
    zero_init = state is None
    tmap = lambda s, h, t: (s * nt + t, h)
    gmap = lambda s, h, t: (s * nt + t, 0)
    smap = lambda s, h, t: (s, h, 0, 0)
    ng = gates.shape[1]
    in_specs = [pl.BlockSpec((tt, hd), tmap), pl.BlockSpec((tt, hd), tmap),
                pl.BlockSpec((tt, hd), tmap), pl.BlockSpec((tt, ng), gmap)]
    args = [q, k, v, gates]
    if not zero_init:
        in_specs += [pl.BlockSpec((None, None, hd, hd), smap), pl.BlockSpec((None, None, 1, hd), smap),
                     pl.BlockSpec((None, None, 1, M_LANES), smap)]
        args += list(state)
    return pl.pallas_call(
        functools.partial(_ml_scan_kernel, nchunks=tt // CHUNK, zero_init=zero_init),
        grid=(nseq, ML_HEADS, nt),
        in_specs=in_specs,
        out_specs=(pl.BlockSpec((tt, hd), tmap), pl.BlockSpec((None, None, hd, hd), smap),
                   pl.BlockSpec((None, None, 1, hd), smap), pl.BlockSpec((None, None, 1, M_LANES), smap)),
        out_shape=(jax.ShapeDtypeStruct((n, di), F32),
                   jax.ShapeDtypeStruct((nseq, ML_HEADS, hd, hd), F32),
                   jax.ShapeDtypeStruct((nseq, ML_HEADS, 1, hd), F32),
                   jax.ShapeDtypeStruct((nseq, ML_HEADS, 1, M_LANES), F32)),
        scratch_shapes=[pltpu.VMEM((hd, hd), F32), pltpu.VMEM((1, hd), F32), pltpu.VMEM((1, M_LANES), F32)],
        compiler_params=_cparams(("parallel", "parallel", "arbitrary")),
        name="ml_scan",
    )(*args)


def _ml_post_kernel(hn_ref, xc_ref, z_ref, h_ref, gn_ref, skip_ref, wdown_ref, gffn_ref, wqt_ref,
                    h1_ref, xt_ref, qt_ref):
    y = (hn_ref[...] * gn_ref[...] + skip_ref[...] * xc_ref[...]) * _silu(z_ref[...])
    h1 = h_ref[...] + _dot(y.astype(BF16), wdown_ref[...])
    h1_ref[...] = h1
    _peer_pre_tail(h1, gffn_ref, wqt_ref, xt_ref, qt_ref)


def _ml_post(hn, xc, z, h, g_norm, skip, w_down, g_ffn, w_qt):
    n, d = h.shape
    di = hn.shape[1]
    tb = _pick_block(n, (512, 256, 128))
    dq = w_qt.shape[0]
    return pl.pallas_call(
        _ml_post_kernel,
        grid=(n // tb,),
        in_specs=[_tok(tb, di), _tok(tb, di), _tok(tb, di), _tok(tb, d), _full(g_norm.shape),
                  _full(skip.shape), _full(w_down.shape), _full(g_ffn.shape), _full(w_qt.shape)],
        out_specs=(_tok(tb, d), pl.BlockSpec((d, tb), lambda i: (0, i)),
                   pl.BlockSpec((dq, tb), lambda i: (0, i))),
        out_shape=(jax.ShapeDtypeStruct((n, d), F32), jax.ShapeDtypeStruct((d, n), BF16),
                   jax.ShapeDtypeStruct((dq, n), BF16)),
        compiler_params=_cparams(("parallel",)),
        name="ml_post",
    )(hn, xc, z, h, g_norm, skip, w_down, g_ffn, w_qt)


NCAND_ROWS = 72


def _first_max(vals, ids, big):
    m = jnp.max(vals, axis=0, keepdims=True)
    sel = jnp.min(jnp.where(vals == m, ids, big), axis=0, keepdims=True)
    return m, sel


def _peer_route_kernel(qt_ref, keys_ref, r2_ref, e2_ref, n_ref, e1_ref, s_scr, rank_scr, top_scr, *, tk):
    nk, k16 = PEER_NKEYS, PEER_TOPK
    key_id = lax.broadcasted_iota(jnp.int32, (nk, tk), 0)
    slot_id = lax.broadcasted_iota(jnp.int32, (k16, tk), 0)

    def half_body(hc, carry):
        start = pl.multiple_of(hc * PEER_HALF, PEER_HALF)
        s = _dot(keys_ref[hc], qt_ref[pl.ds(start, PEER_HALF), :])
        s_scr[hc] = s

        def pick(r, st):
            s_cur, rank, top = st
            m, sel = _first_max(s_cur, key_id, nk)
            hit = key_id == sel
            return (jnp.where(hit, NEG_INF, s_cur), jnp.where(hit, r.astype(F32), rank),
                    jnp.where(slot_id == r, m, top))

        _, rank, top = lax.fori_loop(
            0, k16, pick, (s, jnp.full((nk, tk), float(k16), F32), jnp.zeros((k16, tk), F32)))
        rank_scr[hc] = rank
        top_scr[hc] = top
        return carry

    lax.fori_loop(0, 2 * PEER_HEADS, half_body, 0)

    def cand_ids():
        i16 = lax.broadcasted_iota(jnp.int32, (16, tk), 0)
        i8 = lax.broadcasted_iota(jnp.int32, (8, tk), 0)
        parts = [i16, 16 + i8, 32 + i8, 48 + i8, 16 * i16, 16 * i8 + 1, 16 * i8 + 2]
        return jnp.concatenate(parts, axis=0)

    cand_id = cand_ids()
    row72 = lax.broadcasted_iota(jnp.int32, (NCAND_ROWS, tk), 0)
    cand_ok = (row72 < 40) | (cand_id >= 64)
    big_id = 16 * 16

    def head_body(h, carry):
        a = top_scr[2 * h]
        b = top_scr[2 * h + 1]
        parts = [a[0:1, :] + b, a[1:2, :] + b[0:8, :], a[2:3, :] + b[0:8, :], a[3:4, :] + b[0:8, :],
                 a + b[0:1, :], a[0:8, :] + b[1:2, :], a[0:8, :] + b[2:3, :]]
        cand = jnp.where(cand_ok, jnp.concatenate(parts, axis=0), NEG_INF)

        def pick(r, st):
            cand_cur, cnt, tsel = st
            m, sel = _first_max(cand_cur, cand_id, big_id)
            cand_cur = jnp.where(cand_id == sel, NEG_INF, cand_cur)
            cnt = cnt + (slot_id == (sel >> 4)).astype(F32)
            return cand_cur, cnt, jnp.where(slot_id == r, m, tsel)

        _, cnt, tsel = lax.fori_loop(
            0, k16, pick, (cand, jnp.zeros((k16, tk), F32), jnp.zeros((k16, tk), F32)))
        z = jnp.sum(jnp.exp(tsel - tsel[0:1, :]), axis=0, keepdims=True)
        rank1 = rank_scr[2 * h]
        rank2 = rank_scr[2 * h + 1]
        def spread(r, acc):
            row = jnp.sum(jnp.where(slot_id == r, cnt, 0.0), axis=0, keepdims=True)
            return jnp.where(rank1 == r.astype(F32), row, acc)

        n_sel = lax.fori_loop(0, k16, spread, jnp.zeros((nk, tk), F32))
        n_ref[h] = n_sel
        e1_ref[h] = jnp.where(rank1 < k16, jnp.exp(s_scr[2 * h] - a[0:1, :]) / z, 0.0)
        r2_ref[h] = rank2
        e2_ref[h] = jnp.where(rank2 < k16, jnp.exp(s_scr[2 * h + 1] - b[0:1, :]), 0.0)
        return carry

    lax.fori_loop(0, PEER_HEADS, head_body, 0)


def _peer_route(qt, keys):
    dq, n = qt.shape
    tk = _pick_block(n, (256, 128))
    nk = PEER_NKEYS
    shp = jax.ShapeDtypeStruct((PEER_HEADS, nk, n), F32)
    ospec = pl.BlockSpec((PEER_HEADS, nk, tk), lambda i: (0, 0, i))
    return pl.pallas_call(
        functools.partial(_peer_route_kernel, tk=tk),
        grid=(n // tk,),
        in_specs=[pl.BlockSpec((dq, tk), lambda i: (0, i)), _full(keys.shape)],
        out_specs=(ospec, ospec, ospec, ospec),
        out_shape=(shp, shp, shp, shp),
        scratch_shapes=[pltpu.VMEM((2 * PEER_HEADS, nk, tk), F32), pltpu.VMEM((2 * PEER_HEADS, nk, tk), F32),
                        pltpu.VMEM((2 * PEER_HEADS, PEER_TOPK, tk), F32)],
        compiler_params=_cparams(("parallel",)),
        name="peer_route",
    )(qt, keys)


def _peer_expert_kernel(xt_ref, r2_ref, e2_ref, n_ref, e1_ref, u_ref, vt_ref, out_ref):
    j = pl.program_id(1)
    nk = PEER_NKEYS
    act = _dot(u_ref[...], xt_ref[...])
    ge = _gelu(act)
    parts = []
    for al in range(PEER_EXPERT_TILE // nk):
        a = j * (PEER_EXPERT_TILE // nk) + al
        g = None
        for h in range(PEER_HEADS):
            n_row = n_ref[h, pl.ds(a, 1), :]
            e1_row = e1_ref[h, pl.ds(a, 1), :]
            term = jnp.where(r2_ref[h] < n_row, e2_ref[h], 0.0) * e1_row
            g = term if g is None else g + term
        parts.append(g)
    gate = jnp.concatenate(parts, axis=0)
    coef = (gate * ge).astype(BF16)
    contrib = _dot(vt_ref[...], coef)

    @pl.when(j == 0)
    def _():
        out_ref[...] = contrib

    @pl.when(j > 0)
    def _():
        out_ref[...] += contrib


def _peer_expert(xt, r2, e2, nsel, e1, u, vt):
    d, n = xt.shape
    ne = u.shape[0]
    t = _pick_block(n, (512, 256, 128))
    et = PEER_EXPERT_TILE
    rspec = pl.BlockSpec((PEER_HEADS, PEER_NKEYS, t), lambda i, j: (0, 0, i))
    return pl.pallas_call(
        _peer_expert_kernel,
        grid=(n // t, ne // et),
        in_specs=[pl.BlockSpec((d, t), lambda i, j: (0, i)), rspec, rspec, rspec, rspec,
                  pl.BlockSpec((et, d), lambda i, j: (j, 0)), pl.BlockSpec((d, et), lambda i, j: (0, j))],
        out_specs=pl.BlockSpec((d, t), lambda i, j: (0, i)),
        out_shape=jax.ShapeDtypeStruct((d, n), F32),
        compiler_params=_cparams(("parallel", "arbitrary")),
        name="peer_expert",
    )(xt, r2, e2, nsel, e1, u, vt)


def _ple_kernel(ot_ref, h1_ref, p_ref, gple_ref, wgate_ref, wproj_ref, gnext_ref, h_ref, y_ref):
    h2 = h1_ref[...] + ot_ref[...].T
    gate = jax.nn.sigmoid(_dot(_rms(h2, gple_ref[...]).astype(BF16), wgate_ref[...]))
    h3 = h2 + gate * _dot(p_ref[...].astype(BF16), wproj_ref[...])
    h_ref[...] = h3
    y_ref[...] = _rms(h3, gnext_ref[...])


def _ple(out_t, h1, p, g_ple, w_gate, w_proj, g_next):
    n, d = h1.shape
    tb = _pick_block(n, (512, 256, 128))
    return pl.pallas_call(
        _ple_kernel,
        grid=(n // tb,),
        in_specs=[pl.BlockSpec((d, tb), lambda i: (0, i)), _tok(tb, d), _tok(tb, p.shape[1]),
                  _full(g_ple.shape), _full(w_gate.shape), _full(w_proj.shape), _full(g_next.shape)],
        out_specs=(_tok(tb, d), _tok(tb, d)),
        out_shape=(jax.ShapeDtypeStruct((n, d), F32), jax.ShapeDtypeStruct((n, d), F32)),
        compiler_params=_cparams(("parallel",)),
        name="ple",
    )(out_t, h1, p, g_ple, w_gate, w_proj, g_next)


def _block_diag_tiles(w):
    g, bi, bo = w.shape
    per = MXU_TILE // bi
    wt = w.reshape(g // per, per, bi, bo)
    eye = jnp.eye(per, dtype=w.dtype)
    dense = jnp.einsum("tgio,gh->tgiho", wt, eye)
    return dense.reshape(g // per, per * bi, per * bo)


def _row(x):
    return x.reshape(1, -1)


def _prep_weights(W, depth):
    P = {}
    n_gla = W["w_gla_in"].shape[0]
    dk = W["w_gla_gate"].shape[2]
    rank = W["w_gla_gate"].shape[1]
    dv = (W["w_gla_in"].shape[2] - rank - 2 * dk) // 2
    P["gla_dims"] = (dk, dv)
    pad = 128 - rank
    P["w_gla_qkvr"] = W["w_gla_in"][:, :, :2 * dk + 2 * dv].astype(BF16)
    P["w_gla_gl"] = jnp.pad(W["w_gla_in"][:, :, 2 * dk + 2 * dv:], ((0, 0), (0, 0), (0, pad))).astype(BF16)
    P["w_gla_gate"] = jnp.pad(W["w_gla_gate"], ((0, 0), (0, pad), (0, 0))).astype(BF16)
    P["w_gla_out"] = W["w_gla_out"].astype(BF16)
    P["w_ml_up"] = W["w_ml_up"].astype(BF16)
    n_ml = W["w_ml_up"].shape[0]
    for nm in ("w_ml_q", "w_ml_k", "w_ml_v"):
        P[nm] = jnp.stack([_block_diag_tiles(W[nm][j]) for j in range(n_ml)]).astype(BF16)
    wif = jnp.concatenate([W["w_ml_igate"], W["w_ml_fgate"]], axis=-1)
    P["w_ml_if"] = jnp.pad(wif, ((0, 0), (0, 0), (0, 128 - wif.shape[-1]))).astype(BF16)
    bif = jnp.concatenate([W["b_ml_igate"], W["b_ml_fgate"]], axis=-1)
    P["b_ml_if"] = jnp.pad(bif, ((0, 0), (0, 128 - bif.shape[-1])))
    P["w_ml_down"] = W["w_ml_down"].astype(BF16)
    P["w_peer_qt"] = jnp.swapaxes(W["w_peer_query"], 1, 2).astype(BF16)
    keys = W["peer_keys"]
    P["peer_keys"] = jnp.swapaxes(keys, 1, 2).reshape(depth, 2 * PEER_HEADS, PEER_NKEYS, PEER_HALF).astype(BF16)
    P["peer_u"] = W["peer_u"].astype(BF16)
    P["peer_vt"] = jnp.swapaxes(W["peer_v"], 1, 2).astype(BF16)
    P["w_ple_gate"] = W["w_ple_gate"].astype(BF16)
    P["w_ple_proj"] = W["w_ple_proj"].astype(BF16)
    del n_gla
    return P


def _run_group(x, p, s_gla, c_ml, n_ml, m_ml, buf_ml, W, P, depth):
    bsz, seqlen, d = x.shape
    n = bsz * seqlen
    h = x.reshape(n, d)
    dk, dv = P["gla_dims"]
    new_s, new_c, new_n, new_m, new_buf = [], [], [], [], []
    y = None
    for i in range(depth):
        j = i // 2
        if i % 2 == 0:
            q, k, v, r, lg = _gla_pre(h, _row(W["norm_mix"][i]), P["w_gla_qkvr"][j], P["w_gla_gl"][j],
                                      P["w_gla_gate"][j], _row(W["b_gla_gate"][j]), dk, dv)
            s0 = None if s_gla is None else s_gla[j]
            o, s_fin = _gla_scan(q, k, v, lg, _row(W["g_gla_norm"][j]), s0, bsz, seqlen)
            new_s.append(s_fin)
            h1, xt, qt = _gla_post(o, r, h, P["w_gla_out"][j], _row(W["norm_ffn"][i]), P["w_peer_qt"][i])
        else:
            xm, z = _ml_pre(h, _row(W["norm_mix"][i]), P["w_ml_up"][j])
            di = xm.shape[1]
            if buf_ml is None:
                buf = jnp.zeros((bsz, HALO, di), F32)
            else:
                buf = jnp.pad(buf_ml[j], ((0, 0), (HALO - (ML_CONV - 1), 0), (0, 0)))
            qm, km, vm, xc, gates = _ml_qkv(xm, buf, W["ml_conv_w"][j], _row(W["ml_conv_b"][j]),
                                            P["w_ml_q"][j], P["w_ml_k"][j], P["w_ml_v"][j],
                                            P["w_ml_if"][j], _row(P["b_ml_if"][j]), bsz, seqlen)
            if c_ml is None:
                state = None
            else:
                state = (c_ml[j], n_ml[j][:, :, None, :],
                         jnp.broadcast_to(m_ml[j][:, :, None, None], m_ml[j].shape + (1, M_LANES)))
            hn, c_fin, n_fin, m_fin = _ml_scan(qm, km, vm, gates, state, bsz, seqlen)
            new_c.append(c_fin)
            new_n.append(n_fin[:, :, 0, :])
            new_m.append(m_fin[:, :, 0, 0])
            xm3 = xm.reshape(bsz, seqlen, di)
            if seqlen >= ML_CONV - 1:
                new_buf.append(xm3[:, seqlen - (ML_CONV - 1):, :])
            else:
                new_buf.append(jnp.concatenate([buf[:, HALO - (ML_CONV - 1):, :], xm3], axis=1)[:, -(ML_CONV - 1):, :])
            h1, xt, qt = _ml_post(hn, xc, z, h, _row(W["g_ml_norm"][j]), _row(W["ml_skip"][j]),
                                  P["w_ml_down"][j], _row(W["norm_ffn"][i]), P["w_peer_qt"][i])
        r2, e2, nsel, e1 = _peer_route(qt, P["peer_keys"][i])
        out_t = _peer_expert(xt, r2, e2, nsel, e1, P["peer_u"][i], P["peer_vt"][i])
        g_next = W["norm_final"] if i == depth - 1 else W["norm_mix"][i + 1]
        h, y = _ple(out_t, h1, p[i].reshape(n, -1), _row(W["norm_ple"][i]), P["w_ple_gate"][i],
                    P["w_ple_proj"][i], _row(g_next))
    return (y.reshape(bsz, seqlen, d), jnp.stack(new_s), jnp.stack(new_c), jnp.stack(new_n),
            jnp.stack(new_m), jnp.stack(new_buf))


def kernel(x_prompt, x_sample, state_gla_S, state_mlstm_C, state_mlstm_n, state_mlstm_m, state_mlstm_conv,
           p_prompt, p_sample, w_gla_in, w_gla_gate, b_gla_gate, g_gla_norm, w_gla_out,
           w_ml_up, ml_conv_w, ml_conv_b, w_ml_q, w_ml_k, w_ml_v, w_ml_igate, b_ml_igate,
           w_ml_fgate, b_ml_fgate, g_ml_norm, ml_skip, w_ml_down,
           w_peer_query, peer_keys, peer_u, peer_v, norm_mix, norm_ffn, norm_ple,
           w_ple_gate, w_ple_proj, norm_final):
    W = dict(w_gla_in=w_gla_in, w_gla_gate=w_gla_gate, b_gla_gate=b_gla_gate, g_gla_norm=g_gla_norm,
             w_gla_out=w_gla_out, w_ml_up=w_ml_up, ml_conv_w=ml_conv_w, ml_conv_b=ml_conv_b,
             w_ml_q=w_ml_q, w_ml_k=w_ml_k, w_ml_v=w_ml_v, w_ml_igate=w_ml_igate, b_ml_igate=b_ml_igate,
             w_ml_fgate=w_ml_fgate, b_ml_fgate=b_ml_fgate, g_ml_norm=g_ml_norm, ml_skip=ml_skip,
             w_ml_down=w_ml_down, w_peer_query=w_peer_query, peer_keys=peer_keys, peer_u=peer_u,
             peer_v=peer_v, norm_mix=norm_mix, norm_ffn=norm_ffn, norm_ple=norm_ple,
             w_ple_gate=w_ple_gate, w_ple_proj=w_ple_proj, norm_final=norm_final)
    depth = norm_mix.shape[0]
    P = _prep_weights(W, depth)
    y_p, s_p, c_p, n_p, m_p, buf_p = _run_group(x_prompt, p_prompt, None, None, None, None, None, W, P, depth)
    y_s, s_s, c_s, n_s, m_s, buf_s = _run_group(x_sample, p_sample, state_gla_S, state_mlstm_C,
                                                state_mlstm_n, state_mlstm_m, state_mlstm_conv, W, P, depth)
    return (y_p, y_s, s_p, s_s, c_p, c_s, n_p, n_s, m_p, m_s, buf_p, buf_s)
```

```python
import functools

import jax
import jax.numpy as jnp
from jax import lax
from jax.experimental import pallas as pl
from jax.experimental.pallas import tpu as pltpu

F32 = jnp.float32
BF16 = jnp.bfloat16
EPS = 1e-6
CHUNK = 64
HIGHEST = lax.Precision.HIGHEST
NEG_INF = float("-inf")

GLA_HEADS = 4
GLA_GATE_NORMALIZER = 16.0
ML_HEADS = 4
ML_CONV = 4
ML_QKV_BLOCK = 4
PEER_HEADS = 8
PEER_NKEYS = 128
PEER_HALF = 64
PEER_TOPK = 16

VMEM_LIMIT_BYTES = 52 * 1024 * 1024
MXU_TILE = 256
PEER_EXPERT_HALF = 512
GELU_ARG_SCALE = 0.7071067811865476
GELU_OUT_SCALE = 0.5 / GELU_ARG_SCALE


def _cparams(sem):
    return pltpu.CompilerParams(dimension_semantics=sem, vmem_limit_bytes=VMEM_LIMIT_BYTES)


def _pick_block(n, candidates):
    for c in candidates:
        if n % c == 0:
            return c
    raise ValueError(f"no block size in {candidates} divides {n}")


def _rms(x, g):
    ms = jnp.mean(x * x, axis=-1, keepdims=True)
    return x * lax.rsqrt(ms + EPS) * g


def _log_sigmoid(x):
    return jnp.minimum(x, 0.0) - jnp.log1p(jnp.exp(-jnp.abs(x)))


def _silu(x):
    return x * jax.nn.sigmoid(x)


def _dot(a, b):
    return jnp.dot(a, b, preferred_element_type=F32)


def _dot_nt(a, b):
    return lax.dot_general(a, b, (((1,), (1,)), ((), ())), preferred_element_type=F32)


def _dot_tn(a, b):
    return lax.dot_general(a, b, (((0,), (0,)), ((), ())), preferred_element_type=F32)


def _tok(tb, d):
    return pl.BlockSpec((tb, d), lambda i: (i, 0))


def _full(shape):
    nd = len(shape)
    return pl.BlockSpec(shape, lambda *_: (0,) * nd)


def _gla_pre_kernel(h_ref, g_ref, w_ref, wgl_ref, wgate_ref, bgate_ref,
                    q_ref, k_ref, v_ref, r_ref, lg_ref, *, dk, dv, hk):
    xn = _rms(h_ref[...], g_ref[...]).astype(BF16)
    proj = _dot(xn, w_ref[...])
    q_ref[...] = proj[:, :dk] * (hk ** -0.5)
    k_ref[...] = proj[:, dk:2 * dk]
    v_ref[...] = proj[:, 2 * dk:2 * dk + dv].astype(BF16)
    r_ref[...] = proj[:, 2 * dk + dv:]
    gl = _dot(xn, wgl_ref[...])
    gate = _dot(gl.astype(BF16), wgate_ref[...]) + bgate_ref[...]
    lg_ref[...] = _log_sigmoid(gate) * (1.0 / GLA_GATE_NORMALIZER)


def _gla_pre(h, g, w_qkvr, w_gl, w_gate, b_gate, dk, dv):
    n, d = h.shape
    tb = _pick_block(n, (512, 256, 128, 64))
    hk = dk // GLA_HEADS
    outs = (jax.ShapeDtypeStruct((n, dk), F32), jax.ShapeDtypeStruct((n, dk), F32),
            jax.ShapeDtypeStruct((n, dv), BF16), jax.ShapeDtypeStruct((n, dv), F32),
            jax.ShapeDtypeStruct((n, dk), F32))
    return pl.pallas_call(
        functools.partial(_gla_pre_kernel, dk=dk, dv=dv, hk=hk),
        grid=(n // tb,),
        in_specs=[_tok(tb, d), _full(g.shape), _full(w_qkvr.shape), _full(w_gl.shape),
                  _full(w_gate.shape), _full(b_gate.shape)],
        out_specs=(_tok(tb, dk), _tok(tb, dk), _tok(tb, dv), _tok(tb, dv), _tok(tb, dk)),
        out_shape=outs,
        compiler_params=_cparams(("parallel",)),
        name="gla_pre",
    )(h, g, w_qkvr, w_gl, w_gate, b_gate)


GLA_SUB = 16


def _gla_scan_kernel(*refs, nchunks, zero_init):
    if zero_init:
        q_ref, k_ref, v_ref, lg_ref, gn_ref, o_ref, sout_ref, st_ref = refs
        s0_ref = None
    else:
        q_ref, k_ref, v_ref, lg_ref, gn_ref, s0_ref, o_ref, sout_ref, st_ref = refs
    t = pl.program_id(2)

    @pl.when(t == 0)
    def _():
        if zero_init:
            st_ref[...] = jnp.zeros_like(st_ref)
        else:
            st_ref[...] = s0_ref[...].T

    row = lax.broadcasted_iota(jnp.int32, (CHUNK, CHUNK), 0)
    col = lax.broadcasted_iota(jnp.int32, (CHUNK, CHUNK), 1)
    tril = (row >= col).astype(F32)
    gn = gn_ref[...]
    nsub = CHUNK // GLA_SUB

    for c in range(nchunks):
        sl = pl.ds(c * CHUNK, CHUNK)
        lg = lg_ref[sl, :]
        q = q_ref[sl, :]
        k = k_ref[sl, :]
        v = v_ref[sl, :]
        st = st_ref[...]
        b = jnp.dot(tril, lg, precision=HIGHEST, preferred_element_type=F32)
        b_last = b[CHUNK - 1:CHUNK, :]
        inter = _dot_nt((q * jnp.exp(b)).astype(BF16), st.astype(BF16))
        intra_parts = []
        for i in range(nsub):
            lo, hi = i * GLA_SUB, (i + 1) * GLA_SUB
            b_ref_row = b[lo:lo + 1, :]
            qe = (q[lo:hi, :] * jnp.exp(b[lo:hi, :] - b_ref_row)).astype(BF16)
            ke = (k[:hi, :] * jnp.exp(b_ref_row - b[:hi, :])).astype(BF16)
            att = _dot_nt(qe, ke)
            r_i = lax.broadcasted_iota(jnp.int32, (GLA_SUB, hi), 0) + lo
            c_i = lax.broadcasted_iota(jnp.int32, (GLA_SUB, hi), 1)
            att = jnp.where(r_i >= c_i, att, 0.0)
            intra_parts.append(_dot(att.astype(BF16), v[:hi, :]))
        o = inter + jnp.concatenate(intra_parts, axis=0)
        kd = (k * jnp.exp(b_last - b)).astype(BF16)
        st_ref[...] = st * jnp.exp(b_last) + _dot_tn(v, kd)
        o = o * lax.rsqrt(jnp.mean(o * o, axis=-1, keepdims=True) + EPS) * gn
        o_ref[sl, :] = o

    @pl.when(t == pl.num_programs(2) - 1)
    def _():
        sout_ref[...] = st_ref[...].T


def _gla_scan(q, k, v, lg, gn, s0, nseq, seqlen):
    n, dk = q.shape
    dv = v.shape[1]
    hk, hv = dk // GLA_HEADS, dv // GLA_HEADS
    tt = _pick_block(seqlen, (512, 256, 128, 64))
    nt = seqlen // tt
    zero_init = s0 is None
    tmap = lambda s, h, t: (s * nt + t, h)
    smap = lambda s, h, t: (s, h, 0, 0)
    in_specs = [pl.BlockSpec((tt, hk), tmap), pl.BlockSpec((tt, hk), tmap),
                pl.BlockSpec((tt, hv), tmap), pl.BlockSpec((tt, hk), tmap),
                pl.BlockSpec((1, hv), lambda s, h, t: (0, 0))]
    args = [q, k, v, lg, gn]
    if not zero_init:
        in_specs.append(pl.BlockSpec((None, None, hk, hv), smap))
        args.append(s0)
    return pl.pallas_call(
        functools.partial(_gla_scan_kernel, nchunks=tt // CHUNK, zero_init=zero_init),
        grid=(nseq, GLA_HEADS, nt),
        in_specs=in_specs,
        out_specs=(pl.BlockSpec((tt, hv), tmap), pl.BlockSpec((None, None, hk, hv), smap)),
        out_shape=(jax.ShapeDtypeStruct((n, dv), F32),
                   jax.ShapeDtypeStruct((nseq, GLA_HEADS, hk, hv), F32)),
        scratch_shapes=[pltpu.VMEM((hv, hk), F32)],
        compiler_params=_cparams(("parallel", "parallel", "arbitrary")),
        name="gla_scan",
    )(*args)


def _peer_pre_tail(h1, gffn_ref, wqt_ref, xt_ref, qt_ref):
    xn = _rms(h1, gffn_ref[...])
    xt = xn.T.astype(BF16)
    xt_ref[...] = xt
    qt_ref[...] = _dot(wqt_ref[...], xt).astype(BF16)


def _gla_post_kernel(o_ref, r_ref, h_ref, wout_ref, gffn_ref, wqt_ref, h1_ref, xt_ref, qt_ref):
    y = (o_ref[...] * _silu(r_ref[...])).astype(BF16)
    h1 = h_ref[...] + _dot(y, wout_ref[...])
    h1_ref[...] = h1
    _peer_pre_tail(h1, gffn_ref, wqt_ref, xt_ref, qt_ref)


def _gla_post(o, r, h, w_out, g_ffn, w_qt):
    n, d = h.shape
    tb = _pick_block(n, (512, 256, 128))
    dq = w_qt.shape[0]
    return pl.pallas_call(
        _gla_post_kernel,
        grid=(n // tb,),
        in_specs=[_tok(tb, o.shape[1]), _tok(tb, r.shape[1]), _tok(tb, d), _full(w_out.shape),
                  _full(g_ffn.shape), _full(w_qt.shape)],
        out_specs=(_tok(tb, d), pl.BlockSpec((d, tb), lambda i: (0, i)),
                   pl.BlockSpec((dq, tb), lambda i: (0, i))),
        out_shape=(jax.ShapeDtypeStruct((n, d), F32), jax.ShapeDtypeStruct((d, n), BF16),
                   jax.ShapeDtypeStruct((dq, n), BF16)),
        compiler_params=_cparams(("parallel",)),
        name="gla_post",
    )(o, r, h, w_out, g_ffn, w_qt)


def _ml_pre_kernel(h_ref, g_ref, w_ref, xm_ref, z_ref, *, di):
    xn = _rms(h_ref[...], g_ref[...]).astype(BF16)
    up = _dot(xn, w_ref[...])
    xm_ref[...] = up[:, :di]
    z_ref[...] = up[:, di:]


def _ml_pre(h, g, w_up):
    n, d = h.shape
    di = w_up.shape[1] // 2
    tb = _pick_block(n, (512, 256, 128, 64))
    return pl.pallas_call(
        functools.partial(_ml_pre_kernel, di=di),
        grid=(n // tb,),
        in_specs=[_tok(tb, d), _full(g.shape), _full(w_up.shape)],
        out_specs=(_tok(tb, di), _tok(tb, di)),
        out_shape=(jax.ShapeDtypeStruct((n, di), F32), jax.ShapeDtypeStruct((n, di), F32)),
        compiler_params=_cparams(("parallel",)),
        name="ml_pre",
    )(h, g, w_up)


HALO = 8


def _ml_qkv_kernel(xm_ref, prev_ref, buf_ref, cw_ref, cb_ref, wq_ref, wk_ref, wv_ref, wif_ref, bif_ref,
                   q_ref, k_ref, v_ref, xc_ref, gates_ref, xp_scr, *, tt, di, hd):
    t = pl.program_id(1)
    x = xm_ref[...]
    halo = jnp.where(t == 0, buf_ref[...], prev_ref[...])
    xp_scr[0:HALO, :] = halo
    xp_scr[HALO:HALO + tt, :] = x
    y = cb_ref[...]
    for j in range(ML_CONV):
        y = y + cw_ref[j:j + 1, :] * xp_scr[pl.ds(HALO - (ML_CONV - 1) + j, tt), :]
    xc = _silu(y)
    xc_ref[...] = xc
    xcb = xc.astype(BF16)
    xmb = x.astype(BF16)
    gates = jnp.zeros((tt, gates_ref.shape[1]), F32) + bif_ref[...]
    ntile = di // MXU_TILE
    for j in range(ntile):
        cs = slice(j * MXU_TILE, (j + 1) * MXU_TILE)
        qj = _dot(xcb[:, cs], wq_ref[j])
        kj = _dot(xcb[:, cs], wk_ref[j])
        vj = _dot(xmb[:, cs], wv_ref[j])
        q_ref[:, cs] = qj.astype(BF16)
        k_ref[:, cs] = kj * (hd ** -0.5)
        v_ref[:, cs] = vj.astype(BF16)
        gates = gates + _dot(qj.astype(BF16), wif_ref[j * MXU_TILE:(j + 1) * MXU_TILE, :])
        gates = gates + _dot(kj.astype(BF16), wif_ref[di + j * MXU_TILE:di + (j + 1) * MXU_TILE, :])
        gates = gates + _dot(vj.astype(BF16), wif_ref[2 * di + j * MXU_TILE:2 * di + (j + 1) * MXU_TILE, :])
    lane = lax.broadcasted_iota(jnp.int32, gates.shape, 1)
    is_f = (lane >= ML_HEADS) & (lane < 2 * ML_HEADS)
    gates_ref[...] = jnp.where(is_f, _log_sigmoid(gates), gates)


def _ml_qkv(xm, buf, conv_w, conv_b, wq, wk, wv, wif, bif, nseq, seqlen):
    n, di = xm.shape
    hd = di // ML_HEADS
    tt = _pick_block(seqlen, (256, 128, 64))
    nt = seqlen // tt
    tmap = lambda s, t: (s * nt + t, 0)
    prev_map = lambda s, t: (jnp.maximum((s * nt + t) * (tt // HALO) - 1, 0), 0)
    ng = wif.shape[1]
    return pl.pallas_call(
        functools.partial(_ml_qkv_kernel, tt=tt, di=di, hd=hd),
        grid=(nseq, nt),
        in_specs=[pl.BlockSpec((tt, di), tmap), pl.BlockSpec((HALO, di), prev_map),
                  pl.BlockSpec((None, HALO, di), lambda s, t: (s, 0, 0)),
                  _full(conv_w.shape), _full(conv_b.shape), _full(wq.shape), _full(wk.shape),
                  _full(wv.shape), _full(wif.shape), _full(bif.shape)],
        out_specs=(pl.BlockSpec((tt, di), tmap), pl.BlockSpec((tt, di), tmap),
                   pl.BlockSpec((tt, di), tmap), pl.BlockSpec((tt, di), tmap),
                   pl.BlockSpec((tt, ng), tmap)),
        out_shape=(jax.ShapeDtypeStruct((n, di), BF16), jax.ShapeDtypeStruct((n, di), F32),
                   jax.ShapeDtypeStruct((n, di), BF16), jax.ShapeDtypeStruct((n, di), F32),
                   jax.ShapeDtypeStruct((n, ng), F32)),
        scratch_shapes=[pltpu.VMEM((HALO + tt, di), F32)],
        compiler_params=_cparams(("parallel", "arbitrary")),
        name="ml_qkv",
    )(xm, xm, buf, conv_w, conv_b, wq, wk, wv, wif, bif)


def _lane_pick(x, idx):
    lane = lax.broadcasted_iota(jnp.int32, x.shape, 1)
    return jnp.sum(jnp.where(lane == idx, x, 0.0), axis=-1, keepdims=True)


def _ml_scan_kernel(*refs, nchunks, zero_init):
    if zero_init:
        q_ref, k_ref, v_ref, g_ref, h_ref, cout_ref, nout_ref, mout_ref, c_scr, n_scr, m_scr = refs
    else:
        (q_ref, k_ref, v_ref, g_ref, c0_ref, n0_ref, m0_ref,
         h_ref, cout_ref, nout_ref, mout_ref, c_scr, n_scr, m_scr) = refs
    head = pl.program_id(1)
    t = pl.program_id(2)

    @pl.when(t == 0)
    def _():
        if zero_init:
            c_scr[...] = jnp.zeros_like(c_scr)
            n_scr[...] = jnp.zeros_like(n_scr)
            m_scr[...] = jnp.zeros_like(m_scr)
        else:
            c_scr[...] = c0_ref[...]
            n_scr[...] = n0_ref[...]
            m_scr[...] = m0_ref[...]

    row = lax.broadcasted_iota(jnp.int32, (CHUNK, CHUNK), 0)
    col = lax.broadcasted_iota(jnp.int32, (CHUNK, CHUNK), 1)
    causal = row >= col
    tril = causal.astype(F32)
    eye = row == col
    ones = jnp.ones((CHUNK, CHUNK), F32)

    for c in range(nchunks):
        sl = pl.ds(c * CHUNK, CHUNK)
        q = q_ref[sl, :]
        kf = k_ref[sl, :]
        v = v_ref[sl, :]
        g = g_ref[sl, :]
        cmat = c_scr[...]
        nvec = n_scr[...]
        m = m_scr[0:1, 0:1]
        gcum = jnp.dot(tril, g, precision=HIGHEST, preferred_element_type=F32)
        i_col = _lane_pick(g, head)
        f_col = _lane_pick(gcum, head + ML_HEADS)
        w_row = jnp.dot(ones, jnp.where(eye, i_col - f_col, 0.0), precision=HIGHEST,
                        preferred_element_type=F32)
        log_d = jnp.where(causal, f_col + w_row, NEG_INF)
        inter = f_col + m
        mt = jnp.maximum(inter, jnp.max(log_d, axis=-1, keepdims=True))
        sc = _dot_nt(q, kf.astype(BF16)) * jnp.exp(log_d - mt)
        a = jnp.exp(inter - mt)
        num = a * _dot(q, cmat.astype(BF16)) + _dot(sc.astype(BF16), v)
        qn = jnp.sum(q.astype(F32) * nvec, axis=-1, keepdims=True)
        den = a * qn + jnp.sum(sc, axis=-1, keepdims=True)
        hh = num / jnp.maximum(jnp.abs(den), jnp.exp(-mt))
        m_new = mt[CHUNK - 1:CHUNK, :]
        f_last = f_col[CHUNK - 1:CHUNK, :]
        w_end = jnp.exp(f_last - f_col + i_col - m_new)
        a_end = jnp.exp(f_last + m - m_new)
        kw = kf * w_end
        c_scr[...] = a_end * cmat + _dot_tn(kw.astype(BF16), v)
        n_scr[...] = a_end * nvec + jnp.sum(kw, axis=0, keepdims=True)
        m_scr[...] = jnp.broadcast_to(m_new, m_scr.shape)
        mu = jnp.mean(hh, axis=-1, keepdims=True)
        var = jnp.mean(jnp.square(hh - mu), axis=-1, keepdims=True)
        h_ref[sl, :] = (hh - mu) * lax.rsqrt(var + EPS)

    @pl.when(t == pl.num_programs(2) - 1)
    def _():
        cout_ref[...] = c_scr[...]
        nout_ref[...] = n_scr[...]
        mout_ref[...] = m_scr[...]


M_LANES = 128


def _ml_scan(q, k, v, gates, state, nseq, seqlen):
    n, di = q.shape
    hd = di // ML_HEADS
    tt = _pick_block(seqlen, (256, 128, 64))
    nt = seqlen // tt
    zero_init = state is None
    tmap = lambda s, h, t: (s * nt + t, h)
    gmap = lambda s, h, t: (s * nt + t, 0)
    smap = lambda s, h, t: (s, h, 0, 0)
    ng = gates.shape[1]
    in_specs = [pl.BlockSpec((tt, hd), tmap), pl.BlockSpec((tt, hd), tmap),
                pl.BlockSpec((tt, hd), tmap), pl.BlockSpec((tt, ng), gmap)]
    args = [q, k, v, gates]
    if not zero_init:
        in_specs += [pl.BlockSpec((None, None, hd, hd), smap), pl.BlockSpec((None, None, 1, hd), smap),
                     pl.BlockSpec((None, None, 1, M_LANES), smap)]
        args += list(state)
    return pl.pallas_call(
        functools.partial(_ml_scan_kernel, nchunks=tt // CHUNK, zero_init=zero_init),
        grid=(nseq, ML_HEADS, nt),
        in_specs=in_specs,
        out_specs=(pl.BlockSpec((tt, hd), tmap), pl.BlockSpec((None, None, hd, hd), smap),
                   pl.BlockSpec((None, None, 1, hd), smap), pl.BlockSpec((None, None, 1, M_LANES), smap)),
        out_shape=(jax.ShapeDtypeStruct((n, di), F32),
                   jax.ShapeDtypeStruct((nseq, ML_HEADS, hd, hd), F32),
                   jax.ShapeDtypeStruct((nseq, ML_HEADS, 1, hd), F32),
                   jax.ShapeDtypeStruct((nseq, ML_HEADS, 1, M_LANES), F32)),
        scratch_shapes=[pltpu.VMEM((hd, hd), F32), pltpu.VMEM((1, hd), F32), pltpu.VMEM((1, M_LANES), F32)],
        compiler_params=_cparams(("parallel", "parallel", "arbitrary")),
        name="ml_scan",
    )(*args)


def _ml_post_kernel(hn_ref, xc_ref, z_ref, h_ref, gn_ref, skip_ref, wdown_ref, gffn_ref, wqt_ref,
                    h1_ref, xt_ref, qt_ref):
    y = (hn_ref[...] * gn_ref[...] + skip_ref[...] * xc_ref[...]) * _silu(z_ref[...])
    h1 = h_ref[...] + _dot(y.astype(BF16), wdown_ref[...])
    h1_ref[...] = h1
    _peer_pre_tail(h1, gffn_ref, wqt_ref, xt_ref, qt_ref)


def _ml_post(hn, xc, z, h, g_norm, skip, w_down, g_ffn, w_qt):
    n, d = h.shape
    di = hn.shape[1]
    tb = _pick_block(n, (512, 256, 128))
    dq = w_qt.shape[0]
    return pl.pallas_call(
        _ml_post_kernel,
        grid=(n // tb,),
        in_specs=[_tok(tb, di), _tok(tb, di), _tok(tb, di), _tok(tb, d), _full(g_norm.shape),
                  _full(skip.shape), _full(w_down.shape), _full(g_ffn.shape), _full(w_qt.shape)],
        out_specs=(_tok(tb, d), pl.BlockSpec((d, tb), lambda i: (0, i)),
                   pl.BlockSpec((dq, tb), lambda i: (0, i))),
        out_shape=(jax.ShapeDtypeStruct((n, d), F32), jax.ShapeDtypeStruct((d, n), BF16),
                   jax.ShapeDtypeStruct((dq, n), BF16)),
        compiler_params=_cparams(("parallel",)),
        name="ml_post",
    )(hn, xc, z, h, g_norm, skip, w_down, g_ffn, w_qt)


NCAND_ROWS = 72


def _first_max(vals, ids, big):
    m = jnp.max(vals, axis=0, keepdims=True)
    sel = jnp.min(jnp.where(vals == m, ids, big), axis=0, keepdims=True)
    return m, sel


def _sorting_network(n):
    size = 1
    while size < n:
        size *= 2
    pairs = []
    p = 1
    while p < size:
        k = p
        while k >= 1:
            for j in range(k % p, size - k, 2 * k):
                for i in range(min(k, size - j - k)):
                    if (i + j) // (2 * p) == (i + j + k) // (2 * p):
                        pairs.append((i + j, i + j + k))
            k //= 2
        p *= 2
    return [(i, j) for i, j in pairs if j < n]


SUBLANES = 8


def _sorted_best_values(s, count):
    rows, lanes = s.shape
    nslab = rows // SUBLANES
    v = [s[SUBLANES * j:SUBLANES * (j + 1), :] for j in range(nslab)]
    for i, j in _sorting_network(nslab):
        v[i], v[j] = jnp.maximum(v[i], v[j]), jnp.minimum(v[i], v[j])
    tops = []
    multi = jnp.zeros((1, lanes), F32)
    for r in range(count):
        head = v[0]
        m = jnp.max(head, axis=0, keepdims=True)
        hit = head == m
        multi = jnp.maximum(multi, jnp.sum(hit.astype(F32), axis=0, keepdims=True))
        tops.append(m)
        if r == count - 1:
            break
        for j in range(min(nslab, count - 1 - r)):
            nxt = v[j + 1] if j + 1 < nslab else jnp.full_like(head, NEG_INF)
            v[j] = jnp.where(hit, nxt, v[j])
    tie = (multi > 1.5).astype(F32)
    for r in range(count - 1):
        tie = jnp.maximum(tie, (tops[r] == tops[r + 1]).astype(F32))
    return tops, tie


def _peer_route_kernel(qt_ref, keys_ref, r2_ref, e2_ref, n_ref, e1_ref, s_scr, rank_scr, top_scr, *, tk):
    nk, k16 = PEER_NKEYS, PEER_TOPK
    key_id = lax.broadcasted_iota(jnp.int32, (nk, tk), 0)
    slot_id = lax.broadcasted_iota(jnp.int32, (k16, tk), 0)

    def half_body(hc, carry):
        start = pl.multiple_of(hc * PEER_HALF, PEER_HALF)
        s = _dot(keys_ref[hc], qt_ref[pl.ds(start, PEER_HALF), :])
        s_scr[hc] = s

        tops, tie = _sorted_best_values(s, k16 + 1)

        def by_value():
            rank = jnp.full((nk, tk), float(k16), F32)
            for r in range(k16 - 1, -1, -1):
                rank = jnp.where(s >= tops[r], float(r), rank)
            return rank, jnp.concatenate(tops[:k16], axis=0)

        def by_value_and_index():
            def pick(r, st):
                s_cur, rank, top = st
                m, sel = _first_max(s_cur, key_id, nk)
                hit = key_id == sel
                return (jnp.where(hit, NEG_INF, s_cur), jnp.where(hit, r.astype(F32), rank),
                        jnp.where(slot_id == r, m, top))

            _, rank, top = lax.fori_loop(
                0, k16, pick, (s, jnp.full((nk, tk), float(k16), F32), jnp.zeros((k16, tk), F32)))
            return rank, top

        rank, top = lax.cond(jnp.max(tie) > 0.0, by_value_and_index, by_value)
        rank_scr[hc] = rank
        top_scr[hc] = top
        return carry

    lax.fori_loop(0, 2 * PEER_HEADS, half_body, 0)

    def cand_ids():
        i16 = lax.broadcasted_iota(jnp.int32, (16, tk), 0)
        i8 = lax.broadcasted_iota(jnp.int32, (8, tk), 0)
        parts = [i16, 16 + i8, 32 + i8, 48 + i8, 16 * i16, 16 * i8 + 1, 16 * i8 + 2]
        return jnp.concatenate(parts, axis=0)

    cand_id = cand_ids()
    row72 = lax.broadcasted_iota(jnp.int32, (NCAND_ROWS, tk), 0)
    cand_ok = (row72 < 40) | (cand_id >= 64)
    big_id = 16 * 16

    def head_body(h, carry):
        a = top_scr[2 * h]
        b = top_scr[2 * h + 1]
        parts = [a[0:1, :] + b, a[1:2, :] + b[0:8, :], a[2:3, :] + b[0:8, :], a[3:4, :] + b[0:8, :],
                 a + b[0:1, :], a[0:8, :] + b[1:2, :], a[0:8, :] + b[2:3, :]]
        cand = jnp.where(cand_ok, jnp.concatenate(parts, axis=0), NEG_INF)
        tops, tie = _sorted_best_values(cand, k16 + 1)

        def by_value():
            sel = cand >= tops[k16 - 1]
            self = sel.astype(F32)
            cnt = self[40:56, :] + jnp.concatenate(
                [self[56:64, :] + self[64:72, :], jnp.zeros((8, tk), F32)], axis=0)
            for r1, (lo, hi) in enumerate(((0, 16), (16, 24), (24, 32), (32, 40))):
                row = jnp.sum(self[lo:hi, :], axis=0, keepdims=True)
                cnt = cnt + jnp.where(slot_id == r1, row, 0.0)
            z = jnp.sum(jnp.where(sel, jnp.exp(cand - tops[0]), 0.0), axis=0, keepdims=True)
            return cnt, z

        def by_value_and_index():
            def pick(r, st):
                cand_cur, cnt, tsel = st
                m, sel = _first_max(cand_cur, cand_id, big_id)
                cand_cur = jnp.where(cand_id == sel, NEG_INF, cand_cur)
                cnt = cnt + (slot_id == (sel >> 4)).astype(F32)
                return cand_cur, cnt, jnp.where(slot_id == r, m, tsel)

            _, cnt, tsel = lax.fori_loop(
                0, k16, pick, (cand, jnp.zeros((k16, tk), F32), jnp.zeros((k16, tk), F32)))
            return cnt, jnp.sum(jnp.exp(tsel - tsel[0:1, :]), axis=0, keepdims=True)

        cnt, z = lax.cond(jnp.max(tie) > 0.0, by_value_and_index, by_value)
        rank1 = rank_scr[2 * h]
        rank2 = rank_scr[2 * h + 1]
        pk = 16
        rank1b = rank1.astype(BF16).reshape(nk // pk, pk, tk)
        n_sel = jnp.zeros((nk // pk, pk, tk), BF16)
        for r in range(k16):
            row = jnp.broadcast_to(cnt[r:r + 1, :], (pk, tk)).astype(BF16)
            n_sel = jnp.where(rank1b == r, row[None], n_sel)
        n_ref[h] = n_sel.reshape(nk, tk).astype(F32)
        e1_ref[h] = jnp.where(rank1 < k16, jnp.exp(s_scr[2 * h] - a[0:1, :]) * (GELU_OUT_SCALE / z), 0.0)
        r2_ref[h] = rank2.astype(BF16)
        e2_ref[h] = jnp.where(rank2 < k16, jnp.exp(s_scr[2 * h + 1] - b[0:1, :]), 0.0).astype(BF16)
        return carry

    lax.fori_loop(0, PEER_HEADS, head_body, 0)


def _peer_route(qt, keys):
    dq, n = qt.shape
    tk = _pick_block(n, (256, 128))
    nk = PEER_NKEYS
    shp = jax.ShapeDtypeStruct((PEER_HEADS, nk, n), F32)
    shp16 = jax.ShapeDtypeStruct((PEER_HEADS, nk, n), BF16)
    ospec = pl.BlockSpec((PEER_HEADS, nk, tk), lambda i: (0, 0, i))
    return pl.pallas_call(
        functools.partial(_peer_route_kernel, tk=tk),
        grid=(n // tk,),
        in_specs=[pl.BlockSpec((dq, tk), lambda i: (0, i)), _full(keys.shape)],
        out_specs=(ospec, ospec, ospec, ospec),
        out_shape=(shp16, shp16, shp, shp),
        scratch_shapes=[pltpu.VMEM((2 * PEER_HEADS, nk, tk), F32), pltpu.VMEM((2 * PEER_HEADS, nk, tk), F32),
                        pltpu.VMEM((2 * PEER_HEADS, PEER_TOPK, tk), F32)],
        compiler_params=_cparams(("parallel",)),
        name="peer_route",
    )(qt, keys)


def _peer_gate_stage(row0, act_ref, coef_ref, r2_ref, e2_ref, n_ref, e1_ref):
    nk = PEER_NKEYS
    t = act_ref.shape[1]
    pk = 16
    zero = jnp.zeros((), BF16)
    for al in range(PEER_EXPERT_HALF // nk):
        a = row0 + al
        rows = slice(al * nk, (al + 1) * nk)
        y = act_ref[rows, :]
        ge = (y * (1.0 + lax.erf(y))).astype(BF16).reshape(nk // pk, pk, t)
        g = None
        for h in range(PEER_HEADS):
            n16 = jnp.broadcast_to(n_ref[h, a:a + 1, :], (pk, t)).astype(BF16)
            e16 = jnp.broadcast_to(e1_ref[h, a:a + 1, :], (pk, t)).astype(BF16)
            r2h = r2_ref[h].reshape(nk // pk, pk, t)
            e2h = e2_ref[h].reshape(nk // pk, pk, t)
            term = jnp.where(r2h < n16[None], e2h, zero) * e16[None]
            g = term if g is None else g + term
        coef_ref[rows, :] = (g * ge).reshape(nk, t)


def _peer_expert_kernel(xt_ref, r2_ref, e2_ref, np_ref, e1p_ref, nc_ref, e1c_ref, u_ref, vt_ref, out_ref,
                        act0, act1, coef0, coef1):
    k = pl.program_id(1)
    eh = PEER_EXPERT_HALF

    @pl.when(k == 0)
    def _():
        act0[...] = jnp.zeros_like(act0)
        act1[...] = jnp.zeros_like(act1)
        coef0[...] = jnp.zeros_like(coef0)
        coef1[...] = jnp.zeros_like(coef1)
        out_ref[...] = jnp.zeros_like(out_ref)

    xt = xt_ref[...]
    half_rows = eh // PEER_NKEYS
    out_ref[...] += _dot(vt_ref[:, :eh], coef0[...])
    _peer_gate_stage(half_rows, act1, coef1, r2_ref, e2_ref, np_ref, e1p_ref)
    act0[...] = _dot(u_ref[:eh, :], xt)
    out_ref[...] += _dot(vt_ref[:, eh:], coef1[...])
    _peer_gate_stage(0, act0, coef0, r2_ref, e2_ref, nc_ref, e1c_ref)
    act1[...] = _dot(u_ref[eh:, :], xt)


def _peer_expert(xt, r2, e2, nsel, e1, u, vt):
    d, n = xt.shape
    ne = u.shape[0]
    t = _pick_block(n, (512, 256, 128))
    et = 2 * PEER_EXPERT_HALF
    nsteps = ne // et
    rspec = pl.BlockSpec((PEER_HEADS, PEER_NKEYS, t), lambda i, k: (0, 0, i))
    rows = et // PEER_NKEYS
    prev_rows = pl.BlockSpec((PEER_HEADS, rows, t), lambda i, k: (0, jnp.maximum(k - 1, 0), i))
    cur_rows = pl.BlockSpec((PEER_HEADS, rows, t), lambda i, k: (0, jnp.minimum(k, nsteps - 1), i))
    return pl.pallas_call(
        _peer_expert_kernel,
        grid=(n // t, nsteps + 1),
        in_specs=[pl.BlockSpec((d, t), lambda i, k: (0, i)), rspec, rspec,
                  prev_rows, prev_rows, cur_rows, cur_rows,
                  pl.BlockSpec((et, d), lambda i, k: (jnp.minimum(k, nsteps - 1), 0)),
                  pl.BlockSpec((d, et), lambda i, k: (0, jnp.maximum(k - 1, 0)))],
        out_specs=pl.BlockSpec((d, t), lambda i, k: (0, i)),
        out_shape=jax.ShapeDtypeStruct((d, n), F32),
        scratch_shapes=[pltpu.VMEM((PEER_EXPERT_HALF, t), F32), pltpu.VMEM((PEER_EXPERT_HALF, t), F32),
                        pltpu.VMEM((PEER_EXPERT_HALF, t), BF16), pltpu.VMEM((PEER_EXPERT_HALF, t), BF16)],
        compiler_params=_cparams(("parallel", "arbitrary")),
        name="peer_expert",
    )(xt, r2, e2, nsel, e1, nsel, e1, u, vt)


def _ple_kernel(ot_ref, h1_ref, p_ref, gple_ref, wgate_ref, wproj_ref, gnext_ref, h_ref, y_ref):
    h2 = h1_ref[...] + ot_ref[...].T
    gate = jax.nn.sigmoid(_dot(_rms(h2, gple_ref[...]).astype(BF16), wgate_ref[...]))
    h3 = h2 + gate * _dot(p_ref[...].astype(BF16), wproj_ref[...])
    h_ref[...] = h3
    y_ref[...] = _rms(h3, gnext_ref[...])


def _ple(out_t, h1, p, g_ple, w_gate, w_proj, g_next):
    n, d = h1.shape
    tb = _pick_block(n, (512, 256, 128))
    return pl.pallas_call(
        _ple_kernel,
        grid=(n // tb,),
        in_specs=[pl.BlockSpec((d, tb), lambda i: (0, i)), _tok(tb, d), _tok(tb, p.shape[1]),
                  _full(g_ple.shape), _full(w_gate.shape), _full(w_proj.shape), _full(g_next.shape)],
        out_specs=(_tok(tb, d), _tok(tb, d)),
        out_shape=(jax.ShapeDtypeStruct((n, d), F32), jax.ShapeDtypeStruct((n, d), F32)),
        compiler_params=_cparams(("parallel",)),
        name="ple",
    )(out_t, h1, p, g_ple, w_gate, w_proj, g_next)


def _block_diag_tiles(w):
    g, bi, bo = w.shape
    per = MXU_TILE // bi
    wt = w.reshape(g // per, per, bi, bo)
    eye = jnp.eye(per, dtype=w.dtype)
    dense = jnp.einsum("tgio,gh->tgiho", wt, eye)
    return dense.reshape(g // per, per * bi, per * bo)


def _row(x):
    return x.reshape(1, -1)


def _prep_weights(W, depth):
    P = {}
    n_gla = W["w_gla_in"].shape[0]
    dk = W["w_gla_gate"].shape[2]
    rank = W["w_gla_gate"].shape[1]
    dv = (W["w_gla_in"].shape[2] - rank - 2 * dk) // 2
    P["gla_dims"] = (dk, dv)
    pad = 128 - rank
    P["w_gla_qkvr"] = W["w_gla_in"][:, :, :2 * dk + 2 * dv].astype(BF16)
    P["w_gla_gl"] = jnp.pad(W["w_gla_in"][:, :, 2 * dk + 2 * dv:], ((0, 0), (0, 0), (0, pad))).astype(BF16)
    P["w_gla_gate"] = jnp.pad(W["w_gla_gate"], ((0, 0), (0, pad), (0, 0))).astype(BF16)
    P["w_gla_out"] = W["w_gla_out"].astype(BF16)
    P["w_ml_up"] = W["w_ml_up"].astype(BF16)
    n_ml = W["w_ml_up"].shape[0]
    for nm in ("w_ml_q", "w_ml_k", "w_ml_v"):
        P[nm] = jnp.stack([_block_diag_tiles(W[nm][j]) for j in range(n_ml)]).astype(BF16)
    wif = jnp.concatenate([W["w_ml_igate"], W["w_ml_fgate"]], axis=-1)
    P["w_ml_if"] = jnp.pad(wif, ((0, 0), (0, 0), (0, 128 - wif.shape[-1]))).astype(BF16)
    bif = jnp.concatenate([W["b_ml_igate"], W["b_ml_fgate"]], axis=-1)
    P["b_ml_if"] = jnp.pad(bif, ((0, 0), (0, 128 - bif.shape[-1])))
    P["w_ml_down"] = W["w_ml_down"].astype(BF16)
    P["w_peer_qt"] = jnp.swapaxes(W["w_peer_query"], 1, 2).astype(BF16)
    keys = W["peer_keys"]
    P["peer_keys"] = jnp.swapaxes(keys, 1, 2).reshape(depth, 2 * PEER_HEADS, PEER_NKEYS, PEER_HALF).astype(BF16)
    P["peer_u"] = (W["peer_u"] * GELU_ARG_SCALE).astype(BF16)
    P["peer_vt"] = jnp.swapaxes(W["peer_v"], 1, 2).astype(BF16)
    P["w_ple_gate"] = W["w_ple_gate"].astype(BF16)
    P["w_ple_proj"] = W["w_ple_proj"].astype(BF16)
    del n_gla
    return P


def _run_group(x, p, s_gla, c_ml, n_ml, m_ml, buf_ml, W, P, depth):
    bsz, seqlen, d = x.shape
    n = bsz * seqlen
    h = x.reshape(n, d)
    dk, dv = P["gla_dims"]
    new_s, new_c, new_n, new_m, new_buf = [], [], [], [], []
    y = None
    for i in range(depth):
        j = i // 2
        if i % 2 == 0:
            q, k, v, r, lg = _gla_pre(h, _row(W["norm_mix"][i]), P["w_gla_qkvr"][j], P["w_gla_gl"][j],
                                      P["w_gla_gate"][j], _row(W["b_gla_gate"][j]), dk, dv)
            s0 = None if s_gla is None else s_gla[j]
            o, s_fin = _gla_scan(q, k, v, lg, _row(W["g_gla_norm"][j]), s0, bsz, seqlen)
            new_s.append(s_fin)
            h1, xt, qt = _gla_post(o, r, h, P["w_gla_out"][j], _row(W["norm_ffn"][i]), P["w_peer_qt"][i])
        else:
            xm, z = _ml_pre(h, _row(W["norm_mix"][i]), P["w_ml_up"][j])
            di = xm.shape[1]
            if buf_ml is None:
                buf = jnp.zeros((bsz, HALO, di), F32)
            else:
                buf = jnp.pad(buf_ml[j], ((0, 0), (HALO - (ML_CONV - 1), 0), (0, 0)))
            qm, km, vm, xc, gates = _ml_qkv(xm, buf, W["ml_conv_w"][j], _row(W["ml_conv_b"][j]),
                                            P["w_ml_q"][j], P["w_ml_k"][j], P["w_ml_v"][j],
                                            P["w_ml_if"][j], _row(P["b_ml_if"][j]), bsz, seqlen)
            if c_ml is None:
                state = None
            else:
                state = (c_ml[j], n_ml[j][:, :, None, :],
                         jnp.broadcast_to(m_ml[j][:, :, None, None], m_ml[j].shape + (1, M_LANES)))
            hn, c_fin, n_fin, m_fin = _ml_scan(qm, km, vm, gates, state, bsz, seqlen)
            new_c.append(c_fin)
            new_n.append(n_fin[:, :, 0, :])
            new_m.append(m_fin[:, :, 0, 0])
            xm3 = xm.reshape(bsz, seqlen, di)
            if seqlen >= ML_CONV - 1:
                new_buf.append(xm3[:, seqlen - (ML_CONV - 1):, :])
            else:
                new_buf.append(jnp.concatenate([buf[:, HALO - (ML_CONV - 1):, :], xm3], axis=1)[:, -(ML_CONV - 1):, :])
            h1, xt, qt = _ml_post(hn, xc, z, h, _row(W["g_ml_norm"][j]), _row(W["ml_skip"][j]),
                                  P["w_ml_down"][j], _row(W["norm_ffn"][i]), P["w_peer_qt"][i])
        r2, e2, nsel, e1 = _peer_route(qt, P["peer_keys"][i])
        out_t = _peer_expert(xt, r2, e2, nsel, e1, P["peer_u"][i], P["peer_vt"][i])
        g_next = W["norm_final"] if i == depth - 1 else W["norm_mix"][i + 1]
        h, y = _ple(out_t, h1, p[i].reshape(n, -1), _row(W["norm_ple"][i]), P["w_ple_gate"][i],
                    P["w_ple_proj"][i], _row(g_next))
    return (y.reshape(bsz, seqlen, d), jnp.stack(new_s), jnp.stack(new_c), jnp.stack(new_n),
            jnp.stack(new_m), jnp.stack(new_buf))


def kernel(x_prompt, x_sample, state_gla_S, state_mlstm_C, state_mlstm_n, state_mlstm_m, state_mlstm_conv,
           p_prompt, p_sample, w_gla_in, w_gla_gate, b_gla_gate, g_gla_norm, w_gla_out,
           w_ml_up, ml_conv_w, ml_conv_b, w_ml_q, w_ml_k, w_ml_v, w_ml_igate, b_ml_igate,
           w_ml_fgate, b_ml_fgate, g_ml_norm, ml_skip, w_ml_down,
           w_peer_query, peer_keys, peer_u, peer_v, norm_mix, norm_ffn, norm_ple,
           w_ple_gate, w_ple_proj, norm_final):
    W = dict(w_gla_in=w_gla_in, w_gla_gate=w_gla_gate, b_gla_gate=b_gla_gate, g_gla_norm=g_gla_norm,
             w_gla_out=w_gla_out, w_ml_up=w_ml_up, ml_conv_w=ml_conv_w, ml_conv_b=ml_conv_b,
             w_ml_q=w_ml_q, w_ml_k=w_ml_k, w_ml_v=w_ml_v, w_ml_igate=w_ml_igate, b_ml_igate=b_ml_igate,
             w_ml_fgate=w_ml_fgate, b_ml_fgate=b_ml_fgate, g_ml_norm=g_ml_norm, ml_skip=ml_skip,
             w_ml_down=w_ml_down, w_peer_query=w_peer_query, peer_keys=peer_keys, peer_u=peer_u,
             peer_v=peer_v, norm_mix=norm_mix, norm_ffn=norm_ffn, norm_ple=norm_ple,
             w_ple_gate=w_ple_gate, w_ple_proj=w_ple_proj, norm_final=norm_final)
    depth = norm_mix.shape[0]
    P = _prep_weights(W, depth)
    y_p, s_p, c_p, n_p, m_p, buf_p = _run_group(x_prompt, p_prompt, None, None, None, None, None, W, P, depth)
    y_s, s_s, c_s, n_s, m_s, buf_s = _run_group(x_sample, p_sample, state_gla_S, state_mlstm_C,
                                                state_mlstm_n, state_mlstm_m, state_mlstm_conv, W, P, depth)
    return (y_p, y_s, s_p, s_s, c_p, c_s, n_p, n_s, m_p, m_s, buf_p, buf_s)
```

```python
import functools

import jax
import jax.numpy as jnp
from jax import lax
from jax.experimental import pallas as pl
from jax.experimental.pallas import tpu as pltpu

F32 = jnp.float32
BF16 = jnp.bfloat16
EPS = 1e-6
CHUNK = 64
HIGHEST = lax.Precision.HIGHEST
NEG_INF = float("-inf")

GLA_HEADS = 4
GLA_GATE_NORMALIZER = 16.0
ML_HEADS = 4
ML_CONV = 4
ML_QKV_BLOCK = 4
PEER_HEADS = 8
PEER_NKEYS = 128
PEER_HALF = 64
PEER_TOPK = 16

VMEM_LIMIT_BYTES = 52 * 1024 * 1024
MXU_TILE = 256
PEER_EXPERT_STEP = 1024
PEER_EXPERT_PIECE = MXU_TILE
GELU_ARG_SCALE = 0.7071067811865476
GELU_OUT_SCALE = 0.5 / GELU_ARG_SCALE


def _cparams(sem):
    return pltpu.CompilerParams(dimension_semantics=sem, vmem_limit_bytes=VMEM_LIMIT_BYTES)


def _pick_block(n, candidates):
    for c in candidates:
        if n % c == 0:
            return c
    raise ValueError(f"no block size in {candidates} divides {n}")


def _rms(x, g):
    ms = jnp.mean(x * x, axis=-1, keepdims=True)
    return x * lax.rsqrt(ms + EPS) * g


def _log_sigmoid(x):
    return jnp.minimum(x, 0.0) - jnp.log1p(jnp.exp(-jnp.abs(x)))


def _silu(x):
    return x * jax.nn.sigmoid(x)


def _dot(a, b):
    return jnp.dot(a, b, preferred_element_type=F32)


def _dot_nt(a, b):
    return lax.dot_general(a, b, (((1,), (1,)), ((), ())), preferred_element_type=F32)


def _dot_tn(a, b):
    return lax.dot_general(a, b, (((0,), (0,)), ((), ())), preferred_element_type=F32)


def _tok(tb, d):
    return pl.BlockSpec((tb, d), lambda i: (i, 0))


def _full(shape):
    nd = len(shape)
    return pl.BlockSpec(shape, lambda *_: (0,) * nd)


def _gla_pre_kernel(h_ref, g_ref, w_ref, wgl_ref, wgate_ref, bgate_ref,
                    q_ref, k_ref, v_ref, r_ref, lg_ref, *, dk, dv, hk):
    xn = _rms(h_ref[...], g_ref[...]).astype(BF16)
    proj = _dot(xn, w_ref[...])
    q_ref[...] = proj[:, :dk] * (hk ** -0.5)
    k_ref[...] = proj[:, dk:2 * dk]
    v_ref[...] = proj[:, 2 * dk:2 * dk + dv].astype(BF16)
    r_ref[...] = proj[:, 2 * dk + dv:]
    gl = _dot(xn, wgl_ref[...])
    gate = _dot(gl.astype(BF16), wgate_ref[...]) + bgate_ref[...]
    lg_ref[...] = _log_sigmoid(gate) * (1.0 / GLA_GATE_NORMALIZER)


def _gla_pre(h, g, w_qkvr, w_gl, w_gate, b_gate, dk, dv):
    n, d = h.shape
    tb = _pick_block(n, (512, 256, 128, 64))
    hk = dk // GLA_HEADS
    outs = (jax.ShapeDtypeStruct((n, dk), F32), jax.ShapeDtypeStruct((n, dk), F32),
            jax.ShapeDtypeStruct((n, dv), BF16), jax.ShapeDtypeStruct((n, dv), F32),
            jax.ShapeDtypeStruct((n, dk), F32))
    return pl.pallas_call(
        functools.partial(_gla_pre_kernel, dk=dk, dv=dv, hk=hk),
        grid=(n // tb,),
        in_specs=[_tok(tb, d), _full(g.shape), _full(w_qkvr.shape), _full(w_gl.shape),
                  _full(w_gate.shape), _full(b_gate.shape)],
        out_specs=(_tok(tb, dk), _tok(tb, dk), _tok(tb, dv), _tok(tb, dv), _tok(tb, dk)),
        out_shape=outs,
        compiler_params=_cparams(("parallel",)),
        name="gla_pre",
    )(h, g, w_qkvr, w_gl, w_gate, b_gate)


GLA_SUB = 16


def _gla_scan_kernel(*refs, nchunks, zero_init):
    if zero_init:
        q_ref, k_ref, v_ref, lg_ref, gn_ref, o_ref, sout_ref, st_ref = refs
        s0_ref = None
    else:
        q_ref, k_ref, v_ref, lg_ref, gn_ref, s0_ref, o_ref, sout_ref, st_ref = refs
    t = pl.program_id(1)
    nheads, hv, hk = st_ref.shape

    @pl.when(t == 0)
    def _():
        if zero_init:
            st_ref[...] = jnp.zeros_like(st_ref)
        else:
            for h in range(nheads):
                st_ref[h] = s0_ref[h].T

    row = lax.broadcasted_iota(jnp.int32, (CHUNK, CHUNK), 0)
    col = lax.broadcasted_iota(jnp.int32, (CHUNK, CHUNK), 1)
    tril = (row >= col).astype(F32)
    gn = gn_ref[...]
    nsub = CHUNK // GLA_SUB

    def bmm(a, b, lhs_c, rhs_c):
        return lax.dot_general(a, b, (((lhs_c,), (rhs_c,)), ((0,), (0,))), preferred_element_type=F32)

    heads = range(nheads)

    for c in range(nchunks):
        sl = pl.ds(c * CHUNK, CHUNK)
        b_all = jnp.dot(tril, lg_ref[sl, :], precision=HIGHEST, preferred_element_type=F32)
        b = jnp.stack([b_all[:, h * hk:(h + 1) * hk] for h in heads])
        q = jnp.stack([q_ref[sl, h * hk:(h + 1) * hk] for h in heads])
        k = jnp.stack([k_ref[sl, h * hk:(h + 1) * hk] for h in heads])
        v = jnp.stack([v_ref[sl, h * hv:(h + 1) * hv] for h in heads])
        st = st_ref[...]
        b_last = b[:, CHUNK - 1:CHUNK, :]
        inter = bmm((q * jnp.exp(b)).astype(BF16), st.astype(BF16), 2, 2)
        intra_parts = []
        for i in range(nsub):
            lo, hi = i * GLA_SUB, (i + 1) * GLA_SUB
            b_ref_row = b[:, lo:lo + 1, :]
            qe = (q[:, lo:hi, :] * jnp.exp(b[:, lo:hi, :] - b_ref_row)).astype(BF16)
            ke = (k[:, :hi, :] * jnp.exp(b_ref_row - b[:, :hi, :])).astype(BF16)
            att = bmm(qe, ke, 2, 2)
            r_i = lax.broadcasted_iota(jnp.int32, (GLA_SUB, hi), 0) + lo
            c_i = lax.broadcasted_iota(jnp.int32, (GLA_SUB, hi), 1)
            att = jnp.where(r_i >= c_i, att, 0.0)
            intra_parts.append(bmm(att.astype(BF16), v[:, :hi, :], 2, 1))
        o = inter + jnp.concatenate(intra_parts, axis=1)
        kd = (k * jnp.exp(b_last - b)).astype(BF16)
        st_ref[...] = st * jnp.exp(b_last) + bmm(v, kd, 1, 1)
        o = o * lax.rsqrt(jnp.mean(o * o, axis=-1, keepdims=True) + EPS) * gn
        for h in heads:
            o_ref[sl, h * hv:(h + 1) * hv] = o[h]

    @pl.when(t == pl.num_programs(1) - 1)
    def _():
        for h in range(nheads):
            sout_ref[h] = st_ref[h].T


def _gla_scan(q, k, v, lg, gn, s0, nseq, seqlen):
    n, dk = q.shape
    dv = v.shape[1]
    hk, hv = dk // GLA_HEADS, dv // GLA_HEADS
    tt = _pick_block(seqlen, (256, 128, 64))
    nt = seqlen // tt
    zero_init = s0 is None
    tmap = lambda s, t: (s * nt + t, 0)
    smap = lambda s, t: (s, 0, 0, 0)
    in_specs = [pl.BlockSpec((tt, dk), tmap), pl.BlockSpec((tt, dk), tmap),
                pl.BlockSpec((tt, dv), tmap), pl.BlockSpec((tt, dk), tmap),
                pl.BlockSpec((1, hv), lambda s, t: (0, 0))]
    args = [q, k, v, lg, gn]
    if not zero_init:
        in_specs.append(pl.BlockSpec((None, GLA_HEADS, hk, hv), smap))
        args.append(s0)
    return pl.pallas_call(
        functools.partial(_gla_scan_kernel, nchunks=tt // CHUNK, zero_init=zero_init),
        grid=(nseq, nt),
        in_specs=in_specs,
        out_specs=(pl.BlockSpec((tt, dv), tmap), pl.BlockSpec((None, GLA_HEADS, hk, hv), smap)),
        out_shape=(jax.ShapeDtypeStruct((n, dv), F32),
                   jax.ShapeDtypeStruct((nseq, GLA_HEADS, hk, hv), F32)),
        scratch_shapes=[pltpu.VMEM((GLA_HEADS, hv, hk), F32)],
        compiler_params=_cparams(("parallel", "arbitrary")),
        name="gla_scan",
    )(*args)


def _peer_pre_tail(h1, gffn_ref, wqt_ref, xt_ref, qt_ref):
    xn = _rms(h1, gffn_ref[...])
    xt = xn.T.astype(BF16)
    xt_ref[...] = xt
    qt_ref[...] = _dot(wqt_ref[...], xt).astype(BF16)


def _gla_post_kernel(o_ref, r_ref, h_ref, wout_ref, gffn_ref, wqt_ref, h1_ref, xt_ref, qt_ref):
    y = (o_ref[...] * _silu(r_ref[...])).astype(BF16)
    h1 = h_ref[...] + _dot(y, wout_ref[...])
    h1_ref[...] = h1
    _peer_pre_tail(h1, gffn_ref, wqt_ref, xt_ref, qt_ref)


def _gla_post(o, r, h, w_out, g_ffn, w_qt):
    n, d = h.shape
    tb = _pick_block(n, (512, 256, 128))
    dq = w_qt.shape[0]
    return pl.pallas_call(
        _gla_post_kernel,
        grid=(n // tb,),
        in_specs=[_tok(tb, o.shape[1]), _tok(tb, r.shape[1]), _tok(tb, d), _full(w_out.shape),
                  _full(g_ffn.shape), _full(w_qt.shape)],
        out_specs=(_tok(tb, d), pl.BlockSpec((d, tb), lambda i: (0, i)),
                   pl.BlockSpec((dq, tb), lambda i: (0, i))),
        out_shape=(jax.ShapeDtypeStruct((n, d), F32), jax.ShapeDtypeStruct((d, n), BF16),
                   jax.ShapeDtypeStruct((dq, n), BF16)),
        compiler_params=_cparams(("parallel",)),
        name="gla_post",
    )(o, r, h, w_out, g_ffn, w_qt)


def _ml_pre_kernel(h_ref, g_ref, w_ref, xm_ref, z_ref, *, di):
    xn = _rms(h_ref[...], g_ref[...]).astype(BF16)
    up = _dot(xn, w_ref[...])
    xm_ref[...] = up[:, :di]
    z_ref[...] = up[:, di:]


def _ml_pre(h, g, w_up):
    n, d = h.shape
    di = w_up.shape[1] // 2
    tb = _pick_block(n, (512, 256, 128, 64))
    return pl.pallas_call(
        functools.partial(_ml_pre_kernel, di=di),
        grid=(n // tb,),
        in_specs=[_tok(tb, d), _full(g.shape), _full(w_up.shape)],
        out_specs=(_tok(tb, di), _tok(tb, di)),
        out_shape=(jax.ShapeDtypeStruct((n, di), F32), jax.ShapeDtypeStruct((n, di), F32)),
        compiler_params=_cparams(("parallel",)),
        name="ml_pre",
    )(h, g, w_up)


HALO = 8


def _ml_qkv_kernel(xm_ref, prev_ref, buf_ref, cw_ref, cb_ref, wq_ref, wk_ref, wv_ref, wif_ref, bif_ref,
                   q_ref, k_ref, v_ref, xc_ref, gates_ref, xp_scr, *, tt, di, hd):
    t = pl.program_id(1)
    x = xm_ref[...]
    halo = jnp.where(t == 0, buf_ref[...], prev_ref[...])
    xp_scr[0:HALO, :] = halo
    xp_scr[HALO:HALO + tt, :] = x
    y = cb_ref[...]
    for j in range(ML_CONV):
        y = y + cw_ref[j:j + 1, :] * xp_scr[pl.ds(HALO - (ML_CONV - 1) + j, tt), :]
    xc = _silu(y)
    xc_ref[...] = xc
    xcb = xc.astype(BF16)
    xmb = x.astype(BF16)
    gates = jnp.zeros((tt, gates_ref.shape[1]), F32) + bif_ref[...]
    ntile = di // MXU_TILE
    for j in range(ntile):
        cs = slice(j * MXU_TILE, (j + 1) * MXU_TILE)
        qj = _dot(xcb[:, cs], wq_ref[j])
        kj = _dot(xcb[:, cs], wk_ref[j])
        vj = _dot(xmb[:, cs], wv_ref[j])
        q_ref[:, cs] = qj.astype(BF16)
        k_ref[:, cs] = kj * (hd ** -0.5)
        v_ref[:, cs] = vj.astype(BF16)
        gates = gates + _dot(qj.astype(BF16), wif_ref[j * MXU_TILE:(j + 1) * MXU_TILE, :])
        gates = gates + _dot(kj.astype(BF16), wif_ref[di + j * MXU_TILE:di + (j + 1) * MXU_TILE, :])
        gates = gates + _dot(vj.astype(BF16), wif_ref[2 * di + j * MXU_TILE:2 * di + (j + 1) * MXU_TILE, :])
    lane = lax.broadcasted_iota(jnp.int32, gates.shape, 1)
    is_f = (lane >= ML_HEADS) & (lane < 2 * ML_HEADS)
    gates_ref[...] = jnp.where(is_f, _log_sigmoid(gates), gates)


def _ml_qkv(xm, buf, conv_w, conv_b, wq, wk, wv, wif, bif, nseq, seqlen):
    n, di = xm.shape
    hd = di // ML_HEADS
    tt = _pick_block(seqlen, (256, 128, 64))
    nt = seqlen // tt
    tmap = lambda s, t: (s * nt + t, 0)
    prev_map = lambda s, t: (jnp.maximum((s * nt + t) * (tt // HALO) - 1, 0), 0)
    ng = wif.shape[1]
    return pl.pallas_call(
        functools.partial(_ml_qkv_kernel, tt=tt, di=di, hd=hd),
        grid=(nseq, nt),
        in_specs=[pl.BlockSpec((tt, di), tmap), pl.BlockSpec((HALO, di), prev_map),
                  pl.BlockSpec((None, HALO, di), lambda s, t: (s, 0, 0)),
                  _full(conv_w.shape), _full(conv_b.shape), _full(wq.shape), _full(wk.shape),
                  _full(wv.shape), _full(wif.shape), _full(bif.shape)],
        out_specs=(pl.BlockSpec((tt, di), tmap), pl.BlockSpec((tt, di), tmap),
                   pl.BlockSpec((tt, di), tmap), pl.BlockSpec((tt, di), tmap),
                   pl.BlockSpec((tt, ng), tmap)),
        out_shape=(jax.ShapeDtypeStruct((n, di), BF16), jax.ShapeDtypeStruct((n, di), F32),
                   jax.ShapeDtypeStruct((n, di), BF16), jax.ShapeDtypeStruct((n, di), F32),
                   jax.ShapeDtypeStruct((n, ng), F32)),
        scratch_shapes=[pltpu.VMEM((HALO + tt, di), F32)],
        compiler_params=_cparams(("parallel", "arbitrary")),
        name="ml_qkv",
    )(xm, xm, buf, conv_w, conv_b, wq, wk, wv, wif, bif)


def _ml_scan_kernel(*refs, nchunks, zero_init):
    if zero_init:
        q_ref, k_ref, v_ref, g_ref, h_ref, cout_ref, nout_ref, mout_ref, c_scr, n_scr, m_scr = refs
    else:
        (q_ref, k_ref, v_ref, g_ref, c0_ref, n0_ref, m0_ref,
         h_ref, cout_ref, nout_ref, mout_ref, c_scr, n_scr, m_scr) = refs
    t = pl.program_id(1)
    nheads, hd = c_scr.shape[0], c_scr.shape[1]

    @pl.when(t == 0)
    def _():
        if zero_init:
            c_scr[...] = jnp.zeros_like(c_scr)
            n_scr[...] = jnp.zeros_like(n_scr)
            m_scr[...] = jnp.zeros_like(m_scr)
        else:
            c_scr[...] = c0_ref[...]
            n_scr[...] = n0_ref[...]
            m_scr[...] = m0_ref[...]

    row = lax.broadcasted_iota(jnp.int32, (CHUNK, CHUNK), 0)
    col = lax.broadcasted_iota(jnp.int32, (CHUNK, CHUNK), 1)
    causal = row >= col
    tril = causal.astype(F32)
    eye = row == col
    ones = jnp.ones((CHUNK, CHUNK), F32)

    def bmm(a, b, lhs_c, rhs_c, precision=None):
        return lax.dot_general(a, b, (((lhs_c,), (rhs_c,)), ((0,), (0,))), precision=precision,
                               preferred_element_type=F32)

    heads = range(nheads)
    ones_h = jnp.ones((nheads, CHUNK, CHUNK), F32)

    for c in range(nchunks):
        sl = pl.ds(c * CHUNK, CHUNK)
        q = jnp.stack([q_ref[sl, h * hd:(h + 1) * hd] for h in heads])
        kf = jnp.stack([k_ref[sl, h * hd:(h + 1) * hd] for h in heads])
        v = jnp.stack([v_ref[sl, h * hd:(h + 1) * hd] for h in heads])
        g = g_ref[sl, :]
        cmat = c_scr[...]
        nvec = n_scr[...]
        m = m_scr[...][:, :, 0:1]
        gcum = jnp.dot(tril, g, precision=HIGHEST, preferred_element_type=F32)
        i_col = jnp.stack([g[:, h:h + 1] for h in heads])
        f_col = jnp.stack([gcum[:, h + nheads:h + nheads + 1] for h in heads])
        w_row = bmm(ones_h, jnp.where(eye, i_col - f_col, 0.0), 2, 1, precision=HIGHEST)
        log_d = jnp.where(causal, f_col + w_row, NEG_INF)
        inter = f_col + m
        mt = jnp.maximum(inter, jnp.max(log_d, axis=-1, keepdims=True))
        sc = bmm(q, kf.astype(BF16), 2, 2) * jnp.exp(log_d - mt)
        a = jnp.exp(inter - mt)
        num = a * bmm(q, cmat.astype(BF16), 2, 1) + bmm(sc.astype(BF16), v, 2, 1)
        qn = jnp.sum(q.astype(F32) * nvec, axis=-1, keepdims=True)
        den = a * qn + jnp.sum(sc, axis=-1, keepdims=True)
        hh = num / jnp.maximum(jnp.abs(den), jnp.exp(-mt))
        m_new = mt[:, CHUNK - 1:CHUNK, :]
        f_last = f_col[:, CHUNK - 1:CHUNK, :]
        w_end = jnp.exp(f_last - f_col + i_col - m_new)
        a_end = jnp.exp(f_last + m - m_new)
        kw = kf * w_end
        c_scr[...] = a_end * cmat + bmm(kw.astype(BF16), v, 1, 1)
        n_scr[...] = a_end * nvec + jnp.sum(kw, axis=1, keepdims=True)
        m_scr[...] = jnp.broadcast_to(m_new, m_scr.shape)
        mu = jnp.mean(hh, axis=-1, keepdims=True)
        var = jnp.mean(jnp.square(hh - mu), axis=-1, keepdims=True)
        hn = (hh - mu) * lax.rsqrt(var + EPS)
        for h in heads:
            h_ref[sl, h * hd:(h + 1) * hd] = hn[h]

    @pl.when(t == pl.num_programs(1) - 1)
    def _():
        cout_ref[...] = c_scr[...]
        nout_ref[...] = n_scr[...]
        mout_ref[...] = m_scr[...]


M_LANES = 128


def _ml_scan(q, k, v, gates, state, nseq, seqlen):
    n, di = q.shape
    hd = di // ML_HEADS
    tt = _pick_block(seqlen, (256, 128, 64))
    nt = seqlen // tt
    zero_init = state is None
    tmap = lambda s, t: (s * nt + t, 0)
    smap = lambda s, t: (s, 0, 0, 0)
    ng = gates.shape[1]
    nh = ML_HEADS
    in_specs = [pl.BlockSpec((tt, di), tmap), pl.BlockSpec((tt, di), tmap),
                pl.BlockSpec((tt, di), tmap), pl.BlockSpec((tt, ng), tmap)]
    args = [q, k, v, gates]
    state_specs = [pl.BlockSpec((None, nh, hd, hd), smap), pl.BlockSpec((None, nh, 1, hd), smap),
                   pl.BlockSpec((None, nh, 1, M_LANES), smap)]
    if not zero_init:
        in_specs += state_specs
        args += list(state)
    return pl.pallas_call(
        functools.partial(_ml_scan_kernel, nchunks=tt // CHUNK, zero_init=zero_init),
        grid=(nseq, nt),
        in_specs=in_specs,
        out_specs=(pl.BlockSpec((tt, di), tmap), *state_specs),
        out_shape=(jax.ShapeDtypeStruct((n, di), F32),
                   jax.ShapeDtypeStruct((nseq, nh, hd, hd), F32),
                   jax.ShapeDtypeStruct((nseq, nh, 1, hd), F32),
                   jax.ShapeDtypeStruct((nseq, nh, 1, M_LANES), F32)),
        scratch_shapes=[pltpu.VMEM((nh, hd, hd), F32), pltpu.VMEM((nh, 1, hd), F32),
                        pltpu.VMEM((nh, 1, M_LANES), F32)],
        compiler_params=_cparams(("parallel", "arbitrary")),
        name="ml_scan",
    )(*args)


def _ml_post_kernel(hn_ref, xc_ref, z_ref, h_ref, gn_ref, skip_ref, wdown_ref, gffn_ref, wqt_ref,
                    h1_ref, xt_ref, qt_ref):
    y = (hn_ref[...] * gn_ref[...] + skip_ref[...] * xc_ref[...]) * _silu(z_ref[...])
    h1 = h_ref[...] + _dot(y.astype(BF16), wdown_ref[...])
    h1_ref[...] = h1
    _peer_pre_tail(h1, gffn_ref, wqt_ref, xt_ref, qt_ref)


def _ml_post(hn, xc, z, h, g_norm, skip, w_down, g_ffn, w_qt):
    n, d = h.shape
    di = hn.shape[1]
    tb = _pick_block(n, (512, 256, 128))
    dq = w_qt.shape[0]
    return pl.pallas_call(
        _ml_post_kernel,
        grid=(n // tb,),
        in_specs=[_tok(tb, di), _tok(tb, di), _tok(tb, di), _tok(tb, d), _full(g_norm.shape),
                  _full(skip.shape), _full(w_down.shape), _full(g_ffn.shape), _full(w_qt.shape)],
        out_specs=(_tok(tb, d), pl.BlockSpec((d, tb), lambda i: (0, i)),
                   pl.BlockSpec((dq, tb), lambda i: (0, i))),
        out_shape=(jax.ShapeDtypeStruct((n, d), F32), jax.ShapeDtypeStruct((d, n), BF16),
                   jax.ShapeDtypeStruct((dq, n), BF16)),
        compiler_params=_cparams(("parallel",)),
        name="ml_post",
    )(hn, xc, z, h, g_norm, skip, w_down, g_ffn, w_qt)


NCAND_ROWS = 72


def _first_max(vals, ids, big):
    m = jnp.max(vals, axis=0, keepdims=True)
    sel = jnp.min(jnp.where(vals == m, ids, big), axis=0, keepdims=True)
    return m, sel


def _sorting_network(n):
    size = 1
    while size < n:
        size *= 2
    pairs = []
    p = 1
    while p < size:
        k = p
        while k >= 1:
            for j in range(k % p, size - k, 2 * k):
                for i in range(min(k, size - j - k)):
                    if (i + j) // (2 * p) == (i + j + k) // (2 * p):
                        pairs.append((i + j, i + j + k))
            k //= 2
        p *= 2
    return [(i, j) for i, j in pairs if j < n]


SUBLANES = 8


def _sorted_best_values(s, count):
    rows, lanes = s.shape
    nslab = rows // SUBLANES
    v = [s[SUBLANES * j:SUBLANES * (j + 1), :] for j in range(nslab)]
    for i, j in _sorting_network(nslab):
        v[i], v[j] = jnp.maximum(v[i], v[j]), jnp.minimum(v[i], v[j])
    tops = []
    multi = jnp.zeros((1, lanes), F32)
    for r in range(count):
        head = v[0]
        m = jnp.max(head, axis=0, keepdims=True)
        hit = head == m
        multi = jnp.maximum(multi, jnp.sum(hit.astype(F32), axis=0, keepdims=True))
        tops.append(m)
        if r == count - 1:
            break
        for j in range(min(nslab, count - 1 - r)):
            nxt = v[j + 1] if j + 1 < nslab else jnp.full_like(head, NEG_INF)
            v[j] = jnp.where(hit, nxt, v[j])
    tie = (multi > 1.5).astype(F32)
    for r in range(count - 1):
        tie = jnp.maximum(tie, (tops[r] == tops[r + 1]).astype(F32))
    return tops, tie


def _peer_route_kernel(qt_ref, keys_ref, r2_ref, e2_ref, n_ref, e1_ref, s_scr, rank_scr, top_scr, *, tk):
    nk, k16 = PEER_NKEYS, PEER_TOPK
    key_id = lax.broadcasted_iota(jnp.int32, (nk, tk), 0)
    slot_id = lax.broadcasted_iota(jnp.int32, (k16, tk), 0)

    def half_body(hc, carry):
        start = pl.multiple_of(hc * PEER_HALF, PEER_HALF)
        s = _dot(keys_ref[hc], qt_ref[pl.ds(start, PEER_HALF), :])
        s_scr[hc] = s

        tops, tie = _sorted_best_values(s, k16 + 1)

        def by_value():
            rank = jnp.full((nk, tk), float(k16), F32)
            for r in range(k16 - 1, -1, -1):
                rank = jnp.where(s >= tops[r], float(r), rank)
            return rank, jnp.concatenate(tops[:k16], axis=0)

        def by_value_and_index():
            def pick(r, st):
                s_cur, rank, top = st
                m, sel = _first_max(s_cur, key_id, nk)
                hit = key_id == sel
                return (jnp.where(hit, NEG_INF, s_cur), jnp.where(hit, jnp.asarray(r, F32), rank),
                        jnp.where(slot_id == r, m, top))

            _, rank, top = lax.fori_loop(
                0, k16, pick, (s, jnp.full((nk, tk), float(k16), F32), jnp.zeros((k16, tk), F32)))
            return rank, top

        rank, top = lax.cond(jnp.max(tie) > 0.0, by_value_and_index, by_value)
        rank_scr[hc] = rank
        top_scr[hc] = top
        return carry

    lax.fori_loop(0, 2 * PEER_HEADS, half_body, 0)

    def cand_ids():
        i16 = lax.broadcasted_iota(jnp.int32, (16, tk), 0)
        i8 = lax.broadcasted_iota(jnp.int32, (8, tk), 0)
        parts = [i16, 16 + i8, 32 + i8, 48 + i8, 16 * i16, 16 * i8 + 1, 16 * i8 + 2]
        return jnp.concatenate(parts, axis=0)

    cand_id = cand_ids()
    row72 = lax.broadcasted_iota(jnp.int32, (NCAND_ROWS, tk), 0)
    cand_ok = (row72 < 40) | (cand_id >= 64)
    big_id = 16 * 16

    def head_body(h, carry):
        a = top_scr[2 * h]
        b = top_scr[2 * h + 1]
        parts = [a[0:1, :] + b, a[1:2, :] + b[0:8, :], a[2:3, :] + b[0:8, :], a[3:4, :] + b[0:8, :],
                 a + b[0:1, :], a[0:8, :] + b[1:2, :], a[0:8, :] + b[2:3, :]]
        cand = jnp.where(cand_ok, jnp.concatenate(parts, axis=0), NEG_INF)
        tops, tie = _sorted_best_values(cand, k16 + 1)

        def by_value():
            sel = cand >= tops[k16 - 1]
            self = sel.astype(F32)
            cnt = self[40:56, :] + jnp.concatenate(
                [self[56:64, :] + self[64:72, :], jnp.zeros((8, tk), F32)], axis=0)
            for r1, (lo, hi) in enumerate(((0, 16), (16, 24), (24, 32), (32, 40))):
                row = jnp.sum(self[lo:hi, :], axis=0, keepdims=True)
                cnt = cnt + jnp.where(slot_id == r1, row, 0.0)
            z = jnp.sum(jnp.where(sel, jnp.exp(cand - tops[0]), 0.0), axis=0, keepdims=True)
            return cnt, z

        def by_value_and_index():
            def pick(r, st):
                cand_cur, cnt, tsel = st
                m, sel = _first_max(cand_cur, cand_id, big_id)
                cand_cur = jnp.where(cand_id == sel, NEG_INF, cand_cur)
                cnt = cnt + (slot_id == (sel >> 4)).astype(F32)
                return cand_cur, cnt, jnp.where(slot_id == r, m, tsel)

            _, cnt, tsel = lax.fori_loop(
                0, k16, pick, (cand, jnp.zeros((k16, tk), F32), jnp.zeros((k16, tk), F32)))
            return cnt, jnp.sum(jnp.exp(tsel - tsel[0:1, :]), axis=0, keepdims=True)

        cnt, z = lax.cond(jnp.max(tie) > 0.0, by_value_and_index, by_value)
        rank1 = rank_scr[2 * h]
        rank2 = rank_scr[2 * h + 1]
        pk = 16
        rank1b = rank1.astype(BF16).reshape(nk // pk, pk, tk)
        n_sel = jnp.zeros((nk // pk, pk, tk), BF16)
        for r in range(k16):
            row = jnp.broadcast_to(cnt[r:r + 1, :], (pk, tk)).astype(BF16)
            n_sel = jnp.where(rank1b == r, row[None], n_sel)
        n_ref[h] = n_sel.reshape(nk, tk).astype(F32)
        e1_ref[h] = jnp.where(rank1 < k16, jnp.exp(s_scr[2 * h] - a[0:1, :]) * (GELU_OUT_SCALE / z), 0.0)
        r2_ref[h] = rank2.astype(BF16)
        e2_ref[h] = jnp.where(rank2 < k16, jnp.exp(s_scr[2 * h + 1] - b[0:1, :]), 0.0).astype(BF16)
        return carry

    lax.fori_loop(0, PEER_HEADS, head_body, 0)


def _peer_route(qt, keys):
    dq, n = qt.shape
    tk = _pick_block(n, (256, 128))
    nk = PEER_NKEYS
    shp = jax.ShapeDtypeStruct((PEER_HEADS, nk, n), F32)
    shp16 = jax.ShapeDtypeStruct((PEER_HEADS, nk, n), BF16)
    ospec = pl.BlockSpec((PEER_HEADS, nk, tk), lambda i: (0, 0, i))
    return pl.pallas_call(
        functools.partial(_peer_route_kernel, tk=tk),
        grid=(n // tk,),
        in_specs=[pl.BlockSpec((dq, tk), lambda i: (0, i)), _full(keys.shape)],
        out_specs=(ospec, ospec, ospec, ospec),
        out_shape=(shp16, shp16, shp, shp),
        scratch_shapes=[pltpu.VMEM((2 * PEER_HEADS, nk, tk), F32), pltpu.VMEM((2 * PEER_HEADS, nk, tk), F32),
                        pltpu.VMEM((2 * PEER_HEADS, PEER_TOPK, tk), F32)],
        compiler_params=_cparams(("parallel",)),
        name="peer_route",
    )(qt, keys)


def _peer_gate_stage(row0, act_ref, coef_ref, r2_ref, e2_ref, n_ref, e1_ref):
    nk = PEER_NKEYS
    t = act_ref.shape[1]
    pk = 16
    zero = jnp.zeros((), BF16)
    for al in range(PEER_EXPERT_PIECE // nk):
        a = row0 + al
        rows = slice(al * nk, (al + 1) * nk)
        y = act_ref[rows, :]
        ge = (y * (1.0 + lax.erf(y))).astype(BF16).reshape(nk // pk, pk, t)
        g = None
        for h in range(PEER_HEADS):
            n16 = jnp.broadcast_to(n_ref[h, a:a + 1, :], (pk, t)).astype(BF16)
            e16 = jnp.broadcast_to(e1_ref[h, a:a + 1, :], (pk, t)).astype(BF16)
            r2h = r2_ref[h].reshape(nk // pk, pk, t)
            e2h = e2_ref[h].reshape(nk // pk, pk, t)
            term = jnp.where(r2h < n16[None], e2h, zero) * e16[None]
            g = term if g is None else g + term
        coef_ref[rows, :] = (g * ge).reshape(nk, t)


def _peer_expert_kernel(xt_ref, r2_ref, e2_ref, n_ref, e1_ref, u_ref, vt_ref, out_ref, act_scr, coef_scr):
    k = pl.program_id(1)
    ep = PEER_EXPERT_PIECE
    npiece = act_scr.shape[0]
    rows = ep // PEER_NKEYS

    @pl.when(k == 0)
    def _():
        out_ref[...] = jnp.zeros_like(out_ref)

    xt = xt_ref[...]
    for p in range(npiece):
        act_scr[p] = _dot(u_ref[p * ep:(p + 1) * ep, :], xt)
    for p in range(npiece):
        _peer_gate_stage(p * rows, act_scr.at[p], coef_scr.at[p], r2_ref, e2_ref, n_ref, e1_ref)
    for p in range(npiece):
        out_ref[...] += _dot(vt_ref[:, p * ep:(p + 1) * ep], coef_scr[p])


def _peer_expert(xt, r2, e2, nsel, e1, u, vt):
    d, n = xt.shape
    ne = u.shape[0]
    t = _pick_block(n, (512, 256, 128))
    et = PEER_EXPERT_STEP
    npiece = et // PEER_EXPERT_PIECE
    rspec = pl.BlockSpec((PEER_HEADS, PEER_NKEYS, t), lambda i, k: (0, 0, i))
    row_spec = pl.BlockSpec((PEER_HEADS, et // PEER_NKEYS, t), lambda i, k: (0, k, i))
    return pl.pallas_call(
        _peer_expert_kernel,
        grid=(n // t, ne // et),
        in_specs=[pl.BlockSpec((d, t), lambda i, k: (0, i)), rspec, rspec, row_spec, row_spec,
                  pl.BlockSpec((et, d), lambda i, k: (k, 0)),
                  pl.BlockSpec((d, et), lambda i, k: (0, k))],
        out_specs=pl.BlockSpec((d, t), lambda i, k: (0, i)),
        out_shape=jax.ShapeDtypeStruct((d, n), F32),
        scratch_shapes=[pltpu.VMEM((npiece, PEER_EXPERT_PIECE, t), F32),
                        pltpu.VMEM((npiece, PEER_EXPERT_PIECE, t), BF16)],
        compiler_params=_cparams(("parallel", "arbitrary")),
        name="peer_expert",
    )(xt, r2, e2, nsel, e1, u, vt)


def _ple_kernel(ot_ref, h1_ref, p_ref, gple_ref, wgate_ref, wproj_ref, gnext_ref, h_ref, y_ref):
    h2 = h1_ref[...] + ot_ref[...].T
    gate = jax.nn.sigmoid(_dot(_rms(h2, gple_ref[...]).astype(BF16), wgate_ref[...]))
    h3 = h2 + gate * _dot(p_ref[...].astype(BF16), wproj_ref[...])
    h_ref[...] = h3
    y_ref[...] = _rms(h3, gnext_ref[...])


def _ple(out_t, h1, p, g_ple, w_gate, w_proj, g_next):
    n, d = h1.shape
    tb = _pick_block(n, (512, 256, 128))
    return pl.pallas_call(
        _ple_kernel,
        grid=(n // tb,),
        in_specs=[pl.BlockSpec((d, tb), lambda i: (0, i)), _tok(tb, d), _tok(tb, p.shape[1]),
                  _full(g_ple.shape), _full(w_gate.shape), _full(w_proj.shape), _full(g_next.shape)],
        out_specs=(_tok(tb, d), _tok(tb, d)),
        out_shape=(jax.ShapeDtypeStruct((n, d), F32), jax.ShapeDtypeStruct((n, d), F32)),
        compiler_params=_cparams(("parallel",)),
        name="ple",
    )(out_t, h1, p, g_ple, w_gate, w_proj, g_next)


def _block_diag_tiles(w):
    g, bi, bo = w.shape
    per = MXU_TILE // bi
    wt = w.reshape(g // per, per, bi, bo)
    eye = jnp.eye(per, dtype=w.dtype)
    dense = jnp.einsum("tgio,gh->tgiho", wt, eye)
    return dense.reshape(g // per, per * bi, per * bo)


def _row(x):
    return x.reshape(1, -1)


def _prep_weights(W, depth):
    P = {}
    n_gla = W["w_gla_in"].shape[0]
    dk = W["w_gla_gate"].shape[2]
    rank = W["w_gla_gate"].shape[1]
    dv = (W["w_gla_in"].shape[2] - rank - 2 * dk) // 2
    P["gla_dims"] = (dk, dv)
    pad = 128 - rank
    P["w_gla_qkvr"] = W["w_gla_in"][:, :, :2 * dk + 2 * dv].astype(BF16)
    P["w_gla_gl"] = jnp.pad(W["w_gla_in"][:, :, 2 * dk + 2 * dv:], ((0, 0), (0, 0), (0, pad))).astype(BF16)
    P["w_gla_gate"] = jnp.pad(W["w_gla_gate"], ((0, 0), (0, pad), (0, 0))).astype(BF16)
    P["w_gla_out"] = W["w_gla_out"].astype(BF16)
    P["w_ml_up"] = W["w_ml_up"].astype(BF16)
    n_ml = W["w_ml_up"].shape[0]
    for nm in ("w_ml_q", "w_ml_k", "w_ml_v"):
        P[nm] = jnp.stack([_block_diag_tiles(W[nm][j]) for j in range(n_ml)]).astype(BF16)
    wif = jnp.concatenate([W["w_ml_igate"], W["w_ml_fgate"]], axis=-1)
    P["w_ml_if"] = jnp.pad(wif, ((0, 0), (0, 0), (0, 128 - wif.shape[-1]))).astype(BF16)
    bif = jnp.concatenate([W["b_ml_igate"], W["b_ml_fgate"]], axis=-1)
    P["b_ml_if"] = jnp.pad(bif, ((0, 0), (0, 128 - bif.shape[-1])))
    P["w_ml_down"] = W["w_ml_down"].astype(BF16)
    P["w_peer_qt"] = jnp.swapaxes(W["w_peer_query"], 1, 2).astype(BF16)
    keys = W["peer_keys"]
    P["peer_keys"] = jnp.swapaxes(keys, 1, 2).reshape(depth, 2 * PEER_HEADS, PEER_NKEYS, PEER_HALF).astype(BF16)
    P["peer_u"] = (W["peer_u"] * GELU_ARG_SCALE).astype(BF16)
    P["peer_vt"] = jnp.swapaxes(W["peer_v"], 1, 2).astype(BF16)
    P["w_ple_gate"] = W["w_ple_gate"].astype(BF16)
    P["w_ple_proj"] = W["w_ple_proj"].astype(BF16)
    del n_gla
    return P


def _run_group(x, p, s_gla, c_ml, n_ml, m_ml, buf_ml, W, P, depth):
    bsz, seqlen, d = x.shape
    n = bsz * seqlen
    h = x.reshape(n, d)
    dk, dv = P["gla_dims"]
    new_s, new_c, new_n, new_m, new_buf = [], [], [], [], []
    y = None
    for i in range(depth):
        j = i // 2
        if i % 2 == 0:
            q, k, v, r, lg = _gla_pre(h, _row(W["norm_mix"][i]), P["w_gla_qkvr"][j], P["w_gla_gl"][j],
                                      P["w_gla_gate"][j], _row(W["b_gla_gate"][j]), dk, dv)
            s0 = None if s_gla is None else s_gla[j]
            o, s_fin = _gla_scan(q, k, v, lg, _row(W["g_gla_norm"][j]), s0, bsz, seqlen)
            new_s.append(s_fin)
            h1, xt, qt = _gla_post(o, r, h, P["w_gla_out"][j], _row(W["norm_ffn"][i]), P["w_peer_qt"][i])
        else:
            xm, z = _ml_pre(h, _row(W["norm_mix"][i]), P["w_ml_up"][j])
            di = xm.shape[1]
            if buf_ml is None:
                buf = jnp.zeros((bsz, HALO, di), F32)
            else:
                buf = jnp.pad(buf_ml[j], ((0, 0), (HALO - (ML_CONV - 1), 0), (0, 0)))
            qm, km, vm, xc, gates = _ml_qkv(xm, buf, W["ml_conv_w"][j], _row(W["ml_conv_b"][j]),
                                            P["w_ml_q"][j], P["w_ml_k"][j], P["w_ml_v"][j],
                                            P["w_ml_if"][j], _row(P["b_ml_if"][j]), bsz, seqlen)
            if c_ml is None:
                state = None
            else:
                state = (c_ml[j], n_ml[j][:, :, None, :],
                         jnp.broadcast_to(m_ml[j][:, :, None, None], m_ml[j].shape + (1, M_LANES)))
            hn, c_fin, n_fin, m_fin = _ml_scan(qm, km, vm, gates, state, bsz, seqlen)
            new_c.append(c_fin)
            new_n.append(n_fin[:, :, 0, :])
            new_m.append(m_fin[:, :, 0, 0])
            xm3 = xm.reshape(bsz, seqlen, di)
            if seqlen >= ML_CONV - 1:
                new_buf.append(xm3[:, seqlen - (ML_CONV - 1):, :])
            else:
                new_buf.append(jnp.concatenate([buf[:, HALO - (ML_CONV - 1):, :], xm3], axis=1)[:, -(ML_CONV - 1):, :])
            h1, xt, qt = _ml_post(hn, xc, z, h, _row(W["g_ml_norm"][j]), _row(W["ml_skip"][j]),
                                  P["w_ml_down"][j], _row(W["norm_ffn"][i]), P["w_peer_qt"][i])
        r2, e2, nsel, e1 = _peer_route(qt, P["peer_keys"][i])
        out_t = _peer_expert(xt, r2, e2, nsel, e1, P["peer_u"][i], P["peer_vt"][i])
        g_next = W["norm_final"] if i == depth - 1 else W["norm_mix"][i + 1]
        h, y = _ple(out_t, h1, p[i].reshape(n, -1), _row(W["norm_ple"][i]), P["w_ple_gate"][i],
                    P["w_ple_proj"][i], _row(g_next))
    return (y.reshape(bsz, seqlen, d), jnp.stack(new_s), jnp.stack(new_c), jnp.stack(new_n),
            jnp.stack(new_m), jnp.stack(new_buf))


def kernel(x_prompt, x_sample, state_gla_S, state_mlstm_C, state_mlstm_n, state_mlstm_m, state_mlstm_conv,
           p_prompt, p_sample, w_gla_in, w_gla_gate, b_gla_gate, g_gla_norm, w_gla_out,
           w_ml_up, ml_conv_w, ml_conv_b, w_ml_q, w_ml_k, w_ml_v, w_ml_igate, b_ml_igate,
           w_ml_fgate, b_ml_fgate, g_ml_norm, ml_skip, w_ml_down,
           w_peer_query, peer_keys, peer_u, peer_v, norm_mix, norm_ffn, norm_ple,
           w_ple_gate, w_ple_proj, norm_final):
    W = dict(w_gla_in=w_gla_in, w_gla_gate=w_gla_gate, b_gla_gate=b_gla_gate, g_gla_norm=g_gla_norm,
             w_gla_out=w_gla_out, w_ml_up=w_ml_up, ml_conv_w=ml_conv_w, ml_conv_b=ml_conv_b,
             w_ml_q=w_ml_q, w_ml_k=w_ml_k, w_ml_v=w_ml_v, w_ml_igate=w_ml_igate, b_ml_igate=b_ml_igate,
             w_ml_fgate=w_ml_fgate, b_ml_fgate=b_ml_fgate, g_ml_norm=g_ml_norm, ml_skip=ml_skip,
             w_ml_down=w_ml_down, w_peer_query=w_peer_query, peer_keys=peer_keys, peer_u=peer_u,
             peer_v=peer_v, norm_mix=norm_mix, norm_ffn=norm_ffn, norm_ple=norm_ple,
             w_ple_gate=w_ple_gate, w_ple_proj=w_ple_proj, norm_final=norm_final)
    depth = norm_mix.shape[0]
    P = _prep_weights(W, depth)
    y_p, s_p, c_p, n_p, m_p, buf_p = _run_group(x_prompt, p_prompt, None, None, None, None, None, W, P, depth)
    y_s, s_s, c_s, n_s, m_s, buf_s = _run_group(x_sample, p_sample, state_gla_S, state_mlstm_C,
                                                state_mlstm_n, state_mlstm_m, state_mlstm_conv, W, P, depth)
    return (y_p, y_s, s_p, s_s, c_p, c_s, n_p, n_s, m_p, m_s, buf_p, buf_s)
```

```python
import functools

import jax
import jax.numpy as jnp
from jax import lax
from jax.experimental import pallas as pl
from jax.experimental.pallas import tpu as pltpu

F32 = jnp.float32
BF16 = jnp.bfloat16
EPS = 1e-6
CHUNK = 64
HIGHEST = lax.Precision.HIGHEST
NEG_INF = float("-inf")

GLA_HEADS = 4
GLA_GATE_NORMALIZER = 16.0
ML_HEADS = 4
ML_CONV = 4
ML_QKV_BLOCK = 4
PEER_HEADS = 8
PEER_NKEYS = 128
PEER_HALF = 64
PEER_TOPK = 16

VMEM_LIMIT_BYTES = 52 * 1024 * 1024
MXU_TILE = 256
PEER_EXPERT_STEP = 2048
PEER_EXPERT_PIECE = MXU_TILE
GELU_ARG_SCALE = 0.7071067811865476
GELU_OUT_SCALE = 0.5 / GELU_ARG_SCALE


def _cparams(sem):
    return pltpu.CompilerParams(dimension_semantics=sem, vmem_limit_bytes=VMEM_LIMIT_BYTES)


def _pick_block(n, candidates):
    for c in candidates:
        if n % c == 0:
            return c
    raise ValueError(f"no block size in {candidates} divides {n}")


def _rms(x, g):
    ms = jnp.mean(x * x, axis=-1, keepdims=True)
    return x * lax.rsqrt(ms + EPS) * g


def _log_sigmoid(x):
    return jnp.minimum(x, 0.0) - jnp.log1p(jnp.exp(-jnp.abs(x)))


def _silu(x):
    return x * jax.nn.sigmoid(x)


def _dot(a, b):
    return jnp.dot(a, b, preferred_element_type=F32)


def _dot_nt(a, b):
    return lax.dot_general(a, b, (((1,), (1,)), ((), ())), preferred_element_type=F32)


def _dot_tn(a, b):
    return lax.dot_general(a, b, (((0,), (0,)), ((), ())), preferred_element_type=F32)


def _tok(tb, d):
    return pl.BlockSpec((tb, d), lambda i: (i, 0))


def _full(shape):
    nd = len(shape)
    return pl.BlockSpec(shape, lambda *_: (0,) * nd)


def _gla_pre_kernel(h_ref, g_ref, w_ref, wgl_ref, wgate_ref, bgate_ref,
                    q_ref, k_ref, v_ref, r_ref, lg_ref, *, dk, dv, hk):
    xn = _rms(h_ref[...], g_ref[...]).astype(BF16)
    proj = _dot(xn, w_ref[...])
    q_ref[...] = proj[:, :dk] * (hk ** -0.5)
    k_ref[...] = proj[:, dk:2 * dk]
    v_ref[...] = proj[:, 2 * dk:2 * dk + dv].astype(BF16)
    r_ref[...] = proj[:, 2 * dk + dv:]
    gl = _dot(xn, wgl_ref[...])
    gate = _dot(gl.astype(BF16), wgate_ref[...]) + bgate_ref[...]
    lg_ref[...] = _log_sigmoid(gate) * (1.0 / GLA_GATE_NORMALIZER)


def _gla_pre(h, g, w_qkvr, w_gl, w_gate, b_gate, dk, dv):
    n, d = h.shape
    tb = _pick_block(n, (512, 256, 128, 64))
    hk = dk // GLA_HEADS
    outs = (jax.ShapeDtypeStruct((n, dk), F32), jax.ShapeDtypeStruct((n, dk), F32),
            jax.ShapeDtypeStruct((n, dv), BF16), jax.ShapeDtypeStruct((n, dv), F32),
            jax.ShapeDtypeStruct((n, dk), F32))
    return pl.pallas_call(
        functools.partial(_gla_pre_kernel, dk=dk, dv=dv, hk=hk),
        grid=(n // tb,),
        in_specs=[_tok(tb, d), _full(g.shape), _full(w_qkvr.shape), _full(w_gl.shape),
                  _full(w_gate.shape), _full(b_gate.shape)],
        out_specs=(_tok(tb, dk), _tok(tb, dk), _tok(tb, dv), _tok(tb, dv), _tok(tb, dk)),
        out_shape=outs,
        compiler_params=_cparams(("parallel",)),
        name="gla_pre",
    )(h, g, w_qkvr, w_gl, w_gate, b_gate)


GLA_SUB = 16


def _gla_scan_kernel(*refs, nchunks, zero_init):
    if zero_init:
        q_ref, k_ref, v_ref, lg_ref, gn_ref, o_ref, sout_ref, st_ref = refs
        s0_ref = None
    else:
        q_ref, k_ref, v_ref, lg_ref, gn_ref, s0_ref, o_ref, sout_ref, st_ref = refs
    t = pl.program_id(1)
    nheads, hv, hk = st_ref.shape

    @pl.when(t == 0)
    def _():
        if zero_init:
            st_ref[...] = jnp.zeros_like(st_ref)
        else:
            for h in range(nheads):
                st_ref[h] = s0_ref[h].T

    row = lax.broadcasted_iota(jnp.int32, (CHUNK, CHUNK), 0)
    col = lax.broadcasted_iota(jnp.int32, (CHUNK, CHUNK), 1)
    tril = (row >= col).astype(F32)
    gn = gn_ref[...]
    nsub = CHUNK // GLA_SUB

    def bmm(a, b, lhs_c, rhs_c):
        return lax.dot_general(a, b, (((lhs_c,), (rhs_c,)), ((0,), (0,))), preferred_element_type=F32)

    heads = range(nheads)

    for c in range(nchunks):
        sl = pl.ds(c * CHUNK, CHUNK)
        b_all = jnp.dot(tril, lg_ref[sl, :], precision=HIGHEST, preferred_element_type=F32)
        b = jnp.stack([b_all[:, h * hk:(h + 1) * hk] for h in heads])
        q = jnp.stack([q_ref[sl, h * hk:(h + 1) * hk] for h in heads])
        k = jnp.stack([k_ref[sl, h * hk:(h + 1) * hk] for h in heads])
        v = jnp.stack([v_ref[sl, h * hv:(h + 1) * hv] for h in heads])
        st = st_ref[...]
        b_last = b[:, CHUNK - 1:CHUNK, :]
        inter = bmm((q * jnp.exp(b)).astype(BF16), st.astype(BF16), 2, 2)
        intra_parts = []
        for i in range(nsub):
            lo, hi = i * GLA_SUB, (i + 1) * GLA_SUB
            b_ref_row = b[:, lo:lo + 1, :]
            qe = (q[:, lo:hi, :] * jnp.exp(b[:, lo:hi, :] - b_ref_row)).astype(BF16)
            ke = (k[:, :hi, :] * jnp.exp(b_ref_row - b[:, :hi, :])).astype(BF16)
            att = bmm(qe, ke, 2, 2)
            r_i = lax.broadcasted_iota(jnp.int32, (GLA_SUB, hi), 0) + lo
            c_i = lax.broadcasted_iota(jnp.int32, (GLA_SUB, hi), 1)
            att = jnp.where(r_i >= c_i, att, 0.0)
            intra_parts.append(bmm(att.astype(BF16), v[:, :hi, :], 2, 1))
        o = inter + jnp.concatenate(intra_parts, axis=1)
        kd = (k * jnp.exp(b_last - b)).astype(BF16)
        st_ref[...] = st * jnp.exp(b_last) + bmm(v, kd, 1, 1)
        o = o * lax.rsqrt(jnp.mean(o * o, axis=-1, keepdims=True) + EPS) * gn
        for h in heads:
            o_ref[sl, h * hv:(h + 1) * hv] = o[h]

    @pl.when(t == pl.num_programs(1) - 1)
    def _():
        for h in range(nheads):
            sout_ref[h] = st_ref[h].T


def _gla_scan(q, k, v, lg, gn, s0, nseq, seqlen):
    n, dk = q.shape
    dv = v.shape[1]
    hk, hv = dk // GLA_HEADS, dv // GLA_HEADS
    tt = _pick_block(seqlen, (256, 128, 64))
    nt = seqlen // tt
    zero_init = s0 is None
    tmap = lambda s, t: (s * nt + t, 0)
    smap = lambda s, t: (s, 0, 0, 0)
    in_specs = [pl.BlockSpec((tt, dk), tmap), pl.BlockSpec((tt, dk), tmap),
                pl.BlockSpec((tt, dv), tmap), pl.BlockSpec((tt, dk), tmap),
                pl.BlockSpec((1, hv), lambda s, t: (0, 0))]
    args = [q, k, v, lg, gn]
    if not zero_init:
        in_specs.append(pl.BlockSpec((None, GLA_HEADS, hk, hv), smap))
        args.append(s0)
    return pl.pallas_call(
        functools.partial(_gla_scan_kernel, nchunks=tt // CHUNK, zero_init=zero_init),
        grid=(nseq, nt),
        in_specs=in_specs,
        out_specs=(pl.BlockSpec((tt, dv), tmap), pl.BlockSpec((None, GLA_HEADS, hk, hv), smap)),
        out_shape=(jax.ShapeDtypeStruct((n, dv), F32),
                   jax.ShapeDtypeStruct((nseq, GLA_HEADS, hk, hv), F32)),
        scratch_shapes=[pltpu.VMEM((GLA_HEADS, hv, hk), F32)],
        compiler_params=_cparams(("parallel", "arbitrary")),
        name="gla_scan",
    )(*args)


def _peer_pre_tail(h1, gffn_ref, wqt_ref, xt_ref, qt_ref):
    xn = _rms(h1, gffn_ref[...])
    xt = xn.T.astype(BF16)
    xt_ref[...] = xt
    qt_ref[...] = _dot(wqt_ref[...], xt).astype(BF16)


def _gla_post_kernel(o_ref, r_ref, h_ref, wout_ref, gffn_ref, wqt_ref, h1_ref, xt_ref, qt_ref):
    y = (o_ref[...] * _silu(r_ref[...])).astype(BF16)
    h1 = h_ref[...] + _dot(y, wout_ref[...])
    h1_ref[...] = h1
    _peer_pre_tail(h1, gffn_ref, wqt_ref, xt_ref, qt_ref)


def _gla_post(o, r, h, w_out, g_ffn, w_qt):
    n, d = h.shape
    tb = _pick_block(n, (512, 256, 128))
    dq = w_qt.shape[0]
    return pl.pallas_call(
        _gla_post_kernel,
        grid=(n // tb,),
        in_specs=[_tok(tb, o.shape[1]), _tok(tb, r.shape[1]), _tok(tb, d), _full(w_out.shape),
                  _full(g_ffn.shape), _full(w_qt.shape)],
        out_specs=(_tok(tb, d), pl.BlockSpec((d, tb), lambda i: (0, i)),
                   pl.BlockSpec((dq, tb), lambda i: (0, i))),
        out_shape=(jax.ShapeDtypeStruct((n, d), F32), jax.ShapeDtypeStruct((d, n), BF16),
                   jax.ShapeDtypeStruct((dq, n), BF16)),
        compiler_params=_cparams(("parallel",)),
        name="gla_post",
    )(o, r, h, w_out, g_ffn, w_qt)


def _ml_pre_kernel(h_ref, g_ref, w_ref, xm_ref, z_ref, *, di):
    xn = _rms(h_ref[...], g_ref[...]).astype(BF16)
    up = _dot(xn, w_ref[...])
    xm_ref[...] = up[:, :di]
    z_ref[...] = up[:, di:]


def _ml_pre(h, g, w_up):
    n, d = h.shape
    di = w_up.shape[1] // 2
    tb = _pick_block(n, (512, 256, 128, 64))
    return pl.pallas_call(
        functools.partial(_ml_pre_kernel, di=di),
        grid=(n // tb,),
        in_specs=[_tok(tb, d), _full(g.shape), _full(w_up.shape)],
        out_specs=(_tok(tb, di), _tok(tb, di)),
        out_shape=(jax.ShapeDtypeStruct((n, di), F32), jax.ShapeDtypeStruct((n, di), F32)),
        compiler_params=_cparams(("parallel",)),
        name="ml_pre",
    )(h, g, w_up)


HALO = 8


def _ml_qkv_kernel(xm_ref, prev_ref, buf_ref, cw_ref, cb_ref, wq_ref, wk_ref, wv_ref, wif_ref, bif_ref,
                   q_ref, k_ref, v_ref, xc_ref, gates_ref, xp_scr, *, tt, di, hd):
    t = pl.program_id(1)
    x = xm_ref[...]
    halo = jnp.where(t == 0, buf_ref[...], prev_ref[...])
    xp_scr[0:HALO, :] = halo
    xp_scr[HALO:HALO + tt, :] = x
    y = cb_ref[...]
    for j in range(ML_CONV):
        y = y + cw_ref[j:j + 1, :] * xp_scr[pl.ds(HALO - (ML_CONV - 1) + j, tt), :]
    xc = _silu(y)
    xc_ref[...] = xc
    xcb = xc.astype(BF16)
    xmb = x.astype(BF16)
    gates = jnp.zeros((tt, gates_ref.shape[1]), F32) + bif_ref[...]
    ntile = di // MXU_TILE
    for j in range(ntile):
        cs = slice(j * MXU_TILE, (j + 1) * MXU_TILE)
        qj = _dot(xcb[:, cs], wq_ref[j])
        kj = _dot(xcb[:, cs], wk_ref[j])
        vj = _dot(xmb[:, cs], wv_ref[j])
        q_ref[:, cs] = qj.astype(BF16)
        k_ref[:, cs] = kj * (hd ** -0.5)
        v_ref[:, cs] = vj.astype(BF16)
        gates = gates + _dot(qj.astype(BF16), wif_ref[j * MXU_TILE:(j + 1) * MXU_TILE, :])
        gates = gates + _dot(kj.astype(BF16), wif_ref[di + j * MXU_TILE:di + (j + 1) * MXU_TILE, :])
        gates = gates + _dot(vj.astype(BF16), wif_ref[2 * di + j * MXU_TILE:2 * di + (j + 1) * MXU_TILE, :])
    lane = lax.broadcasted_iota(jnp.int32, gates.shape, 1)
    is_f = (lane >= ML_HEADS) & (lane < 2 * ML_HEADS)
    gates_ref[...] = jnp.where(is_f, _log_sigmoid(gates), gates)


def _ml_qkv(xm, buf, conv_w, conv_b, wq, wk, wv, wif, bif, nseq, seqlen):
    n, di = xm.shape
    hd = di // ML_HEADS
    tt = _pick_block(seqlen, (256, 128, 64))
    nt = seqlen // tt
    tmap = lambda s, t: (s * nt + t, 0)
    prev_map = lambda s, t: (jnp.maximum((s * nt + t) * (tt // HALO) - 1, 0), 0)
    ng = wif.shape[1]
    return pl.pallas_call(
        functools.partial(_ml_qkv_kernel, tt=tt, di=di, hd=hd),
        grid=(nseq, nt),
        in_specs=[pl.BlockSpec((tt, di), tmap), pl.BlockSpec((HALO, di), prev_map),
                  pl.BlockSpec((None, HALO, di), lambda s, t: (s, 0, 0)),
                  _full(conv_w.shape), _full(conv_b.shape), _full(wq.shape), _full(wk.shape),
                  _full(wv.shape), _full(wif.shape), _full(bif.shape)],
        out_specs=(pl.BlockSpec((tt, di), tmap), pl.BlockSpec((tt, di), tmap),
                   pl.BlockSpec((tt, di), tmap), pl.BlockSpec((tt, di), tmap),
                   pl.BlockSpec((tt, ng), tmap)),
        out_shape=(jax.ShapeDtypeStruct((n, di), BF16), jax.ShapeDtypeStruct((n, di), F32),
                   jax.ShapeDtypeStruct((n, di), BF16), jax.ShapeDtypeStruct((n, di), F32),
                   jax.ShapeDtypeStruct((n, ng), F32)),
        scratch_shapes=[pltpu.VMEM((HALO + tt, di), F32)],
        compiler_params=_cparams(("parallel", "arbitrary")),
        name="ml_qkv",
    )(xm, xm, buf, conv_w, conv_b, wq, wk, wv, wif, bif)


def _ml_scan_kernel(*refs, nchunks, zero_init):
    if zero_init:
        q_ref, k_ref, v_ref, g_ref, h_ref, cout_ref, nout_ref, mout_ref, c_scr, n_scr, m_scr = refs
    else:
        (q_ref, k_ref, v_ref, g_ref, c0_ref, n0_ref, m0_ref,
         h_ref, cout_ref, nout_ref, mout_ref, c_scr, n_scr, m_scr) = refs
    t = pl.program_id(1)
    nheads, hd = c_scr.shape[0], c_scr.shape[1]

    @pl.when(t == 0)
    def _():
        if zero_init:
            c_scr[...] = jnp.zeros_like(c_scr)
            n_scr[...] = jnp.zeros_like(n_scr)
            m_scr[...] = jnp.zeros_like(m_scr)
        else:
            c_scr[...] = c0_ref[...]
            n_scr[...] = n0_ref[...]
            m_scr[...] = m0_ref[...]

    row = lax.broadcasted_iota(jnp.int32, (CHUNK, CHUNK), 0)
    col = lax.broadcasted_iota(jnp.int32, (CHUNK, CHUNK), 1)
    causal = row >= col
    tril = causal.astype(F32)
    eye = row == col
    ones = jnp.ones((CHUNK, CHUNK), F32)

    def bmm(a, b, lhs_c, rhs_c, precision=None):
        return lax.dot_general(a, b, (((lhs_c,), (rhs_c,)), ((0,), (0,))), precision=precision,
                               preferred_element_type=F32)

    heads = range(nheads)
    ones_h = jnp.ones((nheads, CHUNK, CHUNK), F32)

    for c in range(nchunks):
        sl = pl.ds(c * CHUNK, CHUNK)
        q = jnp.stack([q_ref[sl, h * hd:(h + 1) * hd] for h in heads])
        kf = jnp.stack([k_ref[sl, h * hd:(h + 1) * hd] for h in heads])
        v = jnp.stack([v_ref[sl, h * hd:(h + 1) * hd] for h in heads])
        g = g_ref[sl, :]
        cmat = c_scr[...]
        nvec = n_scr[...]
        m = m_scr[...][:, :, 0:1]
        gcum = jnp.dot(tril, g, precision=HIGHEST, preferred_element_type=F32)
        i_col = jnp.stack([g[:, h:h + 1] for h in heads])
        f_col = jnp.stack([gcum[:, h + nheads:h + nheads + 1] for h in heads])
        w_row = bmm(ones_h, jnp.where(eye, i_col - f_col, 0.0), 2, 1, precision=HIGHEST)
        log_d = jnp.where(causal, f_col + w_row, NEG_INF)
        inter = f_col + m
        mt = jnp.maximum(inter, jnp.max(log_d, axis=-1, keepdims=True))
        sc = bmm(q, kf.astype(BF16), 2, 2) * jnp.exp(log_d - mt)
        a = jnp.exp(inter - mt)
        num = a * bmm(q, cmat.astype(BF16), 2, 1) + bmm(sc.astype(BF16), v, 2, 1)
        qn = jnp.sum(q.astype(F32) * nvec, axis=-1, keepdims=True)
        den = a * qn + jnp.sum(sc, axis=-1, keepdims=True)
        hh = num / jnp.maximum(jnp.abs(den), jnp.exp(-mt))
        m_new = mt[:, CHUNK - 1:CHUNK, :]
        f_last = f_col[:, CHUNK - 1:CHUNK, :]
        w_end = jnp.exp(f_last - f_col + i_col - m_new)
        a_end = jnp.exp(f_last + m - m_new)
        kw = kf * w_end
        c_scr[...] = a_end * cmat + bmm(kw.astype(BF16), v, 1, 1)
        n_scr[...] = a_end * nvec + jnp.sum(kw, axis=1, keepdims=True)
        m_scr[...] = jnp.broadcast_to(m_new, m_scr.shape)
        mu = jnp.mean(hh, axis=-1, keepdims=True)
        var = jnp.mean(jnp.square(hh - mu), axis=-1, keepdims=True)
        hn = (hh - mu) * lax.rsqrt(var + EPS)
        for h in heads:
            h_ref[sl, h * hd:(h + 1) * hd] = hn[h]

    @pl.when(t == pl.num_programs(1) - 1)
    def _():
        cout_ref[...] = c_scr[...]
        nout_ref[...] = n_scr[...]
        mout_ref[...] = m_scr[...]


M_LANES = 128


def _ml_scan(q, k, v, gates, state, nseq, seqlen):
    n, di = q.shape
    hd = di // ML_HEADS
    tt = _pick_block(seqlen, (256, 128, 64))
    nt = seqlen // tt
    zero_init = state is None
    tmap = lambda s, t: (s * nt + t, 0)
    smap = lambda s, t: (s, 0, 0, 0)
    ng = gates.shape[1]
    nh = ML_HEADS
    in_specs = [pl.BlockSpec((tt, di), tmap), pl.BlockSpec((tt, di), tmap),
                pl.BlockSpec((tt, di), tmap), pl.BlockSpec((tt, ng), tmap)]
    args = [q, k, v, gates]
    state_specs = [pl.BlockSpec((None, nh, hd, hd), smap), pl.BlockSpec((None, nh, 1, hd), smap),
                   pl.BlockSpec((None, nh, 1, M_LANES), smap)]
    if not zero_init:
        in_specs += state_specs
        args += list(state)
    return pl.pallas_call(
        functools.partial(_ml_scan_kernel, nchunks=tt // CHUNK, zero_init=zero_init),
        grid=(nseq, nt),
        in_specs=in_specs,
        out_specs=(pl.BlockSpec((tt, di), tmap), *state_specs),
        out_shape=(jax.ShapeDtypeStruct((n, di), F32),
                   jax.ShapeDtypeStruct((nseq, nh, hd, hd), F32),
                   jax.ShapeDtypeStruct((nseq, nh, 1, hd), F32),
                   jax.ShapeDtypeStruct((nseq, nh, 1, M_LANES), F32)),
        scratch_shapes=[pltpu.VMEM((nh, hd, hd), F32), pltpu.VMEM((nh, 1, hd), F32),
                        pltpu.VMEM((nh, 1, M_LANES), F32)],
        compiler_params=_cparams(("parallel", "arbitrary")),
        name="ml_scan",
    )(*args)


def _ml_post_kernel(hn_ref, xc_ref, z_ref, h_ref, gn_ref, skip_ref, wdown_ref, gffn_ref, wqt_ref,
                    h1_ref, xt_ref, qt_ref):
    y = (hn_ref[...] * gn_ref[...] + skip_ref[...] * xc_ref[...]) * _silu(z_ref[...])
    h1 = h_ref[...] + _dot(y.astype(BF16), wdown_ref[...])
    h1_ref[...] = h1
    _peer_pre_tail(h1, gffn_ref, wqt_ref, xt_ref, qt_ref)


def _ml_post(hn, xc, z, h, g_norm, skip, w_down, g_ffn, w_qt):
    n, d = h.shape
    di = hn.shape[1]
    tb = _pick_block(n, (512, 256, 128))
    dq = w_qt.shape[0]
    return pl.pallas_call(
        _ml_post_kernel,
        grid=(n // tb,),
        in_specs=[_tok(tb, di), _tok(tb, di), _tok(tb, di), _tok(tb, d), _full(g_norm.shape),
                  _full(skip.shape), _full(w_down.shape), _full(g_ffn.shape), _full(w_qt.shape)],
        out_specs=(_tok(tb, d), pl.BlockSpec((d, tb), lambda i: (0, i)),
                   pl.BlockSpec((dq, tb), lambda i: (0, i))),
        out_shape=(jax.ShapeDtypeStruct((n, d), F32), jax.ShapeDtypeStruct((d, n), BF16),
                   jax.ShapeDtypeStruct((dq, n), BF16)),
        compiler_params=_cparams(("parallel",)),
        name="ml_post",
    )(hn, xc, z, h, g_norm, skip, w_down, g_ffn, w_qt)


NCAND_ROWS = 72


def _first_max(vals, ids, big):
    m = jnp.max(vals, axis=0, keepdims=True)
    sel = jnp.min(jnp.where(vals == m, ids, big), axis=0, keepdims=True)
    return m, sel


def _sorting_network(n):
    size = 1
    while size < n:
        size *= 2
    pairs = []
    p = 1
    while p < size:
        k = p
        while k >= 1:
            for j in range(k % p, size - k, 2 * k):
                for i in range(min(k, size - j - k)):
                    if (i + j) // (2 * p) == (i + j + k) // (2 * p):
                        pairs.append((i + j, i + j + k))
            k //= 2
        p *= 2
    return [(i, j) for i, j in pairs if j < n]


SUBLANES = 8


def _sorted_best16(s):
    k16 = PEER_TOPK
    rows, lanes = s.shape
    nslab = rows // SUBLANES
    v = [s[SUBLANES * j:SUBLANES * (j + 1), :] for j in range(nslab)]
    for i, j in _sorting_network(nslab):
        v[i], v[j] = jnp.maximum(v[i], v[j]), jnp.minimum(v[i], v[j])
    v = v[:k16] + [jnp.full((SUBLANES, lanes), NEG_INF, F32)] * (k16 - nslab)
    for shift in (1, 2, 4):
        m = [jnp.maximum(v[i], pltpu.roll(v[k16 - 1 - i], shift, axis=0)) for i in range(k16)]
        d = k16 // 2
        while d >= 1:
            for i in range(k16):
                if (i // d) % 2 == 0:
                    m[i], m[i + d] = jnp.maximum(m[i], m[i + d]), jnp.minimum(m[i], m[i + d])
            d //= 2
        v = m
    return v


def _tie_flag(tops, member):
    tie = jnp.zeros_like(tops[0])
    for r in range(len(tops) - 1):
        tie = jnp.maximum(tie, (tops[r] == tops[r + 1]).astype(F32))
    count = jnp.sum(member.astype(F32), axis=0, keepdims=True)
    return jnp.maximum(jnp.max(tie, axis=0, keepdims=True), (count != float(PEER_TOPK)).astype(F32))


def _peer_route_kernel(qt_ref, keys_ref, r2_ref, e2_ref, n_ref, e1_ref, s_scr, rank_scr, top_scr, *, tk):
    nk, k16 = PEER_NKEYS, PEER_TOPK
    key_id = lax.broadcasted_iota(jnp.int32, (nk, tk), 0)
    slot_id = lax.broadcasted_iota(jnp.int32, (k16, tk), 0)

    def half_body(hc, carry):
        start = pl.multiple_of(hc * PEER_HALF, PEER_HALF)
        s = _dot(keys_ref[hc], qt_ref[pl.ds(start, PEER_HALF), :])
        s_scr[hc] = s

        tops = _sorted_best16(s)
        s3 = s.reshape(nk // SUBLANES, SUBLANES, tk)
        member = s3 >= tops[k16 - 1][None]
        tie = _tie_flag(tops, member.reshape(nk, tk))

        def by_value():
            rank = jnp.where(member, float(k16 - 1), float(k16))
            for r in range(k16 - 2, -1, -1):
                rank = jnp.where(s3 >= tops[r][None], float(r), rank)
            sub = lax.broadcasted_iota(jnp.int32, (SUBLANES, tk), 0)
            halves = []
            for base in range(0, k16, SUBLANES):
                slab = tops[base]
                for r in range(1, SUBLANES):
                    slab = jnp.where(sub == r, tops[base + r], slab)
                halves.append(slab)
            return rank.reshape(nk, tk), jnp.concatenate(halves, axis=0)

        def by_value_and_index():
            def pick(r, st):
                s_cur, rank, top = st
                m, sel = _first_max(s_cur, key_id, nk)
                hit = key_id == sel
                return (jnp.where(hit, NEG_INF, s_cur), jnp.where(hit, jnp.asarray(r, F32), rank),
                        jnp.where(slot_id == r, m, top))

            _, rank, top = lax.fori_loop(
                0, k16, pick, (s, jnp.full((nk, tk), float(k16), F32), jnp.zeros((k16, tk), F32)))
            return rank, top

        rank, top = lax.cond(jnp.max(tie) > 0.0, by_value_and_index, by_value)
        rank_scr[hc] = rank
        top_scr[hc] = top
        return carry

    lax.fori_loop(0, 2 * PEER_HEADS, half_body, 0)

    def cand_ids():
        i16 = lax.broadcasted_iota(jnp.int32, (16, tk), 0)
        i8 = lax.broadcasted_iota(jnp.int32, (8, tk), 0)
        parts = [i16, 16 + i8, 32 + i8, 48 + i8, 16 * i16, 16 * i8 + 1, 16 * i8 + 2]
        return jnp.concatenate(parts, axis=0)

    cand_id = cand_ids()
    row72 = lax.broadcasted_iota(jnp.int32, (NCAND_ROWS, tk), 0)
    cand_ok = (row72 < 40) | (cand_id >= 64)
    big_id = 16 * 16

    def head_body(h, carry):
        a = top_scr[2 * h]
        b = top_scr[2 * h + 1]
        parts = [a[0:1, :] + b, a[1:2, :] + b[0:8, :], a[2:3, :] + b[0:8, :], a[3:4, :] + b[0:8, :],
                 a + b[0:1, :], a[0:8, :] + b[1:2, :], a[0:8, :] + b[2:3, :]]
        cand = jnp.where(cand_ok, jnp.concatenate(parts, axis=0), NEG_INF)
        tops = _sorted_best16(cand)
        cand3 = cand.reshape(NCAND_ROWS // SUBLANES, SUBLANES, tk)
        sel = (cand3 >= tops[k16 - 1][None]).reshape(NCAND_ROWS, tk)
        tie = _tie_flag(tops, sel)

        def by_value():
            self = sel.astype(F32)
            cnt = self[40:56, :] + jnp.concatenate(
                [self[56:64, :] + self[64:72, :], jnp.zeros((8, tk), F32)], axis=0)
            for r1, (lo, hi) in enumerate(((0, 16), (16, 24), (24, 32), (32, 40))):
                row = jnp.sum(self[lo:hi, :], axis=0, keepdims=True)
                cnt = cnt + jnp.where(slot_id == r1, row, 0.0)
            ex = jnp.exp(cand3 - tops[0][None]).reshape(NCAND_ROWS, tk)
            z = jnp.sum(jnp.where(sel, ex, 0.0), axis=0, keepdims=True)
            return cnt, z

        def by_value_and_index():
            def pick(r, st):
                cand_cur, cnt, tsel = st
                m, sel = _first_max(cand_cur, cand_id, big_id)
                cand_cur = jnp.where(cand_id == sel, NEG_INF, cand_cur)
                cnt = cnt + (slot_id == (sel >> 4)).astype(F32)
                return cand_cur, cnt, jnp.where(slot_id == r, m, tsel)

            _, cnt, tsel = lax.fori_loop(
                0, k16, pick, (cand, jnp.zeros((k16, tk), F32), jnp.zeros((k16, tk), F32)))
            return cnt, jnp.sum(jnp.exp(tsel - tsel[0:1, :]), axis=0, keepdims=True)

        cnt, z = lax.cond(jnp.max(tie) > 0.0, by_value_and_index, by_value)
        rank1 = rank_scr[2 * h]
        rank2 = rank_scr[2 * h + 1]
        pk = 16
        rank1b = rank1.astype(BF16).reshape(nk // pk, pk, tk)
        n_sel = jnp.zeros((nk // pk, pk, tk), BF16)
        for r in range(k16):
            row = jnp.broadcast_to(cnt[r:r + 1, :], (pk, tk)).astype(BF16)
            n_sel = jnp.where(rank1b == r, row[None], n_sel)
        n_ref[h] = n_sel.reshape(nk, tk).astype(F32)
        e1_ref[h] = jnp.where(rank1 < k16, jnp.exp(s_scr[2 * h] - a[0:1, :]) * (GELU_OUT_SCALE / z), 0.0)
        r2_ref[h] = rank2.astype(BF16)
        e2_ref[h] = jnp.where(rank2 < k16, jnp.exp(s_scr[2 * h + 1] - b[0:1, :]), 0.0).astype(BF16)
        return carry

    lax.fori_loop(0, PEER_HEADS, head_body, 0)


def _peer_route(qt, keys):
    dq, n = qt.shape
    tk = _pick_block(n, (256, 128))
    nk = PEER_NKEYS
    shp = jax.ShapeDtypeStruct((PEER_HEADS, nk, n), F32)
    shp16 = jax.ShapeDtypeStruct((PEER_HEADS, nk, n), BF16)
    ospec = pl.BlockSpec((PEER_HEADS, nk, tk), lambda i: (0, 0, i))
    return pl.pallas_call(
        functools.partial(_peer_route_kernel, tk=tk),
        grid=(n // tk,),
        in_specs=[pl.BlockSpec((dq, tk), lambda i: (0, i)), _full(keys.shape)],
        out_specs=(ospec, ospec, ospec, ospec),
        out_shape=(shp16, shp16, shp, shp),
        scratch_shapes=[pltpu.VMEM((2 * PEER_HEADS, nk, tk), F32), pltpu.VMEM((2 * PEER_HEADS, nk, tk), F32),
                        pltpu.VMEM((2 * PEER_HEADS, PEER_TOPK, tk), F32)],
        compiler_params=_cparams(("parallel",)),
        name="peer_route",
    )(qt, keys)


def _peer_gate_stage(row0, act_ref, coef_ref, r2_ref, e2_ref, n_ref, e1_ref):
    nk = PEER_NKEYS
    t = act_ref.shape[1]
    pk = 16
    zero = jnp.zeros((), BF16)
    for al in range(PEER_EXPERT_PIECE // nk):
        a = row0 + al
        rows = slice(al * nk, (al + 1) * nk)
        y = act_ref[rows, :]
        ge = (y * (1.0 + lax.erf(y))).astype(BF16).reshape(nk // pk, pk, t)
        g = None
        for h in range(PEER_HEADS):
            n16 = jnp.broadcast_to(n_ref[h, a:a + 1, :], (pk, t)).astype(BF16)
            e16 = jnp.broadcast_to(e1_ref[h, a:a + 1, :], (pk, t)).astype(BF16)
            r2h = r2_ref[h].reshape(nk // pk, pk, t)
            e2h = e2_ref[h].reshape(nk // pk, pk, t)
            term = jnp.where(r2h < n16[None], e2h, zero) * e16[None]
            g = term if g is None else g + term
        coef_ref[rows, :] = (g * ge).reshape(nk, t)


def _peer_expert_kernel(xt_ref, r2_ref, e2_ref, n_ref, e1_ref, u_ref, vt_ref, out_ref, act_scr, coef_scr):
    k = pl.program_id(1)
    ep = PEER_EXPERT_PIECE
    npiece = act_scr.shape[0]
    rows = ep // PEER_NKEYS

    @pl.when(k == 0)
    def _():
        out_ref[...] = jnp.zeros_like(out_ref)

    xt = xt_ref[...]
    for p in range(npiece):
        act_scr[p] = _dot(u_ref[p * ep:(p + 1) * ep, :], xt)
    for p in range(npiece):
        _peer_gate_stage(p * rows, act_scr.at[p], coef_scr.at[p], r2_ref, e2_ref, n_ref, e1_ref)
    for p in range(npiece):
        out_ref[...] += _dot(vt_ref[:, p * ep:(p + 1) * ep], coef_scr[p])


def _peer_expert(xt, r2, e2, nsel, e1, u, vt):
    d, n = xt.shape
    ne = u.shape[0]
    t = _pick_block(n, (512, 256, 128))
    et = PEER_EXPERT_STEP
    npiece = et // PEER_EXPERT_PIECE
    rspec = pl.BlockSpec((PEER_HEADS, PEER_NKEYS, t), lambda i, k: (0, 0, i))
    row_spec = pl.BlockSpec((PEER_HEADS, et // PEER_NKEYS, t), lambda i, k: (0, k, i))
    return pl.pallas_call(
        _peer_expert_kernel,
        grid=(n // t, ne // et),
        in_specs=[pl.BlockSpec((d, t), lambda i, k: (0, i)), rspec, rspec, row_spec, row_spec,
                  pl.BlockSpec((et, d), lambda i, k: (k, 0)),
                  pl.BlockSpec((d, et), lambda i, k: (0, k))],
        out_specs=pl.BlockSpec((d, t), lambda i, k: (0, i)),
        out_shape=jax.ShapeDtypeStruct((d, n), F32),
        scratch_shapes=[pltpu.VMEM((npiece, PEER_EXPERT_PIECE, t), F32),
                        pltpu.VMEM((npiece, PEER_EXPERT_PIECE, t), BF16)],
        compiler_params=_cparams(("parallel", "arbitrary")),
        name="peer_expert",
    )(xt, r2, e2, nsel, e1, u, vt)


def _ple_kernel(ot_ref, h1_ref, p_ref, gple_ref, wgate_ref, wproj_ref, gnext_ref, h_ref, y_ref):
    h2 = h1_ref[...] + ot_ref[...].T
    gate = jax.nn.sigmoid(_dot(_rms(h2, gple_ref[...]).astype(BF16), wgate_ref[...]))
    h3 = h2 + gate * _dot(p_ref[...].astype(BF16), wproj_ref[...])
    h_ref[...] = h3
    y_ref[...] = _rms(h3, gnext_ref[...])


def _ple(out_t, h1, p, g_ple, w_gate, w_proj, g_next):
    n, d = h1.shape
    tb = _pick_block(n, (512, 256, 128))
    return pl.pallas_call(
        _ple_kernel,
        grid=(n // tb,),
        in_specs=[pl.BlockSpec((d, tb), lambda i: (0, i)), _tok(tb, d), _tok(tb, p.shape[1]),
                  _full(g_ple.shape), _full(w_gate.shape), _full(w_proj.shape), _full(g_next.shape)],
        out_specs=(_tok(tb, d), _tok(tb, d)),
        out_shape=(jax.ShapeDtypeStruct((n, d), F32), jax.ShapeDtypeStruct((n, d), F32)),
        compiler_params=_cparams(("parallel",)),
        name="ple",
    )(out_t, h1, p, g_ple, w_gate, w_proj, g_next)


def _block_diag_tiles(w):
    g, bi, bo = w.shape
    per = MXU_TILE // bi
    wt = w.reshape(g // per, per, bi, bo)
    eye = jnp.eye(per, dtype=w.dtype)
    dense = jnp.einsum("tgio,gh->tgiho", wt, eye)
    return dense.reshape(g // per, per * bi, per * bo)


def _row(x):
    return x.reshape(1, -1)


def _prep_weights(W, depth):
    P = {}
    n_gla = W["w_gla_in"].shape[0]
    dk = W["w_gla_gate"].shape[2]
    rank = W["w_gla_gate"].shape[1]
    dv = (W["w_gla_in"].shape[2] - rank - 2 * dk) // 2
    P["gla_dims"] = (dk, dv)
    pad = 128 - rank
    P["w_gla_qkvr"] = W["w_gla_in"][:, :, :2 * dk + 2 * dv].astype(BF16)
    P["w_gla_gl"] = jnp.pad(W["w_gla_in"][:, :, 2 * dk + 2 * dv:], ((0, 0), (0, 0), (0, pad))).astype(BF16)
    P["w_gla_gate"] = jnp.pad(W["w_gla_gate"], ((0, 0), (0, pad), (0, 0))).astype(BF16)
    P["w_gla_out"] = W["w_gla_out"].astype(BF16)
    P["w_ml_up"] = W["w_ml_up"].astype(BF16)
    n_ml = W["w_ml_up"].shape[0]
    for nm in ("w_ml_q", "w_ml_k", "w_ml_v"):
        P[nm] = jnp.stack([_block_diag_tiles(W[nm][j]) for j in range(n_ml)]).astype(BF16)
    wif = jnp.concatenate([W["w_ml_igate"], W["w_ml_fgate"]], axis=-1)
    P["w_ml_if"] = jnp.pad(wif, ((0, 0), (0, 0), (0, 128 - wif.shape[-1]))).astype(BF16)
    bif = jnp.concatenate([W["b_ml_igate"], W["b_ml_fgate"]], axis=-1)
    P["b_ml_if"] = jnp.pad(bif, ((0, 0), (0, 128 - bif.shape[-1])))
    P["w_ml_down"] = W["w_ml_down"].astype(BF16)
    P["w_peer_qt"] = jnp.swapaxes(W["w_peer_query"], 1, 2).astype(BF16)
    keys = W["peer_keys"]
    P["peer_keys"] = jnp.swapaxes(keys, 1, 2).reshape(depth, 2 * PEER_HEADS, PEER_NKEYS, PEER_HALF).astype(BF16)
    P["peer_u"] = (W["peer_u"] * GELU_ARG_SCALE).astype(BF16)
    P["peer_vt"] = jnp.swapaxes(W["peer_v"], 1, 2).astype(BF16)
    P["w_ple_gate"] = W["w_ple_gate"].astype(BF16)
    P["w_ple_proj"] = W["w_ple_proj"].astype(BF16)
    del n_gla
    return P


def _run_group(x, p, s_gla, c_ml, n_ml, m_ml, buf_ml, W, P, depth):
    bsz, seqlen, d = x.shape
    n = bsz * seqlen
    h = x.reshape(n, d)
    dk, dv = P["gla_dims"]
    new_s, new_c, new_n, new_m, new_buf = [], [], [], [], []
    y = None
    for i in range(depth):
        j = i // 2
        if i % 2 == 0:
            q, k, v, r, lg = _gla_pre(h, _row(W["norm_mix"][i]), P["w_gla_qkvr"][j], P["w_gla_gl"][j],
                                      P["w_gla_gate"][j], _row(W["b_gla_gate"][j]), dk, dv)
            s0 = None if s_gla is None else s_gla[j]
            o, s_fin = _gla_scan(q, k, v, lg, _row(W["g_gla_norm"][j]), s0, bsz, seqlen)
            new_s.append(s_fin)
            h1, xt, qt = _gla_post(o, r, h, P["w_gla_out"][j], _row(W["norm_ffn"][i]), P["w_peer_qt"][i])
        else:
            xm, z = _ml_pre(h, _row(W["norm_mix"][i]), P["w_ml_up"][j])
            di = xm.shape[1]
            if buf_ml is None:
                buf = jnp.zeros((bsz, HALO, di), F32)
            else:
                buf = jnp.pad(buf_ml[j], ((0, 0), (HALO - (ML_CONV - 1), 0), (0, 0)))
            qm, km, vm, xc, gates = _ml_qkv(xm, buf, W["ml_conv_w"][j], _row(W["ml_conv_b"][j]),
                                            P["w_ml_q"][j], P["w_ml_k"][j], P["w_ml_v"][j],
                                            P["w_ml_if"][j], _row(P["b_ml_if"][j]), bsz, seqlen)
            if c_ml is None:
                state = None
            else:
                state = (c_ml[j], n_ml[j][:, :, None, :],
                         jnp.broadcast_to(m_ml[j][:, :, None, None], m_ml[j].shape + (1, M_LANES)))
            hn, c_fin, n_fin, m_fin = _ml_scan(qm, km, vm, gates, state, bsz, seqlen)
            new_c.append(c_fin)
            new_n.append(n_fin[:, :, 0, :])
            new_m.append(m_fin[:, :, 0, 0])
            xm3 = xm.reshape(bsz, seqlen, di)
            if seqlen >= ML_CONV - 1:
                new_buf.append(xm3[:, seqlen - (ML_CONV - 1):, :])
            else:
                new_buf.append(jnp.concatenate([buf[:, HALO - (ML_CONV - 1):, :], xm3], axis=1)[:, -(ML_CONV - 1):, :])
            h1, xt, qt = _ml_post(hn, xc, z, h, _row(W["g_ml_norm"][j]), _row(W["ml_skip"][j]),
                                  P["w_ml_down"][j], _row(W["norm_ffn"][i]), P["w_peer_qt"][i])
        r2, e2, nsel, e1 = _peer_route(qt, P["peer_keys"][i])
        out_t = _peer_expert(xt, r2, e2, nsel, e1, P["peer_u"][i], P["peer_vt"][i])
        g_next = W["norm_final"] if i == depth - 1 else W["norm_mix"][i + 1]
        h, y = _ple(out_t, h1, p[i].reshape(n, -1), _row(W["norm_ple"][i]), P["w_ple_gate"][i],
                    P["w_ple_proj"][i], _row(g_next))
    return (y.reshape(bsz, seqlen, d), jnp.stack(new_s), jnp.stack(new_c), jnp.stack(new_n),
            jnp.stack(new_m), jnp.stack(new_buf))


def kernel(x_prompt, x_sample, state_gla_S, state_mlstm_C, state_mlstm_n, state_mlstm_m, state_mlstm_conv,
           p_prompt, p_sample, w_gla_in, w_gla_gate, b_gla_gate, g_gla_norm, w_gla_out,
           w_ml_up, ml_conv_w, ml_conv_b, w_ml_q, w_ml_k, w_ml_v, w_ml_igate, b_ml_igate,
           w_ml_fgate, b_ml_fgate, g_ml_norm, ml_skip, w_ml_down,
           w_peer_query, peer_keys, peer_u, peer_v, norm_mix, norm_ffn, norm_ple,
           w_ple_gate, w_ple_proj, norm_final):
    W = dict(w_gla_in=w_gla_in, w_gla_gate=w_gla_gate, b_gla_gate=b_gla_gate, g_gla_norm=g_gla_norm,
             w_gla_out=w_gla_out, w_ml_up=w_ml_up, ml_conv_w=ml_conv_w, ml_conv_b=ml_conv_b,
             w_ml_q=w_ml_q, w_ml_k=w_ml_k, w_ml_v=w_ml_v, w_ml_igate=w_ml_igate, b_ml_igate=b_ml_igate,
             w_ml_fgate=w_ml_fgate, b_ml_fgate=b_ml_fgate, g_ml_norm=g_ml_norm, ml_skip=ml_skip,
             w_ml_down=w_ml_down, w_peer_query=w_peer_query, peer_keys=peer_keys, peer_u=peer_u,
             peer_v=peer_v, norm_mix=norm_mix, norm_ffn=norm_ffn, norm_ple=norm_ple,
             w_ple_gate=w_ple_gate, w_ple_proj=w_ple_proj, norm_final=norm_final)
    depth = norm_mix.shape[0]
    P = _prep_weights(W, depth)
    y_p, s_p, c_p, n_p, m_p, buf_p = _run_group(x_prompt, p_prompt, None, None, None, None, None, W, P, depth)
    y_s, s_s, c_s, n_s, m_s, buf_s = _run_group(x_sample, p_sample, state_gla_S, state_mlstm_C,
                                                state_mlstm_n, state_mlstm_m, state_mlstm_conv, W, P, depth)
    return (y_p, y_s, s_p, s_s, c_p, c_s, n_p, n_s, m_p, m_s, buf_p, buf_s)
```

```python
import functools

import jax
import jax.numpy as jnp
from jax import lax
from jax.experimental import pallas as pl
from jax.experimental.pallas import tpu as pltpu

F32 = jnp.float32
BF16 = jnp.bfloat16
EPS = 1e-6
CHUNK = 64
HIGHEST = lax.Precision.HIGHEST
NEG_INF = float("-inf")

GLA_HEADS = 4
GLA_GATE_NORMALIZER = 16.0
ML_HEADS = 4
ML_CONV = 4
ML_QKV_BLOCK = 4
PEER_HEADS = 8
PEER_NKEYS = 128
PEER_HALF = 64
PEER_TOPK = 16

VMEM_LIMIT_BYTES = 52 * 1024 * 1024
MXU_TILE = 256
PEER_EXPERT_STEP = 2048
PEER_EXPERT_PIECE = MXU_TILE
GELU_ARG_SCALE = 0.7071067811865476
GELU_OUT_SCALE = 0.5 / GELU_ARG_SCALE


def _cparams(sem):
    return pltpu.CompilerParams(dimension_semantics=sem, vmem_limit_bytes=VMEM_LIMIT_BYTES)


def _pick_block(n, candidates):
    for c in candidates:
        if n % c == 0:
            return c
    raise ValueError(f"no block size in {candidates} divides {n}")


def _rms(x, g):
    ms = jnp.mean(x * x, axis=-1, keepdims=True)
    return x * lax.rsqrt(ms + EPS) * g


def _log_sigmoid(x):
    return jnp.minimum(x, 0.0) - jnp.log1p(jnp.exp(-jnp.abs(x)))


def _silu(x):
    return x * jax.nn.sigmoid(x)


def _dot(a, b):
    return jnp.dot(a, b, preferred_element_type=F32)


def _dot_nt(a, b):
    return lax.dot_general(a, b, (((1,), (1,)), ((), ())), preferred_element_type=F32)


def _dot_tn(a, b):
    return lax.dot_general(a, b, (((0,), (0,)), ((), ())), preferred_element_type=F32)


def _tok(tb, d):
    return pl.BlockSpec((tb, d), lambda i: (i, 0))


def _full(shape):
    nd = len(shape)
    return pl.BlockSpec(shape, lambda *_: (0,) * nd)


def _gla_pre_kernel(h_ref, g_ref, w_ref, wgl_ref, wgate_ref, bgate_ref,
                    q_ref, k_ref, v_ref, r_ref, lg_ref, *, dk, dv, hk):
    xn = _rms(h_ref[...], g_ref[...]).astype(BF16)
    proj = _dot(xn, w_ref[...])
    q_ref[...] = proj[:, :dk] * (hk ** -0.5)
    k_ref[...] = proj[:, dk:2 * dk]
    v_ref[...] = proj[:, 2 * dk:2 * dk + dv].astype(BF16)
    r_ref[...] = proj[:, 2 * dk + dv:]
    gl = _dot(xn, wgl_ref[...])
    gate = _dot(gl.astype(BF16), wgate_ref[...]) + bgate_ref[...]
    lg_ref[...] = _log_sigmoid(gate) * (1.0 / GLA_GATE_NORMALIZER)


def _gla_pre(h, g, w_qkvr, w_gl, w_gate, b_gate, dk, dv):
    n, d = h.shape
    tb = _pick_block(n, (512, 256, 128, 64))
    hk = dk // GLA_HEADS
    outs = (jax.ShapeDtypeStruct((n, dk), F32), jax.ShapeDtypeStruct((n, dk), F32),
            jax.ShapeDtypeStruct((n, dv), BF16), jax.ShapeDtypeStruct((n, dv), F32),
            jax.ShapeDtypeStruct((n, dk), F32))
    return pl.pallas_call(
        functools.partial(_gla_pre_kernel, dk=dk, dv=dv, hk=hk),
        grid=(n // tb,),
        in_specs=[_tok(tb, d), _full(g.shape), _full(w_qkvr.shape), _full(w_gl.shape),
                  _full(w_gate.shape), _full(b_gate.shape)],
        out_specs=(_tok(tb, dk), _tok(tb, dk), _tok(tb, dv), _tok(tb, dv), _tok(tb, dk)),
        out_shape=outs,
        compiler_params=_cparams(("parallel",)),
        name="gla_pre",
    )(h, g, w_qkvr, w_gl, w_gate, b_gate)


GLA_SUB = 16


def _gla_scan_kernel(*refs, nchunks, zero_init):
    if zero_init:
        q_ref, k_ref, v_ref, lg_ref, gn_ref, o_ref, sout_ref, st_ref = refs
        s0_ref = None
    else:
        q_ref, k_ref, v_ref, lg_ref, gn_ref, s0_ref, o_ref, sout_ref, st_ref = refs
    t = pl.program_id(1)
    nheads, hv, hk = st_ref.shape

    @pl.when(t == 0)
    def _():
        if zero_init:
            st_ref[...] = jnp.zeros_like(st_ref)
        else:
            for h in range(nheads):
                st_ref[h] = s0_ref[h].T

    row = lax.broadcasted_iota(jnp.int32, (CHUNK, CHUNK), 0)
    col = lax.broadcasted_iota(jnp.int32, (CHUNK, CHUNK), 1)
    tril = (row >= col).astype(F32)
    gn = gn_ref[...]
    nsub = CHUNK // GLA_SUB

    def bmm(a, b, lhs_c, rhs_c):
        return lax.dot_general(a, b, (((lhs_c,), (rhs_c,)), ((0,), (0,))), preferred_element_type=F32)

    heads = range(nheads)

    for c in range(nchunks):
        sl = pl.ds(c * CHUNK, CHUNK)
        b_all = jnp.dot(tril, lg_ref[sl, :], precision=HIGHEST, preferred_element_type=F32)
        b = jnp.stack([b_all[:, h * hk:(h + 1) * hk] for h in heads])
        q = jnp.stack([q_ref[sl, h * hk:(h + 1) * hk] for h in heads])
        k = jnp.stack([k_ref[sl, h * hk:(h + 1) * hk] for h in heads])
        v = jnp.stack([v_ref[sl, h * hv:(h + 1) * hv] for h in heads])
        st = st_ref[...]
        b_last = b[:, CHUNK - 1:CHUNK, :]
        inter = bmm((q * jnp.exp(b)).astype(BF16), st.astype(BF16), 2, 2)
        intra_parts = []
        for i in range(nsub):
            lo, hi = i * GLA_SUB, (i + 1) * GLA_SUB
            b_ref_row = b[:, lo:lo + 1, :]
            qe = (q[:, lo:hi, :] * jnp.exp(b[:, lo:hi, :] - b_ref_row)).astype(BF16)
            ke = (k[:, :hi, :] * jnp.exp(b_ref_row - b[:, :hi, :])).astype(BF16)
            att = bmm(qe, ke, 2, 2)
            r_i = lax.broadcasted_iota(jnp.int32, (GLA_SUB, hi), 0) + lo
            c_i = lax.broadcasted_iota(jnp.int32, (GLA_SUB, hi), 1)
            att = jnp.where(r_i >= c_i, att, 0.0)
            intra_parts.append(bmm(att.astype(BF16), v[:, :hi, :], 2, 1))
        o = inter + jnp.concatenate(intra_parts, axis=1)
        kd = (k * jnp.exp(b_last - b)).astype(BF16)
        st_ref[...] = st * jnp.exp(b_last) + bmm(v, kd, 1, 1)
        o = o * lax.rsqrt(jnp.mean(o * o, axis=-1, keepdims=True) + EPS) * gn
        for h in heads:
            o_ref[sl, h * hv:(h + 1) * hv] = o[h]

    @pl.when(t == pl.num_programs(1) - 1)
    def _():
        for h in range(nheads):
            sout_ref[h] = st_ref[h].T


def _gla_scan(q, k, v, lg, gn, s0, nseq, seqlen):
    n, dk = q.shape
    dv = v.shape[1]
    hk, hv = dk // GLA_HEADS, dv // GLA_HEADS
    tt = _pick_block(seqlen, (256, 128, 64))
    nt = seqlen // tt
    zero_init = s0 is None
    tmap = lambda s, t: (s * nt + t, 0)
    smap = lambda s, t: (s, 0, 0, 0)
    in_specs = [pl.BlockSpec((tt, dk), tmap), pl.BlockSpec((tt, dk), tmap),
                pl.BlockSpec((tt, dv), tmap), pl.BlockSpec((tt, dk), tmap),
                pl.BlockSpec((1, hv), lambda s, t: (0, 0))]
    args = [q, k, v, lg, gn]
    if not zero_init:
        in_specs.append(pl.BlockSpec((None, GLA_HEADS, hk, hv), smap))
        args.append(s0)
    return pl.pallas_call(
        functools.partial(_gla_scan_kernel, nchunks=tt // CHUNK, zero_init=zero_init),
        grid=(nseq, nt),
        in_specs=in_specs,
        out_specs=(pl.BlockSpec((tt, dv), tmap), pl.BlockSpec((None, GLA_HEADS, hk, hv), smap)),
        out_shape=(jax.ShapeDtypeStruct((n, dv), F32),
                   jax.ShapeDtypeStruct((nseq, GLA_HEADS, hk, hv), F32)),
        scratch_shapes=[pltpu.VMEM((GLA_HEADS, hv, hk), F32)],
        compiler_params=_cparams(("parallel", "arbitrary")),
        name="gla_scan",
    )(*args)


def _peer_pre_tail(h1, gffn_ref, wqt_ref, xt_ref, qt_ref):
    xn = _rms(h1, gffn_ref[...])
    xt = xn.T.astype(BF16)
    xt_ref[...] = xt
    qt_ref[...] = _dot(wqt_ref[...], xt).astype(BF16)


def _gla_post_kernel(o_ref, r_ref, h_ref, wout_ref, gffn_ref, wqt_ref, h1_ref, xt_ref, qt_ref):
    y = (o_ref[...] * _silu(r_ref[...])).astype(BF16)
    h1 = h_ref[...] + _dot(y, wout_ref[...])
    h1_ref[...] = h1
    _peer_pre_tail(h1, gffn_ref, wqt_ref, xt_ref, qt_ref)


def _gla_post(o, r, h, w_out, g_ffn, w_qt):
    n, d = h.shape
    tb = _pick_block(n, (512, 256, 128))
    dq = w_qt.shape[0]
    return pl.pallas_call(
        _gla_post_kernel,
        grid=(n // tb,),
        in_specs=[_tok(tb, o.shape[1]), _tok(tb, r.shape[1]), _tok(tb, d), _full(w_out.shape),
                  _full(g_ffn.shape), _full(w_qt.shape)],
        out_specs=(_tok(tb, d), pl.BlockSpec((d, tb), lambda i: (0, i)),
                   pl.BlockSpec((dq, tb), lambda i: (0, i))),
        out_shape=(jax.ShapeDtypeStruct((n, d), F32), jax.ShapeDtypeStruct((d, n), BF16),
                   jax.ShapeDtypeStruct((dq, n), BF16)),
        compiler_params=_cparams(("parallel",)),
        name="gla_post",
    )(o, r, h, w_out, g_ffn, w_qt)


def _ml_pre_kernel(h_ref, g_ref, w_ref, xm_ref, z_ref, *, di):
    xn = _rms(h_ref[...], g_ref[...]).astype(BF16)
    up = _dot(xn, w_ref[...])
    xm_ref[...] = up[:, :di]
    z_ref[...] = up[:, di:]


def _ml_pre(h, g, w_up):
    n, d = h.shape
    di = w_up.shape[1] // 2
    tb = _pick_block(n, (512, 256, 128, 64))
    return pl.pallas_call(
        functools.partial(_ml_pre_kernel, di=di),
        grid=(n // tb,),
        in_specs=[_tok(tb, d), _full(g.shape), _full(w_up.shape)],
        out_specs=(_tok(tb, di), _tok(tb, di)),
        out_shape=(jax.ShapeDtypeStruct((n, di), F32), jax.ShapeDtypeStruct((n, di), F32)),
        compiler_params=_cparams(("parallel",)),
        name="ml_pre",
    )(h, g, w_up)


HALO = 8


def _ml_qkv_kernel(xm_ref, prev_ref, buf_ref, cw_ref, cb_ref, wq_ref, wk_ref, wv_ref, wif_ref, bif_ref,
                   q_ref, k_ref, v_ref, xc_ref, gates_ref, xp_scr, *, tt, di, hd):
    t = pl.program_id(1)
    x = xm_ref[...]
    halo = jnp.where(t == 0, buf_ref[...], prev_ref[...])
    xp_scr[0:HALO, :] = halo
    xp_scr[HALO:HALO + tt, :] = x
    y = cb_ref[...]
    for j in range(ML_CONV):
        y = y + cw_ref[j:j + 1, :] * xp_scr[pl.ds(HALO - (ML_CONV - 1) + j, tt), :]
    xc = _silu(y)
    xc_ref[...] = xc
    xcb = xc.astype(BF16)
    xmb = x.astype(BF16)
    gates = jnp.zeros((tt, gates_ref.shape[1]), F32) + bif_ref[...]
    ntile = di // MXU_TILE
    for j in range(ntile):
        cs = slice(j * MXU_TILE, (j + 1) * MXU_TILE)
        qj = _dot(xcb[:, cs], wq_ref[j])
        kj = _dot(xcb[:, cs], wk_ref[j])
        vj = _dot(xmb[:, cs], wv_ref[j])
        q_ref[:, cs] = qj.astype(BF16)
        k_ref[:, cs] = kj * (hd ** -0.5)
        v_ref[:, cs] = vj.astype(BF16)
        gates = gates + _dot(qj.astype(BF16), wif_ref[j * MXU_TILE:(j + 1) * MXU_TILE, :])
        gates = gates + _dot(kj.astype(BF16), wif_ref[di + j * MXU_TILE:di + (j + 1) * MXU_TILE, :])
        gates = gates + _dot(vj.astype(BF16), wif_ref[2 * di + j * MXU_TILE:2 * di + (j + 1) * MXU_TILE, :])
    lane = lax.broadcasted_iota(jnp.int32, gates.shape, 1)
    is_f = (lane >= ML_HEADS) & (lane < 2 * ML_HEADS)
    gates_ref[...] = jnp.where(is_f, _log_sigmoid(gates), gates)


def _ml_qkv(xm, buf, conv_w, conv_b, wq, wk, wv, wif, bif, nseq, seqlen):
    n, di = xm.shape
    hd = di // ML_HEADS
    tt = _pick_block(seqlen, (256, 128, 64))
    nt = seqlen // tt
    tmap = lambda s, t: (s * nt + t, 0)
    prev_map = lambda s, t: (jnp.maximum((s * nt + t) * (tt // HALO) - 1, 0), 0)
    ng = wif.shape[1]
    return pl.pallas_call(
        functools.partial(_ml_qkv_kernel, tt=tt, di=di, hd=hd),
        grid=(nseq, nt),
        in_specs=[pl.BlockSpec((tt, di), tmap), pl.BlockSpec((HALO, di), prev_map),
                  pl.BlockSpec((None, HALO, di), lambda s, t: (s, 0, 0)),
                  _full(conv_w.shape), _full(conv_b.shape), _full(wq.shape), _full(wk.shape),
                  _full(wv.shape), _full(wif.shape), _full(bif.shape)],
        out_specs=(pl.BlockSpec((tt, di), tmap), pl.BlockSpec((tt, di), tmap),
                   pl.BlockSpec((tt, di), tmap), pl.BlockSpec((tt, di), tmap),
                   pl.BlockSpec((tt, ng), tmap)),
        out_shape=(jax.ShapeDtypeStruct((n, di), BF16), jax.ShapeDtypeStruct((n, di), F32),
                   jax.ShapeDtypeStruct((n, di), BF16), jax.ShapeDtypeStruct((n, di), F32),
                   jax.ShapeDtypeStruct((n, ng), F32)),
        scratch_shapes=[pltpu.VMEM((HALO + tt, di), F32)],
        compiler_params=_cparams(("parallel", "arbitrary")),
        name="ml_qkv",
    )(xm, xm, buf, conv_w, conv_b, wq, wk, wv, wif, bif)


def _ml_scan_kernel(*refs, nchunks, blk, zero_init):
    if zero_init:
        q_ref, k_ref, v_ref, g_ref, h_ref, cout_ref, nout_ref, mout_ref, c_scr, n_scr, m_scr = refs
    else:
        (q_ref, k_ref, v_ref, g_ref, c0_ref, n0_ref, m0_ref,
         h_ref, cout_ref, nout_ref, mout_ref, c_scr, n_scr, m_scr) = refs
    t = pl.program_id(1)
    nheads, hd = c_scr.shape[0], c_scr.shape[1]

    @pl.when(t == 0)
    def _():
        if zero_init:
            c_scr[...] = jnp.zeros_like(c_scr)
            n_scr[...] = jnp.zeros_like(n_scr)
            m_scr[...] = jnp.zeros_like(m_scr)
        else:
            c_scr[...] = c0_ref[...]
            n_scr[...] = n0_ref[...]
            m_scr[...] = m0_ref[...]

    row = lax.broadcasted_iota(jnp.int32, (blk, blk), 0)
    col = lax.broadcasted_iota(jnp.int32, (blk, blk), 1)
    causal = row >= col
    tril = causal.astype(F32)
    lanes = g_ref.shape[1]

    def bmm(a, b, lhs_c, rhs_c, precision=None):
        return lax.dot_general(a, b, (((lhs_c,), (rhs_c,)), ((0,), (0,))), precision=precision,
                               preferred_element_type=F32)

    shared = {}
    for c, h0 in [(c, h0) for c in range(nchunks) for h0 in range(0, nheads, ML_HEAD_BATCH)]:
        heads = range(h0, h0 + ML_HEAD_BATCH)
        hsl = slice(h0, h0 + ML_HEAD_BATCH)
        sl = pl.ds(c * blk, blk)
        q = jnp.stack([q_ref[sl, h * hd:(h + 1) * hd] for h in heads])
        kf = jnp.stack([k_ref[sl, h * hd:(h + 1) * hd] for h in heads])
        v = jnp.stack([v_ref[sl, h * hd:(h + 1) * hd] for h in heads])
        cmat = c_scr[hsl]
        nvec = n_scr[hsl]
        m = m_scr[hsl][:, :, 0:1]
        if c not in shared:
            g = g_ref[sl, :]
            gcum = jnp.dot(tril, g, precision=HIGHEST, preferred_element_type=F32)
            diff_t = (g - pltpu.roll(gcum, lanes - nheads, axis=1)).T if blk % lanes == 0 else None
            shared[c] = (g, gcum, diff_t)
        g, gcum, diff_t = shared[c]
        i_col = jnp.stack([g[:, h:h + 1] for h in heads])
        f_col = jnp.stack([gcum[:, h + nheads:h + nheads + 1] for h in heads])
        if diff_t is not None:
            w_row = jnp.stack([diff_t[h:h + 1, :] for h in heads])
        else:
            eye = row == col
            w_row = bmm(jnp.ones((ML_HEAD_BATCH, blk, blk), F32), jnp.where(eye, i_col - f_col, 0.0), 2, 1,
                        precision=HIGHEST)
        log_d = jnp.where(causal, f_col + w_row, NEG_INF)
        inter = f_col + m
        mt = jnp.maximum(inter, jnp.max(log_d, axis=-1, keepdims=True))
        sc = bmm(q, kf.astype(BF16), 2, 2) * jnp.exp(log_d - mt)
        a = jnp.exp(inter - mt)
        num = a * bmm(q, cmat.astype(BF16), 2, 1) + bmm(sc.astype(BF16), v, 2, 1)
        qn = jnp.sum(q.astype(F32) * nvec, axis=-1, keepdims=True)
        den = a * qn + jnp.sum(sc, axis=-1, keepdims=True)
        hh = num / jnp.maximum(jnp.abs(den), jnp.exp(-mt))
        m_new = mt[:, blk - 1:blk, :]
        f_last = f_col[:, blk - 1:blk, :]
        w_end = jnp.exp(f_last - f_col + i_col - m_new)
        a_end = jnp.exp(f_last + m - m_new)
        kw = kf * w_end
        c_scr[hsl] = a_end * cmat + bmm(kw.astype(BF16), v, 1, 1)
        n_scr[hsl] = a_end * nvec + jnp.sum(kw, axis=1, keepdims=True)
        m_scr[hsl] = jnp.broadcast_to(m_new, (ML_HEAD_BATCH, 1, M_LANES))
        mu = jnp.mean(hh, axis=-1, keepdims=True)
        var = jnp.mean(jnp.square(hh - mu), axis=-1, keepdims=True)
        hn = (hh - mu) * lax.rsqrt(var + EPS)
        for i, h in enumerate(heads):
            h_ref[sl, h * hd:(h + 1) * hd] = hn[i]

    @pl.when(t == pl.num_programs(1) - 1)
    def _():
        cout_ref[...] = c_scr[...]
        nout_ref[...] = n_scr[...]
        mout_ref[...] = m_scr[...]


M_LANES = 128
ML_SCAN_BLOCK = 256
ML_HEAD_BATCH = 2


def _ml_scan(q, k, v, gates, state, nseq, seqlen):
    n, di = q.shape
    hd = di // ML_HEADS
    tt = _pick_block(seqlen, (256, 128, 64))
    blk = min(tt, ML_SCAN_BLOCK)
    nt = seqlen // tt
    zero_init = state is None
    tmap = lambda s, t: (s * nt + t, 0)
    smap = lambda s, t: (s, 0, 0, 0)
    ng = gates.shape[1]
    nh = ML_HEADS
    in_specs = [pl.BlockSpec((tt, di), tmap), pl.BlockSpec((tt, di), tmap),
                pl.BlockSpec((tt, di), tmap), pl.BlockSpec((tt, ng), tmap)]
    args = [q, k, v, gates]
    state_specs = [pl.BlockSpec((None, nh, hd, hd), smap), pl.BlockSpec((None, nh, 1, hd), smap),
                   pl.BlockSpec((None, nh, 1, M_LANES), smap)]
    if not zero_init:
        in_specs += state_specs
        args += list(state)
    return pl.pallas_call(
        functools.partial(_ml_scan_kernel, nchunks=tt // blk, blk=blk, zero_init=zero_init),
        grid=(nseq, nt),
        in_specs=in_specs,
        out_specs=(pl.BlockSpec((tt, di), tmap), *state_specs),
        out_shape=(jax.ShapeDtypeStruct((n, di), F32),
                   jax.ShapeDtypeStruct((nseq, nh, hd, hd), F32),
                   jax.ShapeDtypeStruct((nseq, nh, 1, hd), F32),
                   jax.ShapeDtypeStruct((nseq, nh, 1, M_LANES), F32)),
        scratch_shapes=[pltpu.VMEM((nh, hd, hd), F32), pltpu.VMEM((nh, 1, hd), F32),
                        pltpu.VMEM((nh, 1, M_LANES), F32)],
        compiler_params=_cparams(("parallel", "arbitrary")),
        name="ml_scan",
    )(*args)


def _ml_post_kernel(hn_ref, xc_ref, z_ref, h_ref, gn_ref, skip_ref, wdown_ref, gffn_ref, wqt_ref,
                    h1_ref, xt_ref, qt_ref):
    y = (hn_ref[...] * gn_ref[...] + skip_ref[...] * xc_ref[...]) * _silu(z_ref[...])
    h1 = h_ref[...] + _dot(y.astype(BF16), wdown_ref[...])
    h1_ref[...] = h1
    _peer_pre_tail(h1, gffn_ref, wqt_ref, xt_ref, qt_ref)


def _ml_post(hn, xc, z, h, g_norm, skip, w_down, g_ffn, w_qt):
    n, d = h.shape
    di = hn.shape[1]
    tb = _pick_block(n, (512, 256, 128))
    dq = w_qt.shape[0]
    return pl.pallas_call(
        _ml_post_kernel,
        grid=(n // tb,),
        in_specs=[_tok(tb, di), _tok(tb, di), _tok(tb, di), _tok(tb, d), _full(g_norm.shape),
                  _full(skip.shape), _full(w_down.shape), _full(g_ffn.shape), _full(w_qt.shape)],
        out_specs=(_tok(tb, d), pl.BlockSpec((d, tb), lambda i: (0, i)),
                   pl.BlockSpec((dq, tb), lambda i: (0, i))),
        out_shape=(jax.ShapeDtypeStruct((n, d), F32), jax.ShapeDtypeStruct((d, n), BF16),
                   jax.ShapeDtypeStruct((dq, n), BF16)),
        compiler_params=_cparams(("parallel",)),
        name="ml_post",
    )(hn, xc, z, h, g_norm, skip, w_down, g_ffn, w_qt)


NCAND_ROWS = 72


def _first_max(vals, ids, big):
    m = jnp.max(vals, axis=0, keepdims=True)
    sel = jnp.min(jnp.where(vals == m, ids, big), axis=0, keepdims=True)
    return m, sel


def _sorting_network(n):
    size = 1
    while size < n:
        size *= 2
    pairs = []
    p = 1
    while p < size:
        k = p
        while k >= 1:
            for j in range(k % p, size - k, 2 * k):
                for i in range(min(k, size - j - k)):
                    if (i + j) // (2 * p) == (i + j + k) // (2 * p):
                        pairs.append((i + j, i + j + k))
            k //= 2
        p *= 2
    return [(i, j) for i, j in pairs if j < n]


SUBLANES = 8


def _sorted_best16(s):
    k16 = PEER_TOPK
    rows, lanes = s.shape
    nslab = rows // SUBLANES
    v = [s[SUBLANES * j:SUBLANES * (j + 1), :] for j in range(nslab)]
    for i, j in _sorting_network(nslab):
        v[i], v[j] = jnp.maximum(v[i], v[j]), jnp.minimum(v[i], v[j])
    v = v[:k16] + [jnp.full((SUBLANES, lanes), NEG_INF, F32)] * (k16 - nslab)
    for shift in (1, 2, 4):
        m = [jnp.maximum(v[i], pltpu.roll(v[k16 - 1 - i], shift, axis=0)) for i in range(k16)]
        d = k16 // 2
        while d >= 1:
            for i in range(k16):
                if (i // d) % 2 == 0:
                    m[i], m[i + d] = jnp.maximum(m[i], m[i + d]), jnp.minimum(m[i], m[i + d])
            d //= 2
        v = m
    return v


def _tie_flag(tops, member):
    tie = jnp.zeros_like(tops[0])
    for r in range(len(tops) - 1):
        tie = jnp.maximum(tie, (tops[r] == tops[r + 1]).astype(F32))
    count = jnp.sum(member.astype(F32), axis=0, keepdims=True)
    return jnp.maximum(jnp.max(tie, axis=0, keepdims=True), (count != float(PEER_TOPK)).astype(F32))


def _peer_route_kernel(qt_ref, keys_ref, r2_ref, e2_ref, n_ref, e1_ref, s_scr, rank_scr, top_scr, *, tk):
    nk, k16 = PEER_NKEYS, PEER_TOPK
    key_id = lax.broadcasted_iota(jnp.int32, (nk, tk), 0)
    slot_id = lax.broadcasted_iota(jnp.int32, (k16, tk), 0)

    def half_body(hc, carry):
        start = pl.multiple_of(hc * PEER_HALF, PEER_HALF)
        s = _dot(keys_ref[hc], qt_ref[pl.ds(start, PEER_HALF), :])
        s_scr[hc] = s

        tops = _sorted_best16(s)
        s3 = s.reshape(nk // SUBLANES, SUBLANES, tk)
        member = s3 >= tops[k16 - 1][None]
        tie = _tie_flag(tops, member.reshape(nk, tk))

        def by_value():
            rank = jnp.where(member, float(k16 - 1), float(k16))
            for r in range(k16 - 2, -1, -1):
                rank = jnp.where(s3 >= tops[r][None], float(r), rank)
            sub = lax.broadcasted_iota(jnp.int32, (SUBLANES, tk), 0)
            halves = []
            for base in range(0, k16, SUBLANES):
                slab = tops[base]
                for r in range(1, SUBLANES):
                    slab = jnp.where(sub == r, tops[base + r], slab)
                halves.append(slab)
            return rank.reshape(nk, tk), jnp.concatenate(halves, axis=0)

        def by_value_and_index():
            def pick(r, st):
                s_cur, rank, top = st
                m, sel = _first_max(s_cur, key_id, nk)
                hit = key_id == sel
                return (jnp.where(hit, NEG_INF, s_cur), jnp.where(hit, jnp.asarray(r, F32), rank),
                        jnp.where(slot_id == r, m, top))

            _, rank, top = lax.fori_loop(
                0, k16, pick, (s, jnp.full((nk, tk), float(k16), F32), jnp.zeros((k16, tk), F32)))
            return rank, top

        rank, top = lax.cond(jnp.max(tie) > 0.0, by_value_and_index, by_value)
        rank_scr[hc] = rank
        top_scr[hc] = top
        return carry

    lax.fori_loop(0, 2 * PEER_HEADS, half_body, 0)

    def cand_ids():
        i16 = lax.broadcasted_iota(jnp.int32, (16, tk), 0)
        i8 = lax.broadcasted_iota(jnp.int32, (8, tk), 0)
        parts = [i16, 16 + i8, 32 + i8, 48 + i8, 16 * i16, 16 * i8 + 1, 16 * i8 + 2]
        return jnp.concatenate(parts, axis=0)

    cand_id = cand_ids()
    row72 = lax.broadcasted_iota(jnp.int32, (NCAND_ROWS, tk), 0)
    cand_ok = (row72 < 40) | (cand_id >= 64)
    big_id = 16 * 16

    def head_body(h, carry):
        a = top_scr[2 * h]
        b = top_scr[2 * h + 1]
        parts = [a[0:1, :] + b, a[1:2, :] + b[0:8, :], a[2:3, :] + b[0:8, :], a[3:4, :] + b[0:8, :],
                 a + b[0:1, :], a[0:8, :] + b[1:2, :], a[0:8, :] + b[2:3, :]]
        cand = jnp.where(cand_ok, jnp.concatenate(parts, axis=0), NEG_INF)
        tops = _sorted_best16(cand)
        cand3 = cand.reshape(NCAND_ROWS // SUBLANES, SUBLANES, tk)
        sel = (cand3 >= tops[k16 - 1][None]).reshape(NCAND_ROWS, tk)
        tie = _tie_flag(tops, sel)

        def by_value():
            self = sel.astype(F32)
            cnt = self[40:56, :] + jnp.concatenate(
                [self[56:64, :] + self[64:72, :], jnp.zeros((8, tk), F32)], axis=0)
            for r1, (lo, hi) in enumerate(((0, 16), (16, 24), (24, 32), (32, 40))):
                row = jnp.sum(self[lo:hi, :], axis=0, keepdims=True)
                cnt = cnt + jnp.where(slot_id == r1, row, 0.0)
            ex = jnp.exp(cand3 - tops[0][None]).reshape(NCAND_ROWS, tk)
            z = jnp.sum(jnp.where(sel, ex, 0.0), axis=0, keepdims=True)
            return cnt, z

        def by_value_and_index():
            def pick(r, st):
                cand_cur, cnt, tsel = st
                m, sel = _first_max(cand_cur, cand_id, big_id)
                cand_cur = jnp.where(cand_id == sel, NEG_INF, cand_cur)
                cnt = cnt + (slot_id == (sel >> 4)).astype(F32)
                return cand_cur, cnt, jnp.where(slot_id == r, m, tsel)

            _, cnt, tsel = lax.fori_loop(
                0, k16, pick, (cand, jnp.zeros((k16, tk), F32), jnp.zeros((k16, tk), F32)))
            return cnt, jnp.sum(jnp.exp(tsel - tsel[0:1, :]), axis=0, keepdims=True)

        cnt, z = lax.cond(jnp.max(tie) > 0.0, by_value_and_index, by_value)
        rank1 = rank_scr[2 * h]
        rank2 = rank_scr[2 * h + 1]
        pk = 16
        rank1b = rank1.astype(BF16).reshape(nk // pk, pk, tk)
        n_sel = jnp.zeros((nk // pk, pk, tk), BF16)
        for r in range(k16):
            row = jnp.broadcast_to(cnt[r:r + 1, :], (pk, tk)).astype(BF16)
            n_sel = jnp.where(rank1b == r, row[None], n_sel)
        n_ref[h] = n_sel.reshape(nk, tk).astype(F32)
        e1_ref[h] = jnp.where(rank1 < k16, jnp.exp(s_scr[2 * h] - a[0:1, :]) * (GELU_OUT_SCALE / z), 0.0)
        r2_ref[h] = rank2.astype(BF16)
        e2_ref[h] = jnp.where(rank2 < k16, jnp.exp(s_scr[2 * h + 1] - b[0:1, :]), 0.0).astype(BF16)
        return carry

    lax.fori_loop(0, PEER_HEADS, head_body, 0)


def _peer_route(qt, keys):
    dq, n = qt.shape
    tk = _pick_block(n, (512, 256, 128))
    nk = PEER_NKEYS
    shp = jax.ShapeDtypeStruct((PEER_HEADS, nk, n), F32)
    shp16 = jax.ShapeDtypeStruct((PEER_HEADS, nk, n), BF16)
    ospec = pl.BlockSpec((PEER_HEADS, nk, tk), lambda i: (0, 0, i))
    return pl.pallas_call(
        functools.partial(_peer_route_kernel, tk=tk),
        grid=(n // tk,),
        in_specs=[pl.BlockSpec((dq, tk), lambda i: (0, i)), _full(keys.shape)],
        out_specs=(ospec, ospec, ospec, ospec),
        out_shape=(shp16, shp16, shp, shp),
        scratch_shapes=[pltpu.VMEM((2 * PEER_HEADS, nk, tk), F32), pltpu.VMEM((2 * PEER_HEADS, nk, tk), F32),
                        pltpu.VMEM((2 * PEER_HEADS, PEER_TOPK, tk), F32)],
        compiler_params=_cparams(("parallel",)),
        name="peer_route",
    )(qt, keys)


def _peer_gate_stage(row0, act_ref, coef_ref, r2_ref, e2_ref, n_ref, e1_ref):
    nk = PEER_NKEYS
    t = act_ref.shape[1]
    pk = 16
    zero = jnp.zeros((), BF16)
    for al in range(PEER_EXPERT_PIECE // nk):
        a = row0 + al
        rows = slice(al * nk, (al + 1) * nk)
        y = act_ref[rows, :]
        ge = (y * (1.0 + lax.erf(y))).astype(BF16).reshape(nk // pk, pk, t)
        g = None
        for h in range(PEER_HEADS):
            n16 = jnp.broadcast_to(n_ref[h, a:a + 1, :], (pk, t)).astype(BF16)
            e16 = jnp.broadcast_to(e1_ref[h, a:a + 1, :], (pk, t)).astype(BF16)
            r2h = r2_ref[h].reshape(nk // pk, pk, t)
            e2h = e2_ref[h].reshape(nk // pk, pk, t)
            term = jnp.where(r2h < n16[None], e2h, zero) * e16[None]
            g = term if g is None else g + term
        coef_ref[rows, :] = (g * ge).reshape(nk, t)


def _peer_expert_kernel(xt_ref, r2_ref, e2_ref, n_ref, e1_ref, u_ref, vt_ref, out_ref, act_scr, coef_scr):
    k = pl.program_id(1)
    ep = PEER_EXPERT_PIECE
    npiece = act_scr.shape[0]
    rows = ep // PEER_NKEYS

    @pl.when(k == 0)
    def _():
        out_ref[...] = jnp.zeros_like(out_ref)

    xt = xt_ref[...]
    for p in range(npiece):
        act_scr[p] = _dot(u_ref[p * ep:(p + 1) * ep, :], xt)
    for p in range(npiece):
        _peer_gate_stage(p * rows, act_scr.at[p], coef_scr.at[p], r2_ref, e2_ref, n_ref, e1_ref)
    for p in range(npiece):
        out_ref[...] += _dot(vt_ref[:, p * ep:(p + 1) * ep], coef_scr[p])


def _peer_expert(xt, r2, e2, nsel, e1, u, vt):
    d, n = xt.shape
    ne = u.shape[0]
    t = _pick_block(n, (512, 256, 128))
    et = PEER_EXPERT_STEP
    npiece = et // PEER_EXPERT_PIECE
    rspec = pl.BlockSpec((PEER_HEADS, PEER_NKEYS, t), lambda i, k: (0, 0, i))
    row_spec = pl.BlockSpec((PEER_HEADS, et // PEER_NKEYS, t), lambda i, k: (0, k, i))
    return pl.pallas_call(
        _peer_expert_kernel,
        grid=(n // t, ne // et),
        in_specs=[pl.BlockSpec((d, t), lambda i, k: (0, i)), rspec, rspec, row_spec, row_spec,
                  pl.BlockSpec((et, d), lambda i, k: (k, 0)),
                  pl.BlockSpec((d, et), lambda i, k: (0, k))],
        out_specs=pl.BlockSpec((d, t), lambda i, k: (0, i)),
        out_shape=jax.ShapeDtypeStruct((d, n), F32),
        scratch_shapes=[pltpu.VMEM((npiece, PEER_EXPERT_PIECE, t), F32),
                        pltpu.VMEM((npiece, PEER_EXPERT_PIECE, t), BF16)],
        compiler_params=_cparams(("parallel", "arbitrary")),
        name="peer_expert",
    )(xt, r2, e2, nsel, e1, u, vt)


def _ple_kernel(ot_ref, h1_ref, p_ref, gple_ref, wgate_ref, wproj_ref, gnext_ref, h_ref, y_ref):
    h2 = h1_ref[...] + ot_ref[...].T
    gate = jax.nn.sigmoid(_dot(_rms(h2, gple_ref[...]).astype(BF16), wgate_ref[...]))
    h3 = h2 + gate * _dot(p_ref[...].astype(BF16), wproj_ref[...])
    h_ref[...] = h3
    y_ref[...] = _rms(h3, gnext_ref[...])


def _ple(out_t, h1, p, g_ple, w_gate, w_proj, g_next):
    n, d = h1.shape
    tb = _pick_block(n, (512, 256, 128))
    return pl.pallas_call(
        _ple_kernel,
        grid=(n // tb,),
        in_specs=[pl.BlockSpec((d, tb), lambda i: (0, i)), _tok(tb, d), _tok(tb, p.shape[1]),
                  _full(g_ple.shape), _full(w_gate.shape), _full(w_proj.shape), _full(g_next.shape)],
        out_specs=(_tok(tb, d), _tok(tb, d)),
        out_shape=(jax.ShapeDtypeStruct((n, d), F32), jax.ShapeDtypeStruct((n, d), F32)),
        compiler_params=_cparams(("parallel",)),
        name="ple",
    )(out_t, h1, p, g_ple, w_gate, w_proj, g_next)


def _block_diag_tiles(w):
    g, bi, bo = w.shape
    per = MXU_TILE // bi
    wt = w.reshape(g // per, per, bi, bo)
    eye = jnp.eye(per, dtype=w.dtype)
    dense = jnp.einsum("tgio,gh->tgiho", wt, eye)
    return dense.reshape(g // per, per * bi, per * bo)


def _row(x):
    return x.reshape(1, -1)


def _prep_weights(W, depth):
    P = {}
    n_gla = W["w_gla_in"].shape[0]
    dk = W["w_gla_gate"].shape[2]
    rank = W["w_gla_gate"].shape[1]
    dv = (W["w_gla_in"].shape[2] - rank - 2 * dk) // 2
    P["gla_dims"] = (dk, dv)
    pad = 128 - rank
    P["w_gla_qkvr"] = W["w_gla_in"][:, :, :2 * dk + 2 * dv].astype(BF16)
    P["w_gla_gl"] = jnp.pad(W["w_gla_in"][:, :, 2 * dk + 2 * dv:], ((0, 0), (0, 0), (0, pad))).astype(BF16)
    P["w_gla_gate"] = jnp.pad(W["w_gla_gate"], ((0, 0), (0, pad), (0, 0))).astype(BF16)
    P["w_gla_out"] = W["w_gla_out"].astype(BF16)
    P["w_ml_up"] = W["w_ml_up"].astype(BF16)
    n_ml = W["w_ml_up"].shape[0]
    for nm in ("w_ml_q", "w_ml_k", "w_ml_v"):
        P[nm] = jnp.stack([_block_diag_tiles(W[nm][j]) for j in range(n_ml)]).astype(BF16)
    wif = jnp.concatenate([W["w_ml_igate"], W["w_ml_fgate"]], axis=-1)
    P["w_ml_if"] = jnp.pad(wif, ((0, 0), (0, 0), (0, 128 - wif.shape[-1]))).astype(BF16)
    bif = jnp.concatenate([W["b_ml_igate"], W["b_ml_fgate"]], axis=-1)
    P["b_ml_if"] = jnp.pad(bif, ((0, 0), (0, 128 - bif.shape[-1])))
    P["w_ml_down"] = W["w_ml_down"].astype(BF16)
    P["w_peer_qt"] = jnp.swapaxes(W["w_peer_query"], 1, 2).astype(BF16)
    keys = W["peer_keys"]
    P["peer_keys"] = jnp.swapaxes(keys, 1, 2).reshape(depth, 2 * PEER_HEADS, PEER_NKEYS, PEER_HALF).astype(BF16)
    P["peer_u"] = (W["peer_u"] * GELU_ARG_SCALE).astype(BF16)
    P["peer_vt"] = jnp.swapaxes(W["peer_v"], 1, 2).astype(BF16)
    P["w_ple_gate"] = W["w_ple_gate"].astype(BF16)
    P["w_ple_proj"] = W["w_ple_proj"].astype(BF16)
    del n_gla
    return P


def _run_group(x, p, s_gla, c_ml, n_ml, m_ml, buf_ml, W, P, depth):
    bsz, seqlen, d = x.shape
    n = bsz * seqlen
    h = x.reshape(n, d)
    dk, dv = P["gla_dims"]
    new_s, new_c, new_n, new_m, new_buf = [], [], [], [], []
    y = None
    for i in range(depth):
        j = i // 2
        if i % 2 == 0:
            q, k, v, r, lg = _gla_pre(h, _row(W["norm_mix"][i]), P["w_gla_qkvr"][j], P["w_gla_gl"][j],
                                      P["w_gla_gate"][j], _row(W["b_gla_gate"][j]), dk, dv)
            s0 = None if s_gla is None else s_gla[j]
            o, s_fin = _gla_scan(q, k, v, lg, _row(W["g_gla_norm"][j]), s0, bsz, seqlen)
            new_s.append(s_fin)
            h1, xt, qt = _gla_post(o, r, h, P["w_gla_out"][j], _row(W["norm_ffn"][i]), P["w_peer_qt"][i])
        else:
            xm, z = _ml_pre(h, _row(W["norm_mix"][i]), P["w_ml_up"][j])
            di = xm.shape[1]
            if buf_ml is None:
                buf = jnp.zeros((bsz, HALO, di), F32)
            else:
                buf = jnp.pad(buf_ml[j], ((0, 0), (HALO - (ML_CONV - 1), 0), (0, 0)))
            qm, km, vm, xc, gates = _ml_qkv(xm, buf, W["ml_conv_w"][j], _row(W["ml_conv_b"][j]),
                                            P["w_ml_q"][j], P["w_ml_k"][j], P["w_ml_v"][j],
                                            P["w_ml_if"][j], _row(P["b_ml_if"][j]), bsz, seqlen)
            if c_ml is None:
                state = None
            else:
                state = (c_ml[j], n_ml[j][:, :, None, :],
                         jnp.broadcast_to(m_ml[j][:, :, None, None], m_ml[j].shape + (1, M_LANES)))
            hn, c_fin, n_fin, m_fin = _ml_scan(qm, km, vm, gates, state, bsz, seqlen)
            new_c.append(c_fin)
            new_n.append(n_fin[:, :, 0, :])
            new_m.append(m_fin[:, :, 0, 0])
            xm3 = xm.reshape(bsz, seqlen, di)
            if seqlen >= ML_CONV - 1:
                new_buf.append(xm3[:, seqlen - (ML_CONV - 1):, :])
            else:
                new_buf.append(jnp.concatenate([buf[:, HALO - (ML_CONV - 1):, :], xm3], axis=1)[:, -(ML_CONV - 1):, :])
            h1, xt, qt = _ml_post(hn, xc, z, h, _row(W["g_ml_norm"][j]), _row(W["ml_skip"][j]),
                                  P["w_ml_down"][j], _row(W["norm_ffn"][i]), P["w_peer_qt"][i])
        r2, e2, nsel, e1 = _peer_route(qt, P["peer_keys"][i])
        out_t = _peer_expert(xt, r2, e2, nsel, e1, P["peer_u"][i], P["peer_vt"][i])
        g_next = W["norm_final"] if i == depth - 1 else W["norm_mix"][i + 1]
        h, y = _ple(out_t, h1, p[i].reshape(n, -1), _row(W["norm_ple"][i]), P["w_ple_gate"][i],
                    P["w_ple_proj"][i], _row(g_next))
    return (y.reshape(bsz, seqlen, d), jnp.stack(new_s), jnp.stack(new_c), jnp.stack(new_n),
            jnp.stack(new_m), jnp.stack(new_buf))


def kernel(x_prompt, x_sample, state_gla_S, state_mlstm_C, state_mlstm_n, state_mlstm_m, state_mlstm_conv,
           p_prompt, p_sample, w_gla_in, w_gla_gate, b_gla_gate, g_gla_norm, w_gla_out,
           w_ml_up, ml_conv_w, ml_conv_b, w_ml_q, w_ml_k, w_ml_v, w_ml_igate, b_ml_igate,
           w_ml_fgate, b_ml_fgate, g_ml_norm, ml_skip, w_ml_down,
           w_peer_query, peer_keys, peer_u, peer_v, norm_mix, norm_ffn, norm_ple,
           w_ple_gate, w_ple_proj, norm_final):
    W = dict(w_gla_in=w_gla_in, w_gla_gate=w_gla_gate, b_gla_gate=b_gla_gate, g_gla_norm=g_gla_norm,
             w_gla_out=w_gla_out, w_ml_up=w_ml_up, ml_conv_w=ml_conv_w, ml_conv_b=ml_conv_b,
             w_ml_q=w_ml_q, w_ml_k=w_ml_k, w_ml_v=w_ml_v, w_ml_igate=w_ml_igate, b_ml_igate=b_ml_igate,
             w_ml_fgate=w_ml_fgate, b_ml_fgate=b_ml_fgate, g_ml_norm=g_ml_norm, ml_skip=ml_skip,
             w_ml_down=w_ml_down, w_peer_query=w_peer_query, peer_keys=peer_keys, peer_u=peer_u,
             peer_v=peer_v, norm_mix=norm_mix, norm_ffn=norm_ffn, norm_ple=norm_ple,
             w_ple_gate=w_ple_gate, w_ple_proj=w_ple_proj, norm_final=norm_final)
    depth = norm_mix.shape[0]
    P = _prep_weights(W, depth)
    y_p, s_p, c_p, n_p, m_p, buf_p = _run_group(x_prompt, p_prompt, None, None, None, None, None, W, P, depth)
    y_s, s_s, c_s, n_s, m_s, buf_s = _run_group(x_sample, p_sample, state_gla_S, state_mlstm_C,
                                                state_mlstm_n, state_mlstm_m, state_mlstm_conv, W, P, depth)
    return (y_p, y_s, s_p, s_s, c_p, c_s, n_p, n_s, m_p, m_s, buf_p, buf_s)
```

```python
import functools

import jax
import jax.numpy as jnp
from jax import lax
from jax.experimental import pallas as pl
from jax.experimental.pallas import tpu as pltpu

F32 = jnp.float32
BF16 = jnp.bfloat16
EPS = 1e-6
CHUNK = 64
HIGHEST = lax.Precision.HIGHEST
NEG_INF = float("-inf")

GLA_HEADS = 4
GLA_GATE_NORMALIZER = 16.0
ML_HEADS = 4
ML_CONV = 4
ML_QKV_BLOCK = 4
PEER_HEADS = 8
PEER_NKEYS = 128
PEER_HALF = 64
PEER_TOPK = 16

VMEM_LIMIT_BYTES = 52 * 1024 * 1024
MXU_TILE = 256
PEER_EXPERT_STEP = 2048
PEER_EXPERT_PIECE = MXU_TILE
GELU_ARG_SCALE = 0.7071067811865476
GELU_OUT_SCALE = 0.5 / GELU_ARG_SCALE


def _cparams(sem):
    return pltpu.CompilerParams(dimension_semantics=sem, vmem_limit_bytes=VMEM_LIMIT_BYTES)


def _pick_block(n, candidates):
    for c in candidates:
        if n % c == 0:
            return c
    raise ValueError(f"no block size in {candidates} divides {n}")


def _rms(x, g):
    ms = jnp.mean(x * x, axis=-1, keepdims=True)
    return x * lax.rsqrt(ms + EPS) * g


def _log_sigmoid(x):
    return jnp.minimum(x, 0.0) - jnp.log1p(jnp.exp(-jnp.abs(x)))


def _silu(x):
    return x * jax.nn.sigmoid(x)


def _dot(a, b):
    return jnp.dot(a, b, preferred_element_type=F32)


def _dot_nt(a, b):
    return lax.dot_general(a, b, (((1,), (1,)), ((), ())), preferred_element_type=F32)


def _dot_tn(a, b):
    return lax.dot_general(a, b, (((0,), (0,)), ((), ())), preferred_element_type=F32)


def _tok(tb, d):
    return pl.BlockSpec((tb, d), lambda i: (i, 0))


def _full(shape):
    nd = len(shape)
    return pl.BlockSpec(shape, lambda *_: (0,) * nd)


def _gla_pre_kernel(h_ref, g_ref, w_ref, wgl_ref, wgate_ref, bgate_ref,
                    q_ref, k_ref, v_ref, r_ref, lg_ref, *, dk, dv, hk):
    xn = _rms(h_ref[...], g_ref[...]).astype(BF16)
    proj = _dot(xn, w_ref[...])
    q_ref[...] = proj[:, :dk] * (hk ** -0.5)
    k_ref[...] = proj[:, dk:2 * dk]
    v_ref[...] = proj[:, 2 * dk:2 * dk + dv].astype(BF16)
    r_ref[...] = proj[:, 2 * dk + dv:]
    gl = _dot(xn, wgl_ref[...])
    gate = _dot(gl.astype(BF16), wgate_ref[...]) + bgate_ref[...]
    lg_ref[...] = _log_sigmoid(gate) * (1.0 / GLA_GATE_NORMALIZER)


def _gla_pre(h, g, w_qkvr, w_gl, w_gate, b_gate, dk, dv):
    n, d = h.shape
    tb = _pick_block(n, (512, 256, 128, 64))
    hk = dk // GLA_HEADS
    outs = (jax.ShapeDtypeStruct((n, dk), F32), jax.ShapeDtypeStruct((n, dk), F32),
            jax.ShapeDtypeStruct((n, dv), BF16), jax.ShapeDtypeStruct((n, dv), F32),
            jax.ShapeDtypeStruct((n, dk), F32))
    return pl.pallas_call(
        functools.partial(_gla_pre_kernel, dk=dk, dv=dv, hk=hk),
        grid=(n // tb,),
        in_specs=[_tok(tb, d), _full(g.shape), _full(w_qkvr.shape), _full(w_gl.shape),
                  _full(w_gate.shape), _full(b_gate.shape)],
        out_specs=(_tok(tb, dk), _tok(tb, dk), _tok(tb, dv), _tok(tb, dv), _tok(tb, dk)),
        out_shape=outs,
        compiler_params=_cparams(("parallel",)),
        name="gla_pre",
    )(h, g, w_qkvr, w_gl, w_gate, b_gate)


GLA_SUB = 16


def _gla_scan_kernel(*refs, nchunks, zero_init):
    if zero_init:
        q_ref, k_ref, v_ref, lg_ref, gn_ref, o_ref, sout_ref, st_ref = refs
        s0_ref = None
    else:
        q_ref, k_ref, v_ref, lg_ref, gn_ref, s0_ref, o_ref, sout_ref, st_ref = refs
    t = pl.program_id(1)
    nheads, hv, hk = st_ref.shape

    @pl.when(t == 0)
    def _():
        if zero_init:
            st_ref[...] = jnp.zeros_like(st_ref)
        else:
            for h in range(nheads):
                st_ref[h] = s0_ref[h].T

    row = lax.broadcasted_iota(jnp.int32, (CHUNK, CHUNK), 0)
    col = lax.broadcasted_iota(jnp.int32, (CHUNK, CHUNK), 1)
    tril = (row >= col).astype(F32)
    gn = gn_ref[...]
    nsub = CHUNK // GLA_SUB

    def bmm(a, b, lhs_c, rhs_c):
        return lax.dot_general(a, b, (((lhs_c,), (rhs_c,)), ((0,), (0,))), preferred_element_type=F32)

    pairs = [(c, h) for c in range(nchunks) for h in range(nheads)]
    b_chunks = [jnp.dot(tril, lg_ref[pl.ds(c * CHUNK, CHUNK), :], precision=HIGHEST,
                        preferred_element_type=F32) for c in range(nchunks)]
    b = jnp.stack([b_chunks[c][:, h * hk:(h + 1) * hk] for c, h in pairs])
    q = jnp.stack([q_ref[pl.ds(c * CHUNK, CHUNK), h * hk:(h + 1) * hk] for c, h in pairs])
    k = jnp.stack([k_ref[pl.ds(c * CHUNK, CHUNK), h * hk:(h + 1) * hk] for c, h in pairs])
    v = jnp.stack([v_ref[pl.ds(c * CHUNK, CHUNK), h * hv:(h + 1) * hv] for c, h in pairs])
    b_last = b[:, CHUNK - 1:CHUNK, :]
    intra_parts = []
    for i in range(nsub):
        lo, hi = i * GLA_SUB, (i + 1) * GLA_SUB
        b_ref_row = b[:, lo:lo + 1, :]
        qe = (q[:, lo:hi, :] * jnp.exp(b[:, lo:hi, :] - b_ref_row)).astype(BF16)
        ke = (k[:, :hi, :] * jnp.exp(b_ref_row - b[:, :hi, :])).astype(BF16)
        att = bmm(qe, ke, 2, 2)
        r_i = lax.broadcasted_iota(jnp.int32, (GLA_SUB, hi), 0) + lo
        c_i = lax.broadcasted_iota(jnp.int32, (GLA_SUB, hi), 1)
        att = jnp.where(r_i >= c_i, att, 0.0)
        intra_parts.append(bmm(att.astype(BF16), v[:, :hi, :], 2, 1))
    intra = jnp.concatenate(intra_parts, axis=1)
    kv = bmm(v, (k * jnp.exp(b_last - b)).astype(BF16), 1, 1)
    q_dec = (q * jnp.exp(b)).astype(BF16)
    decay = jnp.exp(b_last)
    st = st_ref[...]
    for c in range(nchunks):
        ps = slice(c * nheads, (c + 1) * nheads)
        o = bmm(q_dec[ps], st.astype(BF16), 2, 2) + intra[ps]
        st = st * decay[ps] + kv[ps]
        o = o * lax.rsqrt(jnp.mean(o * o, axis=-1, keepdims=True) + EPS) * gn
        for h in range(nheads):
            o_ref[pl.ds(c * CHUNK, CHUNK), h * hv:(h + 1) * hv] = o[h]
    st_ref[...] = st

    @pl.when(t == pl.num_programs(1) - 1)
    def _():
        for h in range(nheads):
            sout_ref[h] = st_ref[h].T


def _gla_scan(q, k, v, lg, gn, s0, nseq, seqlen):
    n, dk = q.shape
    dv = v.shape[1]
    hk, hv = dk // GLA_HEADS, dv // GLA_HEADS
    tt = _pick_block(seqlen, (256, 128, 64))
    nt = seqlen // tt
    zero_init = s0 is None
    tmap = lambda s, t: (s * nt + t, 0)
    smap = lambda s, t: (s, 0, 0, 0)
    in_specs = [pl.BlockSpec((tt, dk), tmap), pl.BlockSpec((tt, dk), tmap),
                pl.BlockSpec((tt, dv), tmap), pl.BlockSpec((tt, dk), tmap),
                pl.BlockSpec((1, hv), lambda s, t: (0, 0))]
    args = [q, k, v, lg, gn]
    if not zero_init:
        in_specs.append(pl.BlockSpec((None, GLA_HEADS, hk, hv), smap))
        args.append(s0)
    return pl.pallas_call(
        functools.partial(_gla_scan_kernel, nchunks=tt // CHUNK, zero_init=zero_init),
        grid=(nseq, nt),
        in_specs=in_specs,
        out_specs=(pl.BlockSpec((tt, dv), tmap), pl.BlockSpec((None, GLA_HEADS, hk, hv), smap)),
        out_shape=(jax.ShapeDtypeStruct((n, dv), F32),
                   jax.ShapeDtypeStruct((nseq, GLA_HEADS, hk, hv), F32)),
        scratch_shapes=[pltpu.VMEM((GLA_HEADS, hv, hk), F32)],
        compiler_params=_cparams(("parallel", "arbitrary")),
        name="gla_scan",
    )(*args)


def _peer_pre_tail(h1, gffn_ref, wqt_ref, xt_ref, qt_ref):
    xn = _rms(h1, gffn_ref[...])
    xt = xn.T.astype(BF16)
    xt_ref[...] = xt
    qt_ref[...] = _dot(wqt_ref[...], xt).astype(BF16)


def _gla_post_kernel(o_ref, r_ref, h_ref, wout_ref, gffn_ref, wqt_ref, h1_ref, xt_ref, qt_ref):
    y = (o_ref[...] * _silu(r_ref[...])).astype(BF16)
    h1 = h_ref[...] + _dot(y, wout_ref[...])
    h1_ref[...] = h1
    _peer_pre_tail(h1, gffn_ref, wqt_ref, xt_ref, qt_ref)


def _gla_post(o, r, h, w_out, g_ffn, w_qt):
    n, d = h.shape
    tb = _pick_block(n, (512, 256, 128))
    dq = w_qt.shape[0]
    return pl.pallas_call(
        _gla_post_kernel,
        grid=(n // tb,),
        in_specs=[_tok(tb, o.shape[1]), _tok(tb, r.shape[1]), _tok(tb, d), _full(w_out.shape),
                  _full(g_ffn.shape), _full(w_qt.shape)],
        out_specs=(_tok(tb, d), pl.BlockSpec((d, tb), lambda i: (0, i)),
                   pl.BlockSpec((dq, tb), lambda i: (0, i))),
        out_shape=(jax.ShapeDtypeStruct((n, d), F32), jax.ShapeDtypeStruct((d, n), BF16),
                   jax.ShapeDtypeStruct((dq, n), BF16)),
        compiler_params=_cparams(("parallel",)),
        name="gla_post",
    )(o, r, h, w_out, g_ffn, w_qt)


def _ml_pre_kernel(h_ref, g_ref, w_ref, xm_ref, z_ref, *, di):
    xn = _rms(h_ref[...], g_ref[...]).astype(BF16)
    up = _dot(xn, w_ref[...])
    xm_ref[...] = up[:, :di]
    z_ref[...] = up[:, di:]


def _ml_pre(h, g, w_up):
    n, d = h.shape
    di = w_up.shape[1] // 2
    tb = _pick_block(n, (512, 256, 128, 64))
    return pl.pallas_call(
        functools.partial(_ml_pre_kernel, di=di),
        grid=(n // tb,),
        in_specs=[_tok(tb, d), _full(g.shape), _full(w_up.shape)],
        out_specs=(_tok(tb, di), _tok(tb, di)),
        out_shape=(jax.ShapeDtypeStruct((n, di), F32), jax.ShapeDtypeStruct((n, di), F32)),
        compiler_params=_cparams(("parallel",)),
        name="ml_pre",
    )(h, g, w_up)


HALO = 8


def _ml_qkv_kernel(xm_ref, prev_ref, buf_ref, cw_ref, cb_ref, wq_ref, wk_ref, wv_ref, wif_ref, bif_ref,
                   q_ref, k_ref, v_ref, xc_ref, gates_ref, xp_scr, *, tt, di, hd):
    t = pl.program_id(1)
    x = xm_ref[...]
    halo = jnp.where(t == 0, buf_ref[...], prev_ref[...])
    xp_scr[0:HALO, :] = halo
    xp_scr[HALO:HALO + tt, :] = x
    y = cb_ref[...]
    for j in range(ML_CONV):
        y = y + cw_ref[j:j + 1, :] * xp_scr[pl.ds(HALO - (ML_CONV - 1) + j, tt), :]
    xc = _silu(y)
    xc_ref[...] = xc
    xcb = xc.astype(BF16)
    xmb = x.astype(BF16)
    gates = jnp.zeros((tt, gates_ref.shape[1]), F32) + bif_ref[...]
    ntile = di // MXU_TILE
    for j in range(ntile):
        cs = slice(j * MXU_TILE, (j + 1) * MXU_TILE)
        qj = _dot(xcb[:, cs], wq_ref[j])
        kj = _dot(xcb[:, cs], wk_ref[j])
        vj = _dot(xmb[:, cs], wv_ref[j])
        q_ref[:, cs] = qj.astype(BF16)
        k_ref[:, cs] = kj * (hd ** -0.5)
        v_ref[:, cs] = vj.astype(BF16)
        gates = gates + _dot(qj.astype(BF16), wif_ref[j * MXU_TILE:(j + 1) * MXU_TILE, :])
        gates = gates + _dot(kj.astype(BF16), wif_ref[di + j * MXU_TILE:di + (j + 1) * MXU_TILE, :])
        gates = gates + _dot(vj.astype(BF16), wif_ref[2 * di + j * MXU_TILE:2 * di + (j + 1) * MXU_TILE, :])
    lane = lax.broadcasted_iota(jnp.int32, gates.shape, 1)
    is_f = (lane >= ML_HEADS) & (lane < 2 * ML_HEADS)
    gates_ref[...] = jnp.where(is_f, _log_sigmoid(gates), gates)


def _ml_qkv(xm, buf, conv_w, conv_b, wq, wk, wv, wif, bif, nseq, seqlen):
    n, di = xm.shape
    hd = di // ML_HEADS
    tt = _pick_block(seqlen, (256, 128, 64))
    nt = seqlen // tt
    tmap = lambda s, t: (s * nt + t, 0)
    prev_map = lambda s, t: (jnp.maximum((s * nt + t) * (tt // HALO) - 1, 0), 0)
    ng = wif.shape[1]
    return pl.pallas_call(
        functools.partial(_ml_qkv_kernel, tt=tt, di=di, hd=hd),
        grid=(nseq, nt),
        in_specs=[pl.BlockSpec((tt, di), tmap), pl.BlockSpec((HALO, di), prev_map),
                  pl.BlockSpec((None, HALO, di), lambda s, t: (s, 0, 0)),
                  _full(conv_w.shape), _full(conv_b.shape), _full(wq.shape), _full(wk.shape),
                  _full(wv.shape), _full(wif.shape), _full(bif.shape)],
        out_specs=(pl.BlockSpec((tt, di), tmap), pl.BlockSpec((tt, di), tmap),
                   pl.BlockSpec((tt, di), tmap), pl.BlockSpec((tt, di), tmap),
                   pl.BlockSpec((tt, ng), tmap)),
        out_shape=(jax.ShapeDtypeStruct((n, di), BF16), jax.ShapeDtypeStruct((n, di), F32),
                   jax.ShapeDtypeStruct((n, di), BF16), jax.ShapeDtypeStruct((n, di), F32),
                   jax.ShapeDtypeStruct((n, ng), F32)),
        scratch_shapes=[pltpu.VMEM((HALO + tt, di), F32)],
        compiler_params=_cparams(("parallel", "arbitrary")),
        name="ml_qkv",
    )(xm, xm, buf, conv_w, conv_b, wq, wk, wv, wif, bif)


def _ml_scan_kernel(*refs, nchunks, blk, zero_init):
    if zero_init:
        q_ref, k_ref, v_ref, g_ref, h_ref, cout_ref, nout_ref, mout_ref, c_scr, n_scr, m_scr = refs
    else:
        (q_ref, k_ref, v_ref, g_ref, c0_ref, n0_ref, m0_ref,
         h_ref, cout_ref, nout_ref, mout_ref, c_scr, n_scr, m_scr) = refs
    t = pl.program_id(1)
    nheads, hd = c_scr.shape[0], c_scr.shape[1]

    @pl.when(t == 0)
    def _():
        if zero_init:
            c_scr[...] = jnp.zeros_like(c_scr)
            n_scr[...] = jnp.zeros_like(n_scr)
            m_scr[...] = jnp.zeros_like(m_scr)
        else:
            c_scr[...] = c0_ref[...]
            n_scr[...] = n0_ref[...]
            m_scr[...] = m0_ref[...]

    row = lax.broadcasted_iota(jnp.int32, (blk, blk), 0)
    col = lax.broadcasted_iota(jnp.int32, (blk, blk), 1)
    causal = row >= col
    tril = causal.astype(F32)
    lanes = g_ref.shape[1]

    def bmm(a, b, lhs_c, rhs_c, precision=None):
        return lax.dot_general(a, b, (((lhs_c,), (rhs_c,)), ((0,), (0,))), precision=precision,
                               preferred_element_type=F32)

    shared = {}
    for c, h0 in [(c, h0) for c in range(nchunks) for h0 in range(0, nheads, ML_HEAD_BATCH)]:
        heads = range(h0, h0 + ML_HEAD_BATCH)
        hsl = slice(h0, h0 + ML_HEAD_BATCH)
        sl = pl.ds(c * blk, blk)
        q = jnp.stack([q_ref[sl, h * hd:(h + 1) * hd] for h in heads])
        kf = jnp.stack([k_ref[sl, h * hd:(h + 1) * hd] for h in heads])
        v = jnp.stack([v_ref[sl, h * hd:(h + 1) * hd] for h in heads])
        cmat = c_scr[hsl]
        nvec = n_scr[hsl]
        m = m_scr[hsl][:, :, 0:1]
        if c not in shared:
            g = g_ref[sl, :]
            gcum = jnp.dot(tril, g, precision=HIGHEST, preferred_element_type=F32)
            diff_t = (g - pltpu.roll(gcum, lanes - nheads, axis=1)).T if blk % lanes == 0 else None
            shared[c] = (g, gcum, diff_t)
        g, gcum, diff_t = shared[c]
        i_col = jnp.stack([g[:, h:h + 1] for h in heads])
        f_col = jnp.stack([gcum[:, h + nheads:h + nheads + 1] for h in heads])
        if diff_t is not None:
            w_row = jnp.stack([diff_t[h:h + 1, :] for h in heads])
        else:
            eye = row == col
            w_row = bmm(jnp.ones((ML_HEAD_BATCH, blk, blk), F32), jnp.where(eye, i_col - f_col, 0.0), 2, 1,
                        precision=HIGHEST)
        log_d = jnp.where(causal, f_col + w_row, NEG_INF)
        inter = f_col + m
        mt = jnp.maximum(inter, jnp.max(log_d, axis=-1, keepdims=True))
        sc = bmm(q, kf.astype(BF16), 2, 2) * jnp.exp(log_d - mt)
        a = jnp.exp(inter - mt)
        num = a * bmm(q, cmat.astype(BF16), 2, 1) + bmm(sc.astype(BF16), v, 2, 1)
        qn = jnp.sum(q.astype(F32) * nvec, axis=-1, keepdims=True)
        den = a * qn + jnp.sum(sc, axis=-1, keepdims=True)
        hh = num / jnp.maximum(jnp.abs(den), jnp.exp(-mt))
        m_new = mt[:, blk - 1:blk, :]
        f_last = f_col[:, blk - 1:blk, :]
        w_end = jnp.exp(f_last - f_col + i_col - m_new)
        a_end = jnp.exp(f_last + m - m_new)
        kw = kf * w_end
        c_scr[hsl] = a_end * cmat + bmm(kw.astype(BF16), v, 1, 1)
        n_scr[hsl] = a_end * nvec + jnp.sum(kw, axis=1, keepdims=True)
        m_scr[hsl] = jnp.broadcast_to(m_new, (ML_HEAD_BATCH, 1, M_LANES))
        mu = jnp.mean(hh, axis=-1, keepdims=True)
        var = jnp.mean(jnp.square(hh - mu), axis=-1, keepdims=True)
        hn = (hh - mu) * lax.rsqrt(var + EPS)
        for i, h in enumerate(heads):
            h_ref[sl, h * hd:(h + 1) * hd] = hn[i]

    @pl.when(t == pl.num_programs(1) - 1)
    def _():
        cout_ref[...] = c_scr[...]
        nout_ref[...] = n_scr[...]
        mout_ref[...] = m_scr[...]


M_LANES = 128
ML_SCAN_BLOCK = 256
ML_HEAD_BATCH = 2


def _ml_scan(q, k, v, gates, state, nseq, seqlen):
    n, di = q.shape
    hd = di // ML_HEADS
    tt = _pick_block(seqlen, (256, 128, 64))
    blk = min(tt, ML_SCAN_BLOCK)
    nt = seqlen // tt
    zero_init = state is None
    tmap = lambda s, t: (s * nt + t, 0)
    smap = lambda s, t: (s, 0, 0, 0)
    ng = gates.shape[1]
    nh = ML_HEADS
    in_specs = [pl.BlockSpec((tt, di), tmap), pl.BlockSpec((tt, di), tmap),
                pl.BlockSpec((tt, di), tmap), pl.BlockSpec((tt, ng), tmap)]
    args = [q, k, v, gates]
    state_specs = [pl.BlockSpec((None, nh, hd, hd), smap), pl.BlockSpec((None, nh, 1, hd), smap),
                   pl.BlockSpec((None, nh, 1, M_LANES), smap)]
    if not zero_init:
        in_specs += state_specs
        args += list(state)
    return pl.pallas_call(
        functools.partial(_ml_scan_kernel, nchunks=tt // blk, blk=blk, zero_init=zero_init),
        grid=(nseq, nt),
        in_specs=in_specs,
        out_specs=(pl.BlockSpec((tt, di), tmap), *state_specs),
        out_shape=(jax.ShapeDtypeStruct((n, di), F32),
                   jax.ShapeDtypeStruct((nseq, nh, hd, hd), F32),
                   jax.ShapeDtypeStruct((nseq, nh, 1, hd), F32),
                   jax.ShapeDtypeStruct((nseq, nh, 1, M_LANES), F32)),
        scratch_shapes=[pltpu.VMEM((nh, hd, hd), F32), pltpu.VMEM((nh, 1, hd), F32),
                        pltpu.VMEM((nh, 1, M_LANES), F32)],
        compiler_params=_cparams(("parallel", "arbitrary")),
        name="ml_scan",
    )(*args)


def _ml_post_kernel(hn_ref, xc_ref, z_ref, h_ref, gn_ref, skip_ref, wdown_ref, gffn_ref, wqt_ref,
                    h1_ref, xt_ref, qt_ref):
    y = (hn_ref[...] * gn_ref[...] + skip_ref[...] * xc_ref[...]) * _silu(z_ref[...])
    h1 = h_ref[...] + _dot(y.astype(BF16), wdown_ref[...])
    h1_ref[...] = h1
    _peer_pre_tail(h1, gffn_ref, wqt_ref, xt_ref, qt_ref)


def _ml_post(hn, xc, z, h, g_norm, skip, w_down, g_ffn, w_qt):
    n, d = h.shape
    di = hn.shape[1]
    tb = _pick_block(n, (512, 256, 128))
    dq = w_qt.shape[0]
    return pl.pallas_call(
        _ml_post_kernel,
        grid=(n // tb,),
        in_specs=[_tok(tb, di), _tok(tb, di), _tok(tb, di), _tok(tb, d), _full(g_norm.shape),
                  _full(skip.shape), _full(w_down.shape), _full(g_ffn.shape), _full(w_qt.shape)],
        out_specs=(_tok(tb, d), pl.BlockSpec((d, tb), lambda i: (0, i)),
                   pl.BlockSpec((dq, tb), lambda i: (0, i))),
        out_shape=(jax.ShapeDtypeStruct((n, d), F32), jax.ShapeDtypeStruct((d, n), BF16),
                   jax.ShapeDtypeStruct((dq, n), BF16)),
        compiler_params=_cparams(("parallel",)),
        name="ml_post",
    )(hn, xc, z, h, g_norm, skip, w_down, g_ffn, w_qt)


NCAND_ROWS = 72


def _first_max(vals, ids, big):
    m = jnp.max(vals, axis=0, keepdims=True)
    sel = jnp.min(jnp.where(vals == m, ids, big), axis=0, keepdims=True)
    return m, sel


def _sorting_network(n):
    size = 1
    while size < n:
        size *= 2
    pairs = []
    p = 1
    while p < size:
        k = p
        while k >= 1:
            for j in range(k % p, size - k, 2 * k):
                for i in range(min(k, size - j - k)):
                    if (i + j) // (2 * p) == (i + j + k) // (2 * p):
                        pairs.append((i + j, i + j + k))
            k //= 2
        p *= 2
    return [(i, j) for i, j in pairs if j < n]


SUBLANES = 8


def _sorted_best16(s):
    k16 = PEER_TOPK
    rows, lanes = s.shape
    nslab = rows // SUBLANES
    v = [s[SUBLANES * j:SUBLANES * (j + 1), :] for j in range(nslab)]
    for i, j in _sorting_network(nslab):
        v[i], v[j] = jnp.maximum(v[i], v[j]), jnp.minimum(v[i], v[j])
    v = v[:k16] + [jnp.full((SUBLANES, lanes), NEG_INF, F32)] * (k16 - nslab)
    for shift in (1, 2, 4):
        m = [jnp.maximum(v[i], pltpu.roll(v[k16 - 1 - i], shift, axis=0)) for i in range(k16)]
        d = k16 // 2
        while d >= 1:
            for i in range(k16):
                if (i // d) % 2 == 0:
                    m[i], m[i + d] = jnp.maximum(m[i], m[i + d]), jnp.minimum(m[i], m[i + d])
            d //= 2
        v = m
    return v


def _tie_flag(tops, member):
    tie = jnp.zeros_like(tops[0])
    for r in range(len(tops) - 1):
        tie = jnp.maximum(tie, (tops[r] == tops[r + 1]).astype(F32))
    count = jnp.sum(member.astype(F32), axis=0, keepdims=True)
    return jnp.maximum(jnp.max(tie, axis=0, keepdims=True), (count != float(PEER_TOPK)).astype(F32))


def _peer_route_kernel(qt_ref, keys_ref, r2_ref, e2_ref, n_ref, e1_ref, s_scr, rank_scr, top_scr, *, tk):
    nk, k16 = PEER_NKEYS, PEER_TOPK
    key_id = lax.broadcasted_iota(jnp.int32, (nk, tk), 0)
    slot_id = lax.broadcasted_iota(jnp.int32, (k16, tk), 0)

    def half_body(hc, carry):
        start = pl.multiple_of(hc * PEER_HALF, PEER_HALF)
        s = _dot(keys_ref[hc], qt_ref[pl.ds(start, PEER_HALF), :])
        s_scr[hc] = s

        tops = _sorted_best16(s)
        s3 = s.reshape(nk // SUBLANES, SUBLANES, tk)
        member = s3 >= tops[k16 - 1][None]
        tie = _tie_flag(tops, member.reshape(nk, tk))

        def by_value():
            rank = jnp.where(member, float(k16 - 1), float(k16))
            for r in range(k16 - 2, -1, -1):
                rank = jnp.where(s3 >= tops[r][None], float(r), rank)
            sub = lax.broadcasted_iota(jnp.int32, (SUBLANES, tk), 0)
            halves = []
            for base in range(0, k16, SUBLANES):
                slab = tops[base]
                for r in range(1, SUBLANES):
                    slab = jnp.where(sub == r, tops[base + r], slab)
                halves.append(slab)
            return rank.reshape(nk, tk), jnp.concatenate(halves, axis=0)

        def by_value_and_index():
            def pick(r, st):
                s_cur, rank, top = st
                m, sel = _first_max(s_cur, key_id, nk)
                hit = key_id == sel
                return (jnp.where(hit, NEG_INF, s_cur), jnp.where(hit, jnp.asarray(r, F32), rank),
                        jnp.where(slot_id == r, m, top))

            _, rank, top = lax.fori_loop(
                0, k16, pick, (s, jnp.full((nk, tk), float(k16), F32), jnp.zeros((k16, tk), F32)))
            return rank, top

        rank, top = lax.cond(jnp.max(tie) > 0.0, by_value_and_index, by_value)
        rank_scr[hc] = rank
        top_scr[hc] = top
        return carry

    lax.fori_loop(0, 2 * PEER_HEADS, half_body, 0)

    def cand_ids():
        i16 = lax.broadcasted_iota(jnp.int32, (16, tk), 0)
        i8 = lax.broadcasted_iota(jnp.int32, (8, tk), 0)
        parts = [i16, 16 + i8, 32 + i8, 48 + i8, 16 * i16, 16 * i8 + 1, 16 * i8 + 2]
        return jnp.concatenate(parts, axis=0)

    cand_id = cand_ids()
    row72 = lax.broadcasted_iota(jnp.int32, (NCAND_ROWS, tk), 0)
    cand_ok = (row72 < 40) | (cand_id >= 64)
    big_id = 16 * 16

    def head_body(h, carry):
        a = top_scr[2 * h]
        b = top_scr[2 * h + 1]
        parts = [a[0:1, :] + b, a[1:2, :] + b[0:8, :], a[2:3, :] + b[0:8, :], a[3:4, :] + b[0:8, :],
                 a + b[0:1, :], a[0:8, :] + b[1:2, :], a[0:8, :] + b[2:3, :]]
        cand = jnp.where(cand_ok, jnp.concatenate(parts, axis=0), NEG_INF)
        tops = _sorted_best16(cand)
        cand3 = cand.reshape(NCAND_ROWS // SUBLANES, SUBLANES, tk)
        sel = (cand3 >= tops[k16 - 1][None]).reshape(NCAND_ROWS, tk)
        tie = _tie_flag(tops, sel)

        def by_value():
            self = sel.astype(F32)
            cnt = self[40:56, :] + jnp.concatenate(
                [self[56:64, :] + self[64:72, :], jnp.zeros((8, tk), F32)], axis=0)
            for r1, (lo, hi) in enumerate(((0, 16), (16, 24), (24, 32), (32, 40))):
                row = jnp.sum(self[lo:hi, :], axis=0, keepdims=True)
                cnt = cnt + jnp.where(slot_id == r1, row, 0.0)
            ex = jnp.exp(cand3 - tops[0][None]).reshape(NCAND_ROWS, tk)
            z = jnp.sum(jnp.where(sel, ex, 0.0), axis=0, keepdims=True)
            return cnt, z

        def by_value_and_index():
            def pick(r, st):
                cand_cur, cnt, tsel = st
                m, sel = _first_max(cand_cur, cand_id, big_id)
                cand_cur = jnp.where(cand_id == sel, NEG_INF, cand_cur)
                cnt = cnt + (slot_id == (sel >> 4)).astype(F32)
                return cand_cur, cnt, jnp.where(slot_id == r, m, tsel)

            _, cnt, tsel = lax.fori_loop(
                0, k16, pick, (cand, jnp.zeros((k16, tk), F32), jnp.zeros((k16, tk), F32)))
            return cnt, jnp.sum(jnp.exp(tsel - tsel[0:1, :]), axis=0, keepdims=True)

        cnt, z = lax.cond(jnp.max(tie) > 0.0, by_value_and_index, by_value)
        rank1 = rank_scr[2 * h]
        rank2 = rank_scr[2 * h + 1]
        pk = 16
        rank1b = rank1.astype(BF16).reshape(nk // pk, pk, tk)
        n_sel = jnp.zeros((nk // pk, pk, tk), BF16)
        for r in range(k16):
            row = jnp.broadcast_to(cnt[r:r + 1, :], (pk, tk)).astype(BF16)
            n_sel = jnp.where(rank1b == r, row[None], n_sel)
        n_ref[h] = n_sel.reshape(nk, tk).astype(F32)
        e1_ref[h] = jnp.where(rank1 < k16, jnp.exp(s_scr[2 * h] - a[0:1, :]) * (GELU_OUT_SCALE / z), 0.0)
        r2_ref[h] = rank2.astype(BF16)
        e2_ref[h] = jnp.where(rank2 < k16, jnp.exp(s_scr[2 * h + 1] - b[0:1, :]), 0.0).astype(BF16)
        return carry

    lax.fori_loop(0, PEER_HEADS, head_body, 0)


def _peer_route(qt, keys):
    dq, n = qt.shape
    tk = _pick_block(n, (512, 256, 128))
    nk = PEER_NKEYS
    shp = jax.ShapeDtypeStruct((PEER_HEADS, nk, n), F32)
    shp16 = jax.ShapeDtypeStruct((PEER_HEADS, nk, n), BF16)
    ospec = pl.BlockSpec((PEER_HEADS, nk, tk), lambda i: (0, 0, i))
    return pl.pallas_call(
        functools.partial(_peer_route_kernel, tk=tk),
        grid=(n // tk,),
        in_specs=[pl.BlockSpec((dq, tk), lambda i: (0, i)), _full(keys.shape)],
        out_specs=(ospec, ospec, ospec, ospec),
        out_shape=(shp16, shp16, shp, shp),
        scratch_shapes=[pltpu.VMEM((2 * PEER_HEADS, nk, tk), F32), pltpu.VMEM((2 * PEER_HEADS, nk, tk), F32),
                        pltpu.VMEM((2 * PEER_HEADS, PEER_TOPK, tk), F32)],
        compiler_params=_cparams(("parallel",)),
        name="peer_route",
    )(qt, keys)


def _peer_gate_stage(row0, act_ref, coef_ref, r2_ref, e2_ref, n_ref, e1_ref):
    nk = PEER_NKEYS
    t = act_ref.shape[1]
    pk = 16
    zero = jnp.zeros((), BF16)
    for al in range(PEER_EXPERT_PIECE // nk):
        a = row0 + al
        rows = slice(al * nk, (al + 1) * nk)
        y = act_ref[rows, :]
        ge = (y * (1.0 + lax.erf(y))).astype(BF16).reshape(nk // pk, pk, t)
        g = None
        for h in range(PEER_HEADS):
            n16 = jnp.broadcast_to(n_ref[h, a:a + 1, :], (pk, t)).astype(BF16)
            e16 = jnp.broadcast_to(e1_ref[h, a:a + 1, :], (pk, t)).astype(BF16)
            r2h = r2_ref[h].reshape(nk // pk, pk, t)
            e2h = e2_ref[h].reshape(nk // pk, pk, t)
            term = jnp.where(r2h < n16[None], e2h, zero) * e16[None]
            g = term if g is None else g + term
        coef_ref[rows, :] = (g * ge).reshape(nk, t)


def _peer_expert_kernel(xt_ref, r2_ref, e2_ref, n_ref, e1_ref, u_ref, vt_ref, out_ref, act_scr, coef_scr):
    k = pl.program_id(1)
    ep = PEER_EXPERT_PIECE
    npiece = act_scr.shape[0]
    rows = ep // PEER_NKEYS

    @pl.when(k == 0)
    def _():
        out_ref[...] = jnp.zeros_like(out_ref)

    xt = xt_ref[...]
    for p in range(npiece):
        act_scr[p] = _dot(u_ref[p * ep:(p + 1) * ep, :], xt)
    for p in range(npiece):
        _peer_gate_stage(p * rows, act_scr.at[p], coef_scr.at[p], r2_ref, e2_ref, n_ref, e1_ref)
    for p in range(npiece):
        out_ref[...] += _dot(vt_ref[:, p * ep:(p + 1) * ep], coef_scr[p])


def _peer_expert(xt, r2, e2, nsel, e1, u, vt):
    d, n = xt.shape
    ne = u.shape[0]
    t = _pick_block(n, (512, 256, 128))
    et = PEER_EXPERT_STEP
    npiece = et // PEER_EXPERT_PIECE
    rspec = pl.BlockSpec((PEER_HEADS, PEER_NKEYS, t), lambda i, k: (0, 0, i))
    row_spec = pl.BlockSpec((PEER_HEADS, et // PEER_NKEYS, t), lambda i, k: (0, k, i))
    return pl.pallas_call(
        _peer_expert_kernel,
        grid=(n // t, ne // et),
        in_specs=[pl.BlockSpec((d, t), lambda i, k: (0, i)), rspec, rspec, row_spec, row_spec,
                  pl.BlockSpec((et, d), lambda i, k: (k, 0)),
                  pl.BlockSpec((d, et), lambda i, k: (0, k))],
        out_specs=pl.BlockSpec((d, t), lambda i, k: (0, i)),
        out_shape=jax.ShapeDtypeStruct((d, n), F32),
        scratch_shapes=[pltpu.VMEM((npiece, PEER_EXPERT_PIECE, t), F32),
                        pltpu.VMEM((npiece, PEER_EXPERT_PIECE, t), BF16)],
        compiler_params=_cparams(("parallel", "arbitrary")),
        name="peer_expert",
    )(xt, r2, e2, nsel, e1, u, vt)


def _ple_kernel(ot_ref, h1_ref, p_ref, gple_ref, wgate_ref, wproj_ref, *rest, final):
    h2 = h1_ref[...] + ot_ref[...].T
    gate = jax.nn.sigmoid(_dot(_rms(h2, gple_ref[...]).astype(BF16), wgate_ref[...]))
    h3 = h2 + gate * _dot(p_ref[...].astype(BF16), wproj_ref[...])
    if final:
        gfinal_ref, y_ref = rest
        y_ref[...] = _rms(h3, gfinal_ref[...])
    else:
        (h_ref,) = rest
        h_ref[...] = h3


def _ple(out_t, h1, p, g_ple, w_gate, w_proj, g_final=None):
    n, d = h1.shape
    tb = _pick_block(n, (512, 256, 128))
    final = g_final is not None
    in_specs = [pl.BlockSpec((d, tb), lambda i: (0, i)), _tok(tb, d), _tok(tb, p.shape[1]),
                _full(g_ple.shape), _full(w_gate.shape), _full(w_proj.shape)]
    args = [out_t, h1, p, g_ple, w_gate, w_proj]
    if final:
        in_specs.append(_full(g_final.shape))
        args.append(g_final)
    return pl.pallas_call(
        functools.partial(_ple_kernel, final=final),
        grid=(n // tb,),
        in_specs=in_specs,
        out_specs=_tok(tb, d),
        out_shape=jax.ShapeDtypeStruct((n, d), F32),
        compiler_params=_cparams(("parallel",)),
        name="ple",
    )(*args)


def _block_diag_tiles(w):
    g, bi, bo = w.shape
    per = MXU_TILE // bi
    wt = w.reshape(g // per, per, bi, bo)
    eye = jnp.eye(per, dtype=w.dtype)
    dense = jnp.einsum("tgio,gh->tgiho", wt, eye)
    return dense.reshape(g // per, per * bi, per * bo)


def _row(x):
    return x.reshape(1, -1)


def _prep_weights(W, depth):
    P = {}
    n_gla = W["w_gla_in"].shape[0]
    dk = W["w_gla_gate"].shape[2]
    rank = W["w_gla_gate"].shape[1]
    dv = (W["w_gla_in"].shape[2] - rank - 2 * dk) // 2
    P["gla_dims"] = (dk, dv)
    pad = 128 - rank
    P["w_gla_qkvr"] = W["w_gla_in"][:, :, :2 * dk + 2 * dv].astype(BF16)
    P["w_gla_gl"] = jnp.pad(W["w_gla_in"][:, :, 2 * dk + 2 * dv:], ((0, 0), (0, 0), (0, pad))).astype(BF16)
    P["w_gla_gate"] = jnp.pad(W["w_gla_gate"], ((0, 0), (0, pad), (0, 0))).astype(BF16)
    P["w_gla_out"] = W["w_gla_out"].astype(BF16)
    P["w_ml_up"] = W["w_ml_up"].astype(BF16)
    n_ml = W["w_ml_up"].shape[0]
    for nm in ("w_ml_q", "w_ml_k", "w_ml_v"):
        P[nm] = jnp.stack([_block_diag_tiles(W[nm][j]) for j in range(n_ml)]).astype(BF16)
    wif = jnp.concatenate([W["w_ml_igate"], W["w_ml_fgate"]], axis=-1)
    P["w_ml_if"] = jnp.pad(wif, ((0, 0), (0, 0), (0, 128 - wif.shape[-1]))).astype(BF16)
    bif = jnp.concatenate([W["b_ml_igate"], W["b_ml_fgate"]], axis=-1)
    P["b_ml_if"] = jnp.pad(bif, ((0, 0), (0, 128 - bif.shape[-1])))
    P["w_ml_down"] = W["w_ml_down"].astype(BF16)
    P["w_peer_qt"] = jnp.swapaxes(W["w_peer_query"], 1, 2).astype(BF16)
    keys = W["peer_keys"]
    P["peer_keys"] = jnp.swapaxes(keys, 1, 2).reshape(depth, 2 * PEER_HEADS, PEER_NKEYS, PEER_HALF).astype(BF16)
    P["peer_u"] = (W["peer_u"] * GELU_ARG_SCALE).astype(BF16)
    P["peer_vt"] = jnp.swapaxes(W["peer_v"], 1, 2).astype(BF16)
    P["w_ple_gate"] = W["w_ple_gate"].astype(BF16)
    P["w_ple_proj"] = W["w_ple_proj"].astype(BF16)
    del n_gla
    return P


def _run_group(x, p, s_gla, c_ml, n_ml, m_ml, buf_ml, W, P, depth):
    bsz, seqlen, d = x.shape
    n = bsz * seqlen
    h = x.reshape(n, d)
    dk, dv = P["gla_dims"]
    new_s, new_c, new_n, new_m, new_buf = [], [], [], [], []
    for i in range(depth):
        j = i // 2
        if i % 2 == 0:
            q, k, v, r, lg = _gla_pre(h, _row(W["norm_mix"][i]), P["w_gla_qkvr"][j], P["w_gla_gl"][j],
                                      P["w_gla_gate"][j], _row(W["b_gla_gate"][j]), dk, dv)
            s0 = None if s_gla is None else s_gla[j]
            o, s_fin = _gla_scan(q, k, v, lg, _row(W["g_gla_norm"][j]), s0, bsz, seqlen)
            new_s.append(s_fin)
            h1, xt, qt = _gla_post(o, r, h, P["w_gla_out"][j], _row(W["norm_ffn"][i]), P["w_peer_qt"][i])
        else:
            xm, z = _ml_pre(h, _row(W["norm_mix"][i]), P["w_ml_up"][j])
            di = xm.shape[1]
            if buf_ml is None:
                buf = jnp.zeros((bsz, HALO, di), F32)
            else:
                buf = jnp.pad(buf_ml[j], ((0, 0), (HALO - (ML_CONV - 1), 0), (0, 0)))
            qm, km, vm, xc, gates = _ml_qkv(xm, buf, W["ml_conv_w"][j], _row(W["ml_conv_b"][j]),
                                            P["w_ml_q"][j], P["w_ml_k"][j], P["w_ml_v"][j],
                                            P["w_ml_if"][j], _row(P["b_ml_if"][j]), bsz, seqlen)
            if c_ml is None:
                state = None
            else:
                state = (c_ml[j], n_ml[j][:, :, None, :],
                         jnp.broadcast_to(m_ml[j][:, :, None, None], m_ml[j].shape + (1, M_LANES)))
            hn, c_fin, n_fin, m_fin = _ml_scan(qm, km, vm, gates, state, bsz, seqlen)
            new_c.append(c_fin)
            new_n.append(n_fin[:, :, 0, :])
            new_m.append(m_fin[:, :, 0, 0])
            xm3 = xm.reshape(bsz, seqlen, di)
            if seqlen >= ML_CONV - 1:
                new_buf.append(xm3[:, seqlen - (ML_CONV - 1):, :])
            else:
                new_buf.append(jnp.concatenate([buf[:, HALO - (ML_CONV - 1):, :], xm3], axis=1)[:, -(ML_CONV - 1):, :])
            h1, xt, qt = _ml_post(hn, xc, z, h, _row(W["g_ml_norm"][j]), _row(W["ml_skip"][j]),
                                  P["w_ml_down"][j], _row(W["norm_ffn"][i]), P["w_peer_qt"][i])
        r2, e2, nsel, e1 = _peer_route(qt, P["peer_keys"][i])
        out_t = _peer_expert(xt, r2, e2, nsel, e1, P["peer_u"][i], P["peer_vt"][i])
        g_final = _row(W["norm_final"]) if i == depth - 1 else None
        h = _ple(out_t, h1, p[i].reshape(n, -1), _row(W["norm_ple"][i]), P["w_ple_gate"][i],
                 P["w_ple_proj"][i], g_final)
    y = h
    return (y.reshape(bsz, seqlen, d), jnp.stack(new_s), jnp.stack(new_c), jnp.stack(new_n),
            jnp.stack(new_m), jnp.stack(new_buf))


def kernel(x_prompt, x_sample, state_gla_S, state_mlstm_C, state_mlstm_n, state_mlstm_m, state_mlstm_conv,
           p_prompt, p_sample, w_gla_in, w_gla_gate, b_gla_gate, g_gla_norm, w_gla_out,
           w_ml_up, ml_conv_w, ml_conv_b, w_ml_q, w_ml_k, w_ml_v, w_ml_igate, b_ml_igate,
           w_ml_fgate, b_ml_fgate, g_ml_norm, ml_skip, w_ml_down,
           w_peer_query, peer_keys, peer_u, peer_v, norm_mix, norm_ffn, norm_ple,
           w_ple_gate, w_ple_proj, norm_final):
    W = dict(w_gla_in=w_gla_in, w_gla_gate=w_gla_gate, b_gla_gate=b_gla_gate, g_gla_norm=g_gla_norm,
             w_gla_out=w_gla_out, w_ml_up=w_ml_up, ml_conv_w=ml_conv_w, ml_conv_b=ml_conv_b,
             w_ml_q=w_ml_q, w_ml_k=w_ml_k, w_ml_v=w_ml_v, w_ml_igate=w_ml_igate, b_ml_igate=b_ml_igate,
             w_ml_fgate=w_ml_fgate, b_ml_fgate=b_ml_fgate, g_ml_norm=g_ml_norm, ml_skip=ml_skip,
             w_ml_down=w_ml_down, w_peer_query=w_peer_query, peer_keys=peer_keys, peer_u=peer_u,
             peer_v=peer_v, norm_mix=norm_mix, norm_ffn=norm_ffn, norm_ple=norm_ple,
             w_ple_gate=w_ple_gate, w_ple_proj=w_ple_proj, norm_final=norm_final)
    depth = norm_mix.shape[0]
    P = _prep_weights(W, depth)
    y_p, s_p, c_p, n_p, m_p, buf_p = _run_group(x_prompt, p_prompt, None, None, None, None, None, W, P, depth)
    y_s, s_s, c_s, n_s, m_s, buf_s = _run_group(x_sample, p_sample, state_gla_S, state_mlstm_C,
                                                state_mlstm_n, state_mlstm_m, state_mlstm_conv, W, P, depth)
    return (y_p, y_s, s_p, s_s, c_p, c_s, n_p, n_s, m_p, m_s, buf_p, buf_s)
```

```python
import functools

import jax
import jax.numpy as jnp
from jax import lax
from jax.experimental import pallas as pl
from jax.experimental.pallas import tpu as pltpu

F32 = jnp.float32
BF16 = jnp.bfloat16
EPS = 1e-6
CHUNK = 64
HIGHEST = lax.Precision.HIGHEST
NEG_INF = float("-inf")

GLA_HEADS = 4
GLA_GATE_NORMALIZER = 16.0
ML_HEADS = 4
ML_CONV = 4
PEER_HEADS = 8
PEER_NKEYS = 128
PEER_HALF = 64
PEER_TOPK = 16

VMEM_LIMIT_BYTES = 52 * 1024 * 1024
MXU_TILE = 256
PEER_EXPERT_STEP = 2048
PEER_EXPERT_PIECE = MXU_TILE
GELU_ARG_SCALE = 0.7071067811865476
GELU_OUT_SCALE = 0.5 / GELU_ARG_SCALE


def _cparams(sem):
    return pltpu.CompilerParams(dimension_semantics=sem, vmem_limit_bytes=VMEM_LIMIT_BYTES)


def _pick_block(n, candidates):
    for c in candidates:
        if n % c == 0:
            return c
    raise ValueError(f"no block size in {candidates} divides {n}")


def _rms(x, g):
    ms = jnp.mean(x * x, axis=-1, keepdims=True)
    return x * lax.rsqrt(ms + EPS) * g


def _log_sigmoid(x):
    return jnp.minimum(x, 0.0) - jnp.log1p(jnp.exp(-jnp.abs(x)))


def _silu(x):
    return x * jax.nn.sigmoid(x)


def _dot(a, b):
    return jnp.dot(a, b, preferred_element_type=F32)


def _tok(tb, d):
    return pl.BlockSpec((tb, d), lambda i: (i, 0))


def _full(shape):
    nd = len(shape)
    return pl.BlockSpec(shape, lambda *_: (0,) * nd)


def _gla_pre_kernel(h_ref, g_ref, w_ref, wgl_ref, wgate_ref, bgate_ref,
                    q_ref, k_ref, v_ref, r_ref, lg_ref, *, dk, dv, hk):
    xn = _rms(h_ref[...], g_ref[...]).astype(BF16)
    proj = _dot(xn, w_ref[...])
    q_ref[...] = proj[:, :dk] * (hk ** -0.5)
    k_ref[...] = proj[:, dk:2 * dk]
    v_ref[...] = proj[:, 2 * dk:2 * dk + dv].astype(BF16)
    r_ref[...] = proj[:, 2 * dk + dv:]
    gl = _dot(xn, wgl_ref[...])
    gate = _dot(gl.astype(BF16), wgate_ref[...]) + bgate_ref[...]
    lg_ref[...] = _log_sigmoid(gate) * (1.0 / GLA_GATE_NORMALIZER)


def _gla_pre(h, g, w_qkvr, w_gl, w_gate, b_gate, dk, dv):
    n, d = h.shape
    tb = _pick_block(n, (512, 256, 128, 64))
    hk = dk // GLA_HEADS
    outs = (jax.ShapeDtypeStruct((n, dk), F32), jax.ShapeDtypeStruct((n, dk), F32),
            jax.ShapeDtypeStruct((n, dv), BF16), jax.ShapeDtypeStruct((n, dv), F32),
            jax.ShapeDtypeStruct((n, dk), F32))
    return pl.pallas_call(
        functools.partial(_gla_pre_kernel, dk=dk, dv=dv, hk=hk),
        grid=(n // tb,),
        in_specs=[_tok(tb, d), _full(g.shape), _full(w_qkvr.shape), _full(w_gl.shape),
                  _full(w_gate.shape), _full(b_gate.shape)],
        out_specs=(_tok(tb, dk), _tok(tb, dk), _tok(tb, dv), _tok(tb, dv), _tok(tb, dk)),
        out_shape=outs,
        compiler_params=_cparams(("parallel",)),
        name="gla_pre",
    )(h, g, w_qkvr, w_gl, w_gate, b_gate)


GLA_SUB = 16


def _gla_scan_kernel(*refs, nchunks, zero_init):
    if zero_init:
        q_ref, k_ref, v_ref, lg_ref, gn_ref, o_ref, sout_ref, st_ref = refs
        s0_ref = None
    else:
        q_ref, k_ref, v_ref, lg_ref, gn_ref, s0_ref, o_ref, sout_ref, st_ref = refs
    t = pl.program_id(1)
    nheads, hv, hk = st_ref.shape

    @pl.when(t == 0)
    def _():
        if zero_init:
            st_ref[...] = jnp.zeros_like(st_ref)
        else:
            for h in range(nheads):
                st_ref[h] = s0_ref[h].T

    row = lax.broadcasted_iota(jnp.int32, (CHUNK, CHUNK), 0)
    col = lax.broadcasted_iota(jnp.int32, (CHUNK, CHUNK), 1)
    tril = (row >= col).astype(F32)
    gn = gn_ref[...]
    nsub = CHUNK // GLA_SUB

    def bmm(a, b, lhs_c, rhs_c):
        return lax.dot_general(a, b, (((lhs_c,), (rhs_c,)), ((0,), (0,))), preferred_element_type=F32)

    pairs = [(c, h) for c in range(nchunks) for h in range(nheads)]
    b_chunks = [jnp.dot(tril, lg_ref[pl.ds(c * CHUNK, CHUNK), :], precision=HIGHEST,
                        preferred_element_type=F32) for c in range(nchunks)]
    b = jnp.stack([b_chunks[c][:, h * hk:(h + 1) * hk] for c, h in pairs])
    q = jnp.stack([q_ref[pl.ds(c * CHUNK, CHUNK), h * hk:(h + 1) * hk] for c, h in pairs])
    k = jnp.stack([k_ref[pl.ds(c * CHUNK, CHUNK), h * hk:(h + 1) * hk] for c, h in pairs])
    v = jnp.stack([v_ref[pl.ds(c * CHUNK, CHUNK), h * hv:(h + 1) * hv] for c, h in pairs])
    b_last = b[:, CHUNK - 1:CHUNK, :]
    intra_parts = []
    for i in range(nsub):
        lo, hi = i * GLA_SUB, (i + 1) * GLA_SUB
        b_ref_row = b[:, lo:lo + 1, :]
        qe = (q[:, lo:hi, :] * jnp.exp(b[:, lo:hi, :] - b_ref_row)).astype(BF16)
        ke = (k[:, :hi, :] * jnp.exp(b_ref_row - b[:, :hi, :])).astype(BF16)
        att = bmm(qe, ke, 2, 2)
        r_i = lax.broadcasted_iota(jnp.int32, (GLA_SUB, hi), 0) + lo
        c_i = lax.broadcasted_iota(jnp.int32, (GLA_SUB, hi), 1)
        att = jnp.where(r_i >= c_i, att, 0.0)
        intra_parts.append(bmm(att.astype(BF16), v[:, :hi, :], 2, 1))
    intra = jnp.concatenate(intra_parts, axis=1)
    kv = bmm(v, (k * jnp.exp(b_last - b)).astype(BF16), 1, 1)
    q_dec = (q * jnp.exp(b)).astype(BF16)
    decay = jnp.exp(b_last)
    st = st_ref[...]
    for c in range(nchunks):
        ps = slice(c * nheads, (c + 1) * nheads)
        o = bmm(q_dec[ps], st.astype(BF16), 2, 2) + intra[ps]
        st = st * decay[ps] + kv[ps]
        o = o * lax.rsqrt(jnp.mean(o * o, axis=-1, keepdims=True) + EPS) * gn
        for h in range(nheads):
            o_ref[pl.ds(c * CHUNK, CHUNK), h * hv:(h + 1) * hv] = o[h]
    st_ref[...] = st

    @pl.when(t == pl.num_programs(1) - 1)
    def _():
        for h in range(nheads):
            sout_ref[h] = st_ref[h].T


def _gla_scan(q, k, v, lg, gn, s0, nseq, seqlen):
    n, dk = q.shape
    dv = v.shape[1]
    hk, hv = dk // GLA_HEADS, dv // GLA_HEADS
    tt = _pick_block(seqlen, (512, 256, 128, 64))
    nt = seqlen // tt
    zero_init = s0 is None
    tmap = lambda s, t: (s * nt + t, 0)
    smap = lambda s, t: (s, 0, 0, 0)
    in_specs = [pl.BlockSpec((tt, dk), tmap), pl.BlockSpec((tt, dk), tmap),
                pl.BlockSpec((tt, dv), tmap), pl.BlockSpec((tt, dk), tmap),
                pl.BlockSpec((1, hv), lambda s, t: (0, 0))]
    args = [q, k, v, lg, gn]
    if not zero_init:
        in_specs.append(pl.BlockSpec((None, GLA_HEADS, hk, hv), smap))
        args.append(s0)
    return pl.pallas_call(
        functools.partial(_gla_scan_kernel, nchunks=tt // CHUNK, zero_init=zero_init),
        grid=(nseq, nt),
        in_specs=in_specs,
        out_specs=(pl.BlockSpec((tt, dv), tmap), pl.BlockSpec((None, GLA_HEADS, hk, hv), smap)),
        out_shape=(jax.ShapeDtypeStruct((n, dv), F32),
                   jax.ShapeDtypeStruct((nseq, GLA_HEADS, hk, hv), F32)),
        scratch_shapes=[pltpu.VMEM((GLA_HEADS, hv, hk), F32)],
        compiler_params=_cparams(("parallel", "arbitrary")),
        name="gla_scan",
    )(*args)


def _peer_pre_tail(h1, gffn_ref, wqt_ref, xt_ref, qt_ref):
    xn = _rms(h1, gffn_ref[...])
    xt = xn.T.astype(BF16)
    xt_ref[...] = xt
    qt_ref[...] = _dot(wqt_ref[...], xt).astype(BF16)


def _gla_post_kernel(o_ref, r_ref, h_ref, wout_ref, gffn_ref, wqt_ref, h1_ref, xt_ref, qt_ref):
    y = (o_ref[...] * _silu(r_ref[...])).astype(BF16)
    h1 = h_ref[...] + _dot(y, wout_ref[...])
    h1_ref[...] = h1
    _peer_pre_tail(h1, gffn_ref, wqt_ref, xt_ref, qt_ref)


def _gla_post(o, r, h, w_out, g_ffn, w_qt):
    n, d = h.shape
    tb = _pick_block(n, (512, 256, 128))
    dq = w_qt.shape[0]
    return pl.pallas_call(
        _gla_post_kernel,
        grid=(n // tb,),
        in_specs=[_tok(tb, o.shape[1]), _tok(tb, r.shape[1]), _tok(tb, d), _full(w_out.shape),
                  _full(g_ffn.shape), _full(w_qt.shape)],
        out_specs=(_tok(tb, d), pl.BlockSpec((d, tb), lambda i: (0, i)),
                   pl.BlockSpec((dq, tb), lambda i: (0, i))),
        out_shape=(jax.ShapeDtypeStruct((n, d), F32), jax.ShapeDtypeStruct((d, n), BF16),
                   jax.ShapeDtypeStruct((dq, n), BF16)),
        compiler_params=_cparams(("parallel",)),
        name="gla_post",
    )(o, r, h, w_out, g_ffn, w_qt)


def _ml_pre_kernel(h_ref, g_ref, w_ref, xm_ref, z_ref, *, di):
    xn = _rms(h_ref[...], g_ref[...]).astype(BF16)
    up = _dot(xn, w_ref[...])
    xm_ref[...] = up[:, :di]
    z_ref[...] = up[:, di:]


def _ml_pre(h, g, w_up):
    n, d = h.shape
    di = w_up.shape[1] // 2
    tb = _pick_block(n, (512, 256, 128, 64))
    return pl.pallas_call(
        functools.partial(_ml_pre_kernel, di=di),
        grid=(n // tb,),
        in_specs=[_tok(tb, d), _full(g.shape), _full(w_up.shape)],
        out_specs=(_tok(tb, di), _tok(tb, di)),
        out_shape=(jax.ShapeDtypeStruct((n, di), F32), jax.ShapeDtypeStruct((n, di), F32)),
        compiler_params=_cparams(("parallel",)),
        name="ml_pre",
    )(h, g, w_up)


HALO = 8


def _ml_qkv_kernel(xm_ref, prev_ref, buf_ref, cw_ref, cb_ref, wq_ref, wk_ref, wv_ref, wif_ref, bif_ref,
                   q_ref, k_ref, v_ref, xc_ref, gates_ref, xp_scr, *, tt, di, hd):
    t = pl.program_id(1)
    x = xm_ref[...]
    halo = jnp.where(t == 0, buf_ref[...], prev_ref[...])
    xp_scr[0:HALO, :] = halo
    xp_scr[HALO:HALO + tt, :] = x
    y = cb_ref[...]
    for j in range(ML_CONV):
        y = y + cw_ref[j:j + 1, :] * xp_scr[pl.ds(HALO - (ML_CONV - 1) + j, tt), :]
    xc = _silu(y)
    xc_ref[...] = xc
    xcb = xc.astype(BF16)
    xmb = x.astype(BF16)
    gates = jnp.zeros((tt, gates_ref.shape[1]), F32) + bif_ref[...]
    ntile = di // MXU_TILE
    for j in range(ntile):
        cs = slice(j * MXU_TILE, (j + 1) * MXU_TILE)
        qj = _dot(xcb[:, cs], wq_ref[j])
        kj = _dot(xcb[:, cs], wk_ref[j])
        vj = _dot(xmb[:, cs], wv_ref[j])
        q_ref[:, cs] = qj.astype(BF16)
        k_ref[:, cs] = kj * (hd ** -0.5)
        v_ref[:, cs] = vj.astype(BF16)
        gates = gates + _dot(qj.astype(BF16), wif_ref[j * MXU_TILE:(j + 1) * MXU_TILE, :])
        gates = gates + _dot(kj.astype(BF16), wif_ref[di + j * MXU_TILE:di + (j + 1) * MXU_TILE, :])
        gates = gates + _dot(vj.astype(BF16), wif_ref[2 * di + j * MXU_TILE:2 * di + (j + 1) * MXU_TILE, :])
    lane = lax.broadcasted_iota(jnp.int32, gates.shape, 1)
    is_f = (lane >= ML_HEADS) & (lane < 2 * ML_HEADS)
    gates_ref[...] = jnp.where(is_f, _log_sigmoid(gates), gates)


def _ml_qkv(xm, buf, conv_w, conv_b, wq, wk, wv, wif, bif, nseq, seqlen):
    n, di = xm.shape
    hd = di // ML_HEADS
    tt = _pick_block(seqlen, (256, 128, 64))
    nt = seqlen // tt
    tmap = lambda s, t: (s * nt + t, 0)
    prev_map = lambda s, t: (jnp.maximum((s * nt + t) * (tt // HALO) - 1, 0), 0)
    ng = wif.shape[1]
    return pl.pallas_call(
        functools.partial(_ml_qkv_kernel, tt=tt, di=di, hd=hd),
        grid=(nseq, nt),
        in_specs=[pl.BlockSpec((tt, di), tmap), pl.BlockSpec((HALO, di), prev_map),
                  pl.BlockSpec((None, HALO, di), lambda s, t: (s, 0, 0)),
                  _full(conv_w.shape), _full(conv_b.shape), _full(wq.shape), _full(wk.shape),
                  _full(wv.shape), _full(wif.shape), _full(bif.shape)],
        out_specs=(pl.BlockSpec((tt, di), tmap), pl.BlockSpec((tt, di), tmap),
                   pl.BlockSpec((tt, di), tmap), pl.BlockSpec((tt, di), tmap),
                   pl.BlockSpec((tt, ng), tmap)),
        out_shape=(jax.ShapeDtypeStruct((n, di), BF16), jax.ShapeDtypeStruct((n, di), F32),
                   jax.ShapeDtypeStruct((n, di), BF16), jax.ShapeDtypeStruct((n, di), F32),
                   jax.ShapeDtypeStruct((n, ng), F32)),
        scratch_shapes=[pltpu.VMEM((HALO + tt, di), F32)],
        compiler_params=_cparams(("parallel", "arbitrary")),
        name="ml_qkv",
    )(xm, xm, buf, conv_w, conv_b, wq, wk, wv, wif, bif)


def _ml_scan_kernel(*refs, nchunks, blk, zero_init):
    if zero_init:
        q_ref, k_ref, v_ref, g_ref, h_ref, cout_ref, nout_ref, mout_ref, c_scr, n_scr, m_scr = refs
    else:
        (q_ref, k_ref, v_ref, g_ref, c0_ref, n0_ref, m0_ref,
         h_ref, cout_ref, nout_ref, mout_ref, c_scr, n_scr, m_scr) = refs
    t = pl.program_id(1)
    nheads, hd = c_scr.shape[0], c_scr.shape[1]

    @pl.when(t == 0)
    def _():
        if zero_init:
            c_scr[...] = jnp.zeros_like(c_scr)
            n_scr[...] = jnp.zeros_like(n_scr)
            m_scr[...] = jnp.zeros_like(m_scr)
        else:
            c_scr[...] = c0_ref[...]
            n_scr[...] = n0_ref[...]
            m_scr[...] = m0_ref[...]

    row = lax.broadcasted_iota(jnp.int32, (blk, blk), 0)
    col = lax.broadcasted_iota(jnp.int32, (blk, blk), 1)
    causal = row >= col
    tril = causal.astype(F32)
    lanes = g_ref.shape[1]

    def bmm(a, b, lhs_c, rhs_c, precision=None):
        return lax.dot_general(a, b, (((lhs_c,), (rhs_c,)), ((0,), (0,))), precision=precision,
                               preferred_element_type=F32)

    shared = {}
    for c, h0 in [(c, h0) for c in range(nchunks) for h0 in range(0, nheads, ML_HEAD_BATCH)]:
        heads = range(h0, h0 + ML_HEAD_BATCH)
        hsl = slice(h0, h0 + ML_HEAD_BATCH)
        sl = pl.ds(c * blk, blk)
        q = jnp.stack([q_ref[sl, h * hd:(h + 1) * hd] for h in heads])
        kf = jnp.stack([k_ref[sl, h * hd:(h + 1) * hd] for h in heads])
        v = jnp.stack([v_ref[sl, h * hd:(h + 1) * hd] for h in heads])
        cmat = c_scr[hsl]
        nvec = n_scr[hsl]
        m = m_scr[hsl][:, :, 0:1]
        if c not in shared:
            g = g_ref[sl, :]
            gcum = jnp.dot(tril, g, precision=HIGHEST, preferred_element_type=F32)
            diff_t = (g - pltpu.roll(gcum, lanes - nheads, axis=1)).T if blk % lanes == 0 else None
            shared[c] = (g, gcum, diff_t)
        g, gcum, diff_t = shared[c]
        i_col = jnp.stack([g[:, h:h + 1] for h in heads])
        f_col = jnp.stack([gcum[:, h + nheads:h + nheads + 1] for h in heads])
        if diff_t is not None:
            w_row = jnp.stack([diff_t[h:h + 1, :] for h in heads])
        else:
            eye = row == col
            w_row = bmm(jnp.ones((ML_HEAD_BATCH, blk, blk), F32), jnp.where(eye, i_col - f_col, 0.0), 2, 1,
                        precision=HIGHEST)
        log_d = jnp.where(causal, f_col + w_row, NEG_INF)
        inter = f_col + m
        mt = jnp.maximum(inter, jnp.max(log_d, axis=-1, keepdims=True))
        sc = bmm(q, kf.astype(BF16), 2, 2) * jnp.exp(log_d - mt)
        a = jnp.exp(inter - mt)
        num = a * bmm(q, cmat.astype(BF16), 2, 1) + bmm(sc.astype(BF16), v, 2, 1)
        qn = jnp.sum(q.astype(F32) * nvec, axis=-1, keepdims=True)
        den = a * qn + jnp.sum(sc, axis=-1, keepdims=True)
        hh = num / jnp.maximum(jnp.abs(den), jnp.exp(-mt))
        m_new = mt[:, blk - 1:blk, :]
        f_last = f_col[:, blk - 1:blk, :]
        w_end = jnp.exp(f_last - f_col + i_col - m_new)
        a_end = jnp.exp(f_last + m - m_new)
        kw = kf * w_end
        c_scr[hsl] = a_end * cmat + bmm(kw.astype(BF16), v, 1, 1)
        n_scr[hsl] = a_end * nvec + jnp.sum(kw, axis=1, keepdims=True)
        m_scr[hsl] = jnp.broadcast_to(m_new, (ML_HEAD_BATCH, 1, M_LANES))
        mu = jnp.mean(hh, axis=-1, keepdims=True)
        var = jnp.mean(jnp.square(hh - mu), axis=-1, keepdims=True)
        hn = (hh - mu) * lax.rsqrt(var + EPS)
        for i, h in enumerate(heads):
            h_ref[sl, h * hd:(h + 1) * hd] = hn[i]

    @pl.when(t == pl.num_programs(1) - 1)
    def _():
        cout_ref[...] = c_scr[...]
        nout_ref[...] = n_scr[...]
        mout_ref[...] = m_scr[...]


M_LANES = 128
ML_SCAN_BLOCK = 256
ML_HEAD_BATCH = 2


def _ml_scan(q, k, v, gates, state, nseq, seqlen):
    n, di = q.shape
    hd = di // ML_HEADS
    tt = _pick_block(seqlen, (256, 128, 64))
    blk = min(tt, ML_SCAN_BLOCK)
    nt = seqlen // tt
    zero_init = state is None
    tmap = lambda s, t: (s * nt + t, 0)
    smap = lambda s, t: (s, 0, 0, 0)
    ng = gates.shape[1]
    nh = ML_HEADS
    in_specs = [pl.BlockSpec((tt, di), tmap), pl.BlockSpec((tt, di), tmap),
                pl.BlockSpec((tt, di), tmap), pl.BlockSpec((tt, ng), tmap)]
    args = [q, k, v, gates]
    state_specs = [pl.BlockSpec((None, nh, hd, hd), smap), pl.BlockSpec((None, nh, 1, hd), smap),
                   pl.BlockSpec((None, nh, 1, M_LANES), smap)]
    if not zero_init:
        in_specs += state_specs
        args += list(state)
    return pl.pallas_call(
        functools.partial(_ml_scan_kernel, nchunks=tt // blk, blk=blk, zero_init=zero_init),
        grid=(nseq, nt),
        in_specs=in_specs,
        out_specs=(pl.BlockSpec((tt, di), tmap), *state_specs),
        out_shape=(jax.ShapeDtypeStruct((n, di), F32),
                   jax.ShapeDtypeStruct((nseq, nh, hd, hd), F32),
                   jax.ShapeDtypeStruct((nseq, nh, 1, hd), F32),
                   jax.ShapeDtypeStruct((nseq, nh, 1, M_LANES), F32)),
        scratch_shapes=[pltpu.VMEM((nh, hd, hd), F32), pltpu.VMEM((nh, 1, hd), F32),
                        pltpu.VMEM((nh, 1, M_LANES), F32)],
        compiler_params=_cparams(("parallel", "arbitrary")),
        name="ml_scan",
    )(*args)


def _ml_post_kernel(hn_ref, xc_ref, z_ref, h_ref, gn_ref, skip_ref, wdown_ref, gffn_ref, wqt_ref,
                    h1_ref, xt_ref, qt_ref):
    y = (hn_ref[...] * gn_ref[...] + skip_ref[...] * xc_ref[...]) * _silu(z_ref[...])
    h1 = h_ref[...] + _dot(y.astype(BF16), wdown_ref[...])
    h1_ref[...] = h1
    _peer_pre_tail(h1, gffn_ref, wqt_ref, xt_ref, qt_ref)


def _ml_post(hn, xc, z, h, g_norm, skip, w_down, g_ffn, w_qt):
    n, d = h.shape
    di = hn.shape[1]
    tb = _pick_block(n, (512, 256, 128))
    dq = w_qt.shape[0]
    return pl.pallas_call(
        _ml_post_kernel,
        grid=(n // tb,),
        in_specs=[_tok(tb, di), _tok(tb, di), _tok(tb, di), _tok(tb, d), _full(g_norm.shape),
                  _full(skip.shape), _full(w_down.shape), _full(g_ffn.shape), _full(w_qt.shape)],
        out_specs=(_tok(tb, d), pl.BlockSpec((d, tb), lambda i: (0, i)),
                   pl.BlockSpec((dq, tb), lambda i: (0, i))),
        out_shape=(jax.ShapeDtypeStruct((n, d), F32), jax.ShapeDtypeStruct((d, n), BF16),
                   jax.ShapeDtypeStruct((dq, n), BF16)),
        compiler_params=_cparams(("parallel",)),
        name="ml_post",
    )(hn, xc, z, h, g_norm, skip, w_down, g_ffn, w_qt)


NCAND_ROWS = 72


def _first_max(vals, ids, big):
    m = jnp.max(vals, axis=0, keepdims=True)
    sel = jnp.min(jnp.where(vals == m, ids, big), axis=0, keepdims=True)
    return m, sel


def _sorting_network(n):
    size = 1
    while size < n:
        size *= 2
    pairs = []
    p = 1
    while p < size:
        k = p
        while k >= 1:
            for j in range(k % p, size - k, 2 * k):
                for i in range(min(k, size - j - k)):
                    if (i + j) // (2 * p) == (i + j + k) // (2 * p):
                        pairs.append((i + j, i + j + k))
            k //= 2
        p *= 2
    return [(i, j) for i, j in pairs if j < n]


SUBLANES = 8
PACKED_SUBLANES = 16


def _sorted_best16(s):
    k16 = PEER_TOPK
    rows, lanes = s.shape
    nslab = rows // SUBLANES
    v = [s[SUBLANES * j:SUBLANES * (j + 1), :] for j in range(nslab)]
    for i, j in _sorting_network(nslab):
        v[i], v[j] = jnp.maximum(v[i], v[j]), jnp.minimum(v[i], v[j])
    v = v[:k16] + [jnp.full((SUBLANES, lanes), NEG_INF, F32)] * (k16 - nslab)
    for shift in (1, 2, 4):
        m = [jnp.maximum(v[i], pltpu.roll(v[k16 - 1 - i], shift, axis=0)) for i in range(k16)]
        d = k16 // 2
        while d >= 1:
            for i in range(k16):
                if (i // d) % 2 == 0:
                    m[i], m[i + d] = jnp.maximum(m[i], m[i + d]), jnp.minimum(m[i], m[i + d])
            d //= 2
        v = m
    return v


def _tie_flag(tops, member):
    tie = jnp.zeros_like(tops[0])
    for r in range(len(tops) - 1):
        tie = jnp.maximum(tie, (tops[r] == tops[r + 1]).astype(F32))
    count = jnp.sum(member.astype(F32), axis=0, keepdims=True)
    return jnp.maximum(jnp.max(tie, axis=0, keepdims=True), (count != float(PEER_TOPK)).astype(F32))


def _peer_route_kernel(qt_ref, keys_ref, r2_ref, e2_ref, n_ref, e1_ref, s_scr, rank_scr, top_scr, *, tk):
    nk, k16 = PEER_NKEYS, PEER_TOPK
    key_id = lax.broadcasted_iota(jnp.int32, (nk, tk), 0)
    slot_id = lax.broadcasted_iota(jnp.int32, (k16, tk), 0)

    def half_body(hc, carry):
        start = pl.multiple_of(hc * PEER_HALF, PEER_HALF)
        s = _dot(keys_ref[hc], qt_ref[pl.ds(start, PEER_HALF), :])
        s_scr[hc] = s

        tops = _sorted_best16(s)
        s3 = s.reshape(nk // SUBLANES, SUBLANES, tk)
        member = s3 >= tops[k16 - 1][None]
        tie = _tie_flag(tops, member.reshape(nk, tk))

        def by_value():
            rank = jnp.where(member, float(k16 - 1), float(k16))
            for r in range(k16 - 2, -1, -1):
                rank = jnp.where(s3 >= tops[r][None], float(r), rank)
            sub = lax.broadcasted_iota(jnp.int32, (SUBLANES, tk), 0)
            halves = []
            for base in range(0, k16, SUBLANES):
                slab = tops[base]
                for r in range(1, SUBLANES):
                    slab = jnp.where(sub == r, tops[base + r], slab)
                halves.append(slab)
            return rank.reshape(nk, tk), jnp.concatenate(halves, axis=0)

        def by_value_and_index():
            def pick(r, st):
                s_cur, rank, top = st
                m, sel = _first_max(s_cur, key_id, nk)
                hit = key_id == sel
                return (jnp.where(hit, NEG_INF, s_cur), jnp.where(hit, jnp.asarray(r, F32), rank),
                        jnp.where(slot_id == r, m, top))

            _, rank, top = lax.fori_loop(
                0, k16, pick, (s, jnp.full((nk, tk), float(k16), F32), jnp.zeros((k16, tk), F32)))
            return rank, top

        rank, top = lax.cond(jnp.max(tie) > 0.0, by_value_and_index, by_value)
        rank_scr[hc] = rank
        top_scr[hc] = top
        return carry

    lax.fori_loop(0, 2 * PEER_HEADS, half_body, 0)

    def cand_ids():
        i16 = lax.broadcasted_iota(jnp.int32, (16, tk), 0)
        i8 = lax.broadcasted_iota(jnp.int32, (8, tk), 0)
        parts = [i16, 16 + i8, 32 + i8, 48 + i8, 16 * i16, 16 * i8 + 1, 16 * i8 + 2]
        return jnp.concatenate(parts, axis=0)

    cand_id = cand_ids()
    row72 = lax.broadcasted_iota(jnp.int32, (NCAND_ROWS, tk), 0)
    cand_ok = (row72 < 40) | (cand_id >= 64)
    big_id = 16 * 16

    def head_body(h, carry):
        a = top_scr[2 * h]
        b = top_scr[2 * h + 1]
        parts = [a[0:1, :] + b, a[1:2, :] + b[0:8, :], a[2:3, :] + b[0:8, :], a[3:4, :] + b[0:8, :],
                 a + b[0:1, :], a[0:8, :] + b[1:2, :], a[0:8, :] + b[2:3, :]]
        cand = jnp.where(cand_ok, jnp.concatenate(parts, axis=0), NEG_INF)
        tops = _sorted_best16(cand)
        cand3 = cand.reshape(NCAND_ROWS // SUBLANES, SUBLANES, tk)
        sel = (cand3 >= tops[k16 - 1][None]).reshape(NCAND_ROWS, tk)
        tie = _tie_flag(tops, sel)

        def by_value():
            self = sel.astype(F32)
            cnt = self[40:56, :] + jnp.concatenate(
                [self[56:64, :] + self[64:72, :], jnp.zeros((8, tk), F32)], axis=0)
            for r1, (lo, hi) in enumerate(((0, 16), (16, 24), (24, 32), (32, 40))):
                row = jnp.sum(self[lo:hi, :], axis=0, keepdims=True)
                cnt = cnt + jnp.where(slot_id == r1, row, 0.0)
            ex = jnp.exp(cand3 - tops[0][None]).reshape(NCAND_ROWS, tk)
            z = jnp.sum(jnp.where(sel, ex, 0.0), axis=0, keepdims=True)
            return cnt, z

        def by_value_and_index():
            def pick(r, st):
                cand_cur, cnt, tsel = st
                m, sel = _first_max(cand_cur, cand_id, big_id)
                cand_cur = jnp.where(cand_id == sel, NEG_INF, cand_cur)
                cnt = cnt + (slot_id == (sel >> 4)).astype(F32)
                return cand_cur, cnt, jnp.where(slot_id == r, m, tsel)

            _, cnt, tsel = lax.fori_loop(
                0, k16, pick, (cand, jnp.zeros((k16, tk), F32), jnp.zeros((k16, tk), F32)))
            return cnt, jnp.sum(jnp.exp(tsel - tsel[0:1, :]), axis=0, keepdims=True)

        cnt, z = lax.cond(jnp.max(tie) > 0.0, by_value_and_index, by_value)
        rank1 = rank_scr[2 * h]
        rank2 = rank_scr[2 * h + 1]
        pk = PACKED_SUBLANES
        rank1b = rank1.astype(BF16).reshape(nk // pk, pk, tk)
        n_sel = jnp.zeros((nk // pk, pk, tk), BF16)
        for r in range(k16):
            row = jnp.broadcast_to(cnt[r:r + 1, :], (pk, tk)).astype(BF16)
            n_sel = jnp.where(rank1b == r, row[None], n_sel)
        n_ref[h] = n_sel.reshape(nk, tk).astype(F32)
        e1_ref[h] = jnp.where(rank1 < k16, jnp.exp(s_scr[2 * h] - a[0:1, :]) * (GELU_OUT_SCALE / z), 0.0)
        r2_ref[h] = rank2.astype(BF16)
        e2_ref[h] = jnp.where(rank2 < k16, jnp.exp(s_scr[2 * h + 1] - b[0:1, :]), 0.0).astype(BF16)
        return carry

    lax.fori_loop(0, PEER_HEADS, head_body, 0)


def _peer_route(qt, keys):
    dq, n = qt.shape
    tk = _pick_block(n, (512, 256, 128))
    nk = PEER_NKEYS
    shp = jax.ShapeDtypeStruct((PEER_HEADS, nk, n), F32)
    shp16 = jax.ShapeDtypeStruct((PEER_HEADS, nk, n), BF16)
    ospec = pl.BlockSpec((PEER_HEADS, nk, tk), lambda i: (0, 0, i))
    return pl.pallas_call(
        functools.partial(_peer_route_kernel, tk=tk),
        grid=(n // tk,),
        in_specs=[pl.BlockSpec((dq, tk), lambda i: (0, i)), _full(keys.shape)],
        out_specs=(ospec, ospec, ospec, ospec),
        out_shape=(shp16, shp16, shp, shp),
        scratch_shapes=[pltpu.VMEM((2 * PEER_HEADS, nk, tk), F32), pltpu.VMEM((2 * PEER_HEADS, nk, tk), F32),
                        pltpu.VMEM((2 * PEER_HEADS, PEER_TOPK, tk), F32)],
        compiler_params=_cparams(("parallel",)),
        name="peer_route",
    )(qt, keys)


def _peer_gate_stage(row0, act_ref, coef_ref, r2_ref, e2_ref, n_ref, e1_ref):
    nk = PEER_NKEYS
    t = act_ref.shape[1]
    pk = PACKED_SUBLANES
    zero = jnp.zeros((), BF16)
    for al in range(PEER_EXPERT_PIECE // nk):
        a = row0 + al
        rows = slice(al * nk, (al + 1) * nk)
        y = act_ref[rows, :]
        ge = (y * (1.0 + lax.erf(y))).astype(BF16).reshape(nk // pk, pk, t)
        g = None
        for h in range(PEER_HEADS):
            n16 = jnp.broadcast_to(n_ref[h, a:a + 1, :], (pk, t)).astype(BF16)
            e16 = jnp.broadcast_to(e1_ref[h, a:a + 1, :], (pk, t)).astype(BF16)
            r2h = r2_ref[h].reshape(nk // pk, pk, t)
            e2h = e2_ref[h].reshape(nk // pk, pk, t)
            term = jnp.where(r2h < n16[None], e2h, zero) * e16[None]
            g = term if g is None else g + term
        coef_ref[rows, :] = (g * ge).reshape(nk, t)


def _peer_expert_kernel(xt_ref, r2_ref, e2_ref, n_ref, e1_ref, u_ref, vt_ref, out_ref, act_scr, coef_scr):
    k = pl.program_id(1)
    ep = PEER_EXPERT_PIECE
    npiece = act_scr.shape[0]
    rows = ep // PEER_NKEYS

    @pl.when(k == 0)
    def _():
        out_ref[...] = jnp.zeros_like(out_ref)

    xt = xt_ref[...]
    for p in range(npiece):
        act_scr[p] = _dot(u_ref[p * ep:(p + 1) * ep, :], xt)
    for p in range(npiece):
        _peer_gate_stage(p * rows, act_scr.at[p], coef_scr.at[p], r2_ref, e2_ref, n_ref, e1_ref)
    for p in range(npiece):
        out_ref[...] += _dot(vt_ref[:, p * ep:(p + 1) * ep], coef_scr[p])


def _peer_expert(xt, r2, e2, nsel, e1, u, vt):
    d, n = xt.shape
    ne = u.shape[0]
    t = _pick_block(n, (512, 256, 128))
    et = PEER_EXPERT_STEP
    npiece = et // PEER_EXPERT_PIECE
    rspec = pl.BlockSpec((PEER_HEADS, PEER_NKEYS, t), lambda i, k: (0, 0, i))
    row_spec = pl.BlockSpec((PEER_HEADS, et // PEER_NKEYS, t), lambda i, k: (0, k, i))
    return pl.pallas_call(
        _peer_expert_kernel,
        grid=(n // t, ne // et),
        in_specs=[pl.BlockSpec((d, t), lambda i, k: (0, i)), rspec, rspec, row_spec, row_spec,
                  pl.BlockSpec((et, d), lambda i, k: (k, 0)),
                  pl.BlockSpec((d, et), lambda i, k: (0, k))],
        out_specs=pl.BlockSpec((d, t), lambda i, k: (0, i)),
        out_shape=jax.ShapeDtypeStruct((d, n), F32),
        scratch_shapes=[pltpu.VMEM((npiece, PEER_EXPERT_PIECE, t), F32),
                        pltpu.VMEM((npiece, PEER_EXPERT_PIECE, t), BF16)],
        compiler_params=_cparams(("parallel", "arbitrary")),
        name="peer_expert",
    )(xt, r2, e2, nsel, e1, u, vt)


def _ple_kernel(ot_ref, h1_ref, p_ref, gple_ref, wgate_ref, wproj_ref, *rest, final):
    h2 = h1_ref[...] + ot_ref[...].T
    gate = jax.nn.sigmoid(_dot(_rms(h2, gple_ref[...]).astype(BF16), wgate_ref[...]))
    h3 = h2 + gate * _dot(p_ref[...].astype(BF16), wproj_ref[...])
    if final:
        gfinal_ref, y_ref = rest
        y_ref[...] = _rms(h3, gfinal_ref[...])
    else:
        (h_ref,) = rest
        h_ref[...] = h3


def _ple(out_t, h1, p, g_ple, w_gate, w_proj, g_final=None):
    n, d = h1.shape
    tb = _pick_block(n, (512, 256, 128))
    final = g_final is not None
    in_specs = [pl.BlockSpec((d, tb), lambda i: (0, i)), _tok(tb, d), _tok(tb, p.shape[1]),
                _full(g_ple.shape), _full(w_gate.shape), _full(w_proj.shape)]
    args = [out_t, h1, p, g_ple, w_gate, w_proj]
    if final:
        in_specs.append(_full(g_final.shape))
        args.append(g_final)
    return pl.pallas_call(
        functools.partial(_ple_kernel, final=final),
        grid=(n // tb,),
        in_specs=in_specs,
        out_specs=_tok(tb, d),
        out_shape=jax.ShapeDtypeStruct((n, d), F32),
        compiler_params=_cparams(("parallel",)),
        name="ple",
    )(*args)


def _block_diag_tiles(w):
    g, bi, bo = w.shape
    per = MXU_TILE // bi
    wt = w.reshape(g // per, per, bi, bo)
    eye = jnp.eye(per, dtype=w.dtype)
    dense = jnp.einsum("tgio,gh->tgiho", wt, eye)
    return dense.reshape(g // per, per * bi, per * bo)


def _row(x):
    return x.reshape(1, -1)


def _prep_weights(W, depth):
    P = {}
    n_gla = W["w_gla_in"].shape[0]
    dk = W["w_gla_gate"].shape[2]
    rank = W["w_gla_gate"].shape[1]
    dv = (W["w_gla_in"].shape[2] - rank - 2 * dk) // 2
    P["gla_dims"] = (dk, dv)
    pad = 128 - rank
    P["w_gla_qkvr"] = W["w_gla_in"][:, :, :2 * dk + 2 * dv].astype(BF16)
    P["w_gla_gl"] = jnp.pad(W["w_gla_in"][:, :, 2 * dk + 2 * dv:], ((0, 0), (0, 0), (0, pad))).astype(BF16)
    P["w_gla_gate"] = jnp.pad(W["w_gla_gate"], ((0, 0), (0, pad), (0, 0))).astype(BF16)
    P["w_gla_out"] = W["w_gla_out"].astype(BF16)
    P["w_ml_up"] = W["w_ml_up"].astype(BF16)
    n_ml = W["w_ml_up"].shape[0]
    for nm in ("w_ml_q", "w_ml_k", "w_ml_v"):
        P[nm] = jnp.stack([_block_diag_tiles(W[nm][j]) for j in range(n_ml)]).astype(BF16)
    wif = jnp.concatenate([W["w_ml_igate"], W["w_ml_fgate"]], axis=-1)
    P["w_ml_if"] = jnp.pad(wif, ((0, 0), (0, 0), (0, 128 - wif.shape[-1]))).astype(BF16)
    bif = jnp.concatenate([W["b_ml_igate"], W["b_ml_fgate"]], axis=-1)
    P["b_ml_if"] = jnp.pad(bif, ((0, 0), (0, 128 - bif.shape[-1])))
    P["w_ml_down"] = W["w_ml_down"].astype(BF16)
    P["w_peer_qt"] = jnp.swapaxes(W["w_peer_query"], 1, 2).astype(BF16)
    keys = W["peer_keys"]
    P["peer_keys"] = jnp.swapaxes(keys, 1, 2).reshape(depth, 2 * PEER_HEADS, PEER_NKEYS, PEER_HALF).astype(BF16)
    P["peer_u"] = (W["peer_u"] * GELU_ARG_SCALE).astype(BF16)
    P["peer_vt"] = jnp.swapaxes(W["peer_v"], 1, 2).astype(BF16)
    P["w_ple_gate"] = W["w_ple_gate"].astype(BF16)
    P["w_ple_proj"] = W["w_ple_proj"].astype(BF16)
    del n_gla
    return P


def _run_group(x, p, s_gla, c_ml, n_ml, m_ml, buf_ml, W, P, depth):
    bsz, seqlen, d = x.shape
    n = bsz * seqlen
    h = x.reshape(n, d)
    dk, dv = P["gla_dims"]
    new_s, new_c, new_n, new_m, new_buf = [], [], [], [], []
    for i in range(depth):
        j = i // 2
        if i % 2 == 0:
            q, k, v, r, lg = _gla_pre(h, _row(W["norm_mix"][i]), P["w_gla_qkvr"][j], P["w_gla_gl"][j],
                                      P["w_gla_gate"][j], _row(W["b_gla_gate"][j]), dk, dv)
            s0 = None if s_gla is None else s_gla[j]
            o, s_fin = _gla_scan(q, k, v, lg, _row(W["g_gla_norm"][j]), s0, bsz, seqlen)
            new_s.append(s_fin)
            h1, xt, qt = _gla_post(o, r, h, P["w_gla_out"][j], _row(W["norm_ffn"][i]), P["w_peer_qt"][i])
        else:
            xm, z = _ml_pre(h, _row(W["norm_mix"][i]), P["w_ml_up"][j])
            di = xm.shape[1]
            if buf_ml is None:
                buf = jnp.zeros((bsz, HALO, di), F32)
            else:
                buf = jnp.pad(buf_ml[j], ((0, 0), (HALO - (ML_CONV - 1), 0), (0, 0)))
            qm, km, vm, xc, gates = _ml_qkv(xm, buf, W["ml_conv_w"][j], _row(W["ml_conv_b"][j]),
                                            P["w_ml_q"][j], P["w_ml_k"][j], P["w_ml_v"][j],
                                            P["w_ml_if"][j], _row(P["b_ml_if"][j]), bsz, seqlen)
            if c_ml is None:
                state = None
            else:
                state = (c_ml[j], n_ml[j][:, :, None, :],
                         jnp.broadcast_to(m_ml[j][:, :, None, None], m_ml[j].shape + (1, M_LANES)))
            hn, c_fin, n_fin, m_fin = _ml_scan(qm, km, vm, gates, state, bsz, seqlen)
            new_c.append(c_fin)
            new_n.append(n_fin[:, :, 0, :])
            new_m.append(m_fin[:, :, 0, 0])
            xm3 = xm.reshape(bsz, seqlen, di)
            if seqlen >= ML_CONV - 1:
                new_buf.append(xm3[:, seqlen - (ML_CONV - 1):, :])
            else:
                new_buf.append(jnp.concatenate([buf[:, HALO - (ML_CONV - 1):, :], xm3], axis=1)[:, -(ML_CONV - 1):, :])
            h1, xt, qt = _ml_post(hn, xc, z, h, _row(W["g_ml_norm"][j]), _row(W["ml_skip"][j]),
                                  P["w_ml_down"][j], _row(W["norm_ffn"][i]), P["w_peer_qt"][i])
        r2, e2, nsel, e1 = _peer_route(qt, P["peer_keys"][i])
        out_t = _peer_expert(xt, r2, e2, nsel, e1, P["peer_u"][i], P["peer_vt"][i])
        g_final = _row(W["norm_final"]) if i == depth - 1 else None
        h = _ple(out_t, h1, p[i].reshape(n, -1), _row(W["norm_ple"][i]), P["w_ple_gate"][i],
                 P["w_ple_proj"][i], g_final)
    y = h
    return (y.reshape(bsz, seqlen, d), jnp.stack(new_s), jnp.stack(new_c), jnp.stack(new_n),
            jnp.stack(new_m), jnp.stack(new_buf))


def kernel(x_prompt, x_sample, state_gla_S, state_mlstm_C, state_mlstm_n, state_mlstm_m, state_mlstm_conv,
           p_prompt, p_sample, w_gla_in, w_gla_gate, b_gla_gate, g_gla_norm, w_gla_out,
           w_ml_up, ml_conv_w, ml_conv_b, w_ml_q, w_ml_k, w_ml_v, w_ml_igate, b_ml_igate,
           w_ml_fgate, b_ml_fgate, g_ml_norm, ml_skip, w_ml_down,
           w_peer_query, peer_keys, peer_u, peer_v, norm_mix, norm_ffn, norm_ple,
           w_ple_gate, w_ple_proj, norm_final):
    W = dict(w_gla_in=w_gla_in, w_gla_gate=w_gla_gate, b_gla_gate=b_gla_gate, g_gla_norm=g_gla_norm,
             w_gla_out=w_gla_out, w_ml_up=w_ml_up, ml_conv_w=ml_conv_w, ml_conv_b=ml_conv_b,
             w_ml_q=w_ml_q, w_ml_k=w_ml_k, w_ml_v=w_ml_v, w_ml_igate=w_ml_igate, b_ml_igate=b_ml_igate,
             w_ml_fgate=w_ml_fgate, b_ml_fgate=b_ml_fgate, g_ml_norm=g_ml_norm, ml_skip=ml_skip,
             w_ml_down=w_ml_down, w_peer_query=w_peer_query, peer_keys=peer_keys, peer_u=peer_u,
             peer_v=peer_v, norm_mix=norm_mix, norm_ffn=norm_ffn, norm_ple=norm_ple,
             w_ple_gate=w_ple_gate, w_ple_proj=w_ple_proj, norm_final=norm_final)
    depth = norm_mix.shape[0]
    P = _prep_weights(W, depth)
    y_p, s_p, c_p, n_p, m_p, buf_p = _run_group(x_prompt, p_prompt, None, None, None, None, None, W, P, depth)
    y_s, s_s, c_s, n_s, m_s, buf_s = _run_group(x_sample, p_sample, state_gla_S, state_mlstm_C,
                                                state_mlstm_n, state_mlstm_m, state_mlstm_conv, W, P, depth)
    return (y_p, y_s, s_p, s_s, c_p, c_s, n_p, n_s, m_p, m_s, buf_p, buf_s)
```

```python
import functools

import jax
import jax.numpy as jnp
from jax import lax
from jax.experimental import pallas as pl
from jax.experimental.pallas import tpu as pltpu

F32 = jnp.float32
BF16 = jnp.bfloat16
EPS = 1e-6
CHUNK = 64
HIGHEST = lax.Precision.HIGHEST
NEG_INF = float("-inf")

GLA_HEADS = 4
GLA_GATE_NORMALIZER = 16.0
ML_HEADS = 4
ML_CONV = 4
PEER_HEADS = 8
PEER_NKEYS = 128
PEER_HALF = 64
PEER_TOPK = 16

VMEM_LIMIT_BYTES = 52 * 1024 * 1024
MXU_TILE = 256
PEER_EXPERT_STEP = 2048
PEER_EXPERT_PIECE = MXU_TILE
GELU_ARG_SCALE = 0.7071067811865476
GELU_OUT_SCALE = 0.5 / GELU_ARG_SCALE


def _cparams(sem):
    return pltpu.CompilerParams(dimension_semantics=sem, vmem_limit_bytes=VMEM_LIMIT_BYTES)


def _pick_block(n, candidates):
    for c in candidates:
        if n % c == 0:
            return c
    raise ValueError(f"no block size in {candidates} divides {n}")


def _rms(x, g):
    ms = jnp.mean(x * x, axis=-1, keepdims=True)
    return x * lax.rsqrt(ms + EPS) * g


def _log_sigmoid(x):
    return jnp.minimum(x, 0.0) - jnp.log1p(jnp.exp(-jnp.abs(x)))


def _silu(x):
    return x * jax.nn.sigmoid(x)


def _dot(a, b):
    return jnp.dot(a, b, preferred_element_type=F32)


def _tok(tb, d):
    return pl.BlockSpec((tb, d), lambda i: (i, 0))


def _full(shape):
    nd = len(shape)
    return pl.BlockSpec(shape, lambda *_: (0,) * nd)


def _gla_pre_kernel(h_ref, g_ref, w_ref, wgl_ref, wgate_ref, bgate_ref,
                    q_ref, k_ref, v_ref, r_ref, lg_ref, *, dk, dv, hk):
    xn = _rms(h_ref[...], g_ref[...]).astype(BF16)
    proj = _dot(xn, w_ref[...])
    q_ref[...] = proj[:, :dk] * (hk ** -0.5)
    k_ref[...] = proj[:, dk:2 * dk]
    v_ref[...] = proj[:, 2 * dk:2 * dk + dv].astype(BF16)
    r_ref[...] = proj[:, 2 * dk + dv:]
    gl = _dot(xn, wgl_ref[...])
    gate = _dot(gl.astype(BF16), wgate_ref[...]) + bgate_ref[...]
    lg_ref[...] = _log_sigmoid(gate) * (1.0 / GLA_GATE_NORMALIZER)


def _gla_pre(h, g, w_qkvr, w_gl, w_gate, b_gate, dk, dv):
    n, d = h.shape
    tb = _pick_block(n, (512, 256, 128, 64))
    hk = dk // GLA_HEADS
    outs = (jax.ShapeDtypeStruct((n, dk), F32), jax.ShapeDtypeStruct((n, dk), F32),
            jax.ShapeDtypeStruct((n, dv), BF16), jax.ShapeDtypeStruct((n, dv), F32),
            jax.ShapeDtypeStruct((n, dk), F32))
    return pl.pallas_call(
        functools.partial(_gla_pre_kernel, dk=dk, dv=dv, hk=hk),
        grid=(n // tb,),
        in_specs=[_tok(tb, d), _full(g.shape), _full(w_qkvr.shape), _full(w_gl.shape),
                  _full(w_gate.shape), _full(b_gate.shape)],
        out_specs=(_tok(tb, dk), _tok(tb, dk), _tok(tb, dv), _tok(tb, dv), _tok(tb, dk)),
        out_shape=outs,
        compiler_params=_cparams(("parallel",)),
        name="gla_pre",
    )(h, g, w_qkvr, w_gl, w_gate, b_gate)


GLA_SUB = 16


def _gla_scan_kernel(*refs, nchunks, zero_init):
    if zero_init:
        q_ref, k_ref, v_ref, lg_ref, gn_ref, o_ref, sout_ref, st_ref = refs
        s0_ref = None
    else:
        q_ref, k_ref, v_ref, lg_ref, gn_ref, s0_ref, o_ref, sout_ref, st_ref = refs
    t = pl.program_id(1)
    nheads, hv, hk = st_ref.shape

    @pl.when(t == 0)
    def _():
        if zero_init:
            st_ref[...] = jnp.zeros_like(st_ref)
        else:
            for h in range(nheads):
                st_ref[h] = s0_ref[h].T

    row = lax.broadcasted_iota(jnp.int32, (CHUNK, CHUNK), 0)
    col = lax.broadcasted_iota(jnp.int32, (CHUNK, CHUNK), 1)
    tril = (row >= col).astype(F32)
    gn = gn_ref[...]
    nsub = CHUNK // GLA_SUB

    def bmm(a, b, lhs_c, rhs_c):
        return lax.dot_general(a, b, (((lhs_c,), (rhs_c,)), ((0,), (0,))), preferred_element_type=F32)

    pairs = [(c, h) for c in range(nchunks) for h in range(nheads)]
    b_chunks = [jnp.dot(tril, lg_ref[pl.ds(c * CHUNK, CHUNK), :], precision=HIGHEST,
                        preferred_element_type=F32) for c in range(nchunks)]
    b = jnp.stack([b_chunks[c][:, h * hk:(h + 1) * hk] for c, h in pairs])
    q = jnp.stack([q_ref[pl.ds(c * CHUNK, CHUNK), h * hk:(h + 1) * hk] for c, h in pairs])
    k = jnp.stack([k_ref[pl.ds(c * CHUNK, CHUNK), h * hk:(h + 1) * hk] for c, h in pairs])
    v = jnp.stack([v_ref[pl.ds(c * CHUNK, CHUNK), h * hv:(h + 1) * hv] for c, h in pairs])
    b_last = b[:, CHUNK - 1:CHUNK, :]
    intra_parts = []
    for i in range(nsub):
        lo, hi = i * GLA_SUB, (i + 1) * GLA_SUB
        b_ref_row = b[:, lo:lo + 1, :]
        qe = (q[:, lo:hi, :] * jnp.exp(b[:, lo:hi, :] - b_ref_row)).astype(BF16)
        ke = (k[:, :hi, :] * jnp.exp(b_ref_row - b[:, :hi, :])).astype(BF16)
        att = bmm(qe, ke, 2, 2)
        r_i = lax.broadcasted_iota(jnp.int32, (GLA_SUB, hi), 0) + lo
        c_i = lax.broadcasted_iota(jnp.int32, (GLA_SUB, hi), 1)
        att = jnp.where(r_i >= c_i, att, 0.0)
        intra_parts.append(bmm(att.astype(BF16), v[:, :hi, :], 2, 1))
    intra = jnp.concatenate(intra_parts, axis=1)
    kv = bmm(v, (k * jnp.exp(b_last - b)).astype(BF16), 1, 1)
    q_dec = (q * jnp.exp(b)).astype(BF16)
    decay = jnp.exp(b_last)
    st = st_ref[...]
    for c in range(nchunks):
        ps = slice(c * nheads, (c + 1) * nheads)
        o = bmm(q_dec[ps], st.astype(BF16), 2, 2) + intra[ps]
        st = st * decay[ps] + kv[ps]
        o = o * lax.rsqrt(jnp.mean(o * o, axis=-1, keepdims=True) + EPS) * gn
        for h in range(nheads):
            o_ref[pl.ds(c * CHUNK, CHUNK), h * hv:(h + 1) * hv] = o[h]
    st_ref[...] = st

    @pl.when(t == pl.num_programs(1) - 1)
    def _():
        for h in range(nheads):
            sout_ref[h] = st_ref[h].T


def _gla_scan(q, k, v, lg, gn, s0, nseq, seqlen):
    n, dk = q.shape
    dv = v.shape[1]
    hk, hv = dk // GLA_HEADS, dv // GLA_HEADS
    tt = _pick_block(seqlen, (512, 256, 128, 64))
    nt = seqlen // tt
    zero_init = s0 is None
    tmap = lambda s, t: (s * nt + t, 0)
    smap = lambda s, t: (s, 0, 0, 0)
    in_specs = [pl.BlockSpec((tt, dk), tmap), pl.BlockSpec((tt, dk), tmap),
                pl.BlockSpec((tt, dv), tmap), pl.BlockSpec((tt, dk), tmap),
                pl.BlockSpec((1, hv), lambda s, t: (0, 0))]
    args = [q, k, v, lg, gn]
    if not zero_init:
        in_specs.append(pl.BlockSpec((None, GLA_HEADS, hk, hv), smap))
        args.append(s0)
    return pl.pallas_call(
        functools.partial(_gla_scan_kernel, nchunks=tt // CHUNK, zero_init=zero_init),
        grid=(nseq, nt),
        in_specs=in_specs,
        out_specs=(pl.BlockSpec((tt, dv), tmap), pl.BlockSpec((None, GLA_HEADS, hk, hv), smap)),
        out_shape=(jax.ShapeDtypeStruct((n, dv), F32),
                   jax.ShapeDtypeStruct((nseq, GLA_HEADS, hk, hv), F32)),
        scratch_shapes=[pltpu.VMEM((GLA_HEADS, hv, hk), F32)],
        compiler_params=_cparams(("parallel", "arbitrary")),
        name="gla_scan",
    )(*args)


def _peer_pre_tail(h1, gffn_ref, wqt_ref, xt_ref, qt_ref):
    xn = _rms(h1, gffn_ref[...])
    xt = xn.T.astype(BF16)
    xt_ref[...] = xt
    qt_ref[...] = _dot(wqt_ref[...], xt).astype(BF16)


def _gla_post_kernel(o_ref, r_ref, h_ref, wout_ref, gffn_ref, wqt_ref, h1_ref, xt_ref, qt_ref):
    y = (o_ref[...] * _silu(r_ref[...])).astype(BF16)
    h1 = h_ref[...] + _dot(y, wout_ref[...])
    h1_ref[...] = h1
    _peer_pre_tail(h1, gffn_ref, wqt_ref, xt_ref, qt_ref)


def _gla_post(o, r, h, w_out, g_ffn, w_qt):
    n, d = h.shape
    tb = _pick_block(n, (512, 256, 128))
    dq = w_qt.shape[0]
    return pl.pallas_call(
        _gla_post_kernel,
        grid=(n // tb,),
        in_specs=[_tok(tb, o.shape[1]), _tok(tb, r.shape[1]), _tok(tb, d), _full(w_out.shape),
                  _full(g_ffn.shape), _full(w_qt.shape)],
        out_specs=(_tok(tb, d), pl.BlockSpec((d, tb), lambda i: (0, i)),
                   pl.BlockSpec((dq, tb), lambda i: (0, i))),
        out_shape=(jax.ShapeDtypeStruct((n, d), F32), jax.ShapeDtypeStruct((d, n), BF16),
                   jax.ShapeDtypeStruct((dq, n), BF16)),
        compiler_params=_cparams(("parallel",)),
        name="gla_post",
    )(o, r, h, w_out, g_ffn, w_qt)


HALO = 8


def _ml_in_kernel(h_ref, g_ref, wup_ref, buf_ref, cw_ref, cb_ref, wq_ref, wk_ref, wv_ref, wif_ref, bif_ref,
                  q_ref, k_ref, v_ref, xc_ref, z_ref, gates_ref, tail_ref, xp_scr, *, tt, di, hd):
    t = pl.program_id(1)

    @pl.when(t == 0)
    def _():
        xp_scr[0:HALO, :] = buf_ref[...]

    @pl.when(t > 0)
    def _():
        xp_scr[0:HALO, :] = xp_scr[tt:tt + HALO, :]

    xn = _rms(h_ref[...], g_ref[...]).astype(BF16)
    up = _dot(xn, wup_ref[...])
    x = up[:, :di]
    z_ref[...] = up[:, di:]
    xp_scr[HALO:HALO + tt, :] = x
    tail_ref[...] = x[tt - HALO:, :]
    y = cb_ref[...]
    for j in range(ML_CONV):
        y = y + cw_ref[j:j + 1, :] * xp_scr[pl.ds(HALO - (ML_CONV - 1) + j, tt), :]
    xc = _silu(y)
    xc_ref[...] = xc
    xcb = xc.astype(BF16)
    xmb = x.astype(BF16)
    gates = jnp.zeros((tt, gates_ref.shape[1]), F32) + bif_ref[...]
    ntile = di // MXU_TILE
    for j in range(ntile):
        cs = slice(j * MXU_TILE, (j + 1) * MXU_TILE)
        qj = _dot(xcb[:, cs], wq_ref[j])
        kj = _dot(xcb[:, cs], wk_ref[j])
        vj = _dot(xmb[:, cs], wv_ref[j])
        q_ref[:, cs] = qj.astype(BF16)
        k_ref[:, cs] = kj * (hd ** -0.5)
        v_ref[:, cs] = vj.astype(BF16)
        gates = gates + _dot(qj.astype(BF16), wif_ref[j * MXU_TILE:(j + 1) * MXU_TILE, :])
        gates = gates + _dot(kj.astype(BF16), wif_ref[di + j * MXU_TILE:di + (j + 1) * MXU_TILE, :])
        gates = gates + _dot(vj.astype(BF16), wif_ref[2 * di + j * MXU_TILE:2 * di + (j + 1) * MXU_TILE, :])
    lane = lax.broadcasted_iota(jnp.int32, gates.shape, 1)
    is_f = (lane >= ML_HEADS) & (lane < 2 * ML_HEADS)
    gates_ref[...] = jnp.where(is_f, _log_sigmoid(gates), gates)


def _const(shape):
    nd = len(shape)
    return pl.BlockSpec(shape, lambda *_: (0,) * nd, pipeline_mode=pl.Buffered(1))


def _ml_in(h, g, w_up, buf, conv_w, conv_b, wq, wk, wv, wif, bif, nseq, seqlen):
    n, d = h.shape
    di = w_up.shape[1] // 2
    hd = di // ML_HEADS
    tt = _pick_block(seqlen, (256, 128, 64))
    nt = seqlen // tt
    tmap = lambda s, t: (s * nt + t, 0)
    smap = lambda s, t: (s, 0, 0)
    ng = wif.shape[1]
    tok = lambda width: pl.BlockSpec((tt, width), tmap)
    return pl.pallas_call(
        functools.partial(_ml_in_kernel, tt=tt, di=di, hd=hd),
        grid=(nseq, nt),
        in_specs=[tok(d), _const(g.shape), _const(w_up.shape), pl.BlockSpec((None, HALO, di), smap),
                  _const(conv_w.shape), _const(conv_b.shape), _const(wq.shape), _const(wk.shape),
                  _const(wv.shape), _const(wif.shape), _const(bif.shape)],
        out_specs=(tok(di), tok(di), tok(di), tok(di), tok(di), tok(ng),
                   pl.BlockSpec((None, HALO, di), smap)),
        out_shape=(jax.ShapeDtypeStruct((n, di), BF16), jax.ShapeDtypeStruct((n, di), F32),
                   jax.ShapeDtypeStruct((n, di), BF16), jax.ShapeDtypeStruct((n, di), F32),
                   jax.ShapeDtypeStruct((n, di), F32), jax.ShapeDtypeStruct((n, ng), F32),
                   jax.ShapeDtypeStruct((nseq, HALO, di), F32)),
        scratch_shapes=[pltpu.VMEM((HALO + tt, di), F32)],
        compiler_params=_cparams(("parallel", "arbitrary")),
        name="ml_in",
    )(h, g, w_up, buf, conv_w, conv_b, wq, wk, wv, wif, bif)


def _ml_scan_kernel(*refs, nchunks, blk, zero_init):
    if zero_init:
        q_ref, k_ref, v_ref, g_ref, h_ref, cout_ref, nout_ref, mout_ref, c_scr, n_scr, m_scr = refs
    else:
        (q_ref, k_ref, v_ref, g_ref, c0_ref, n0_ref, m0_ref,
         h_ref, cout_ref, nout_ref, mout_ref, c_scr, n_scr, m_scr) = refs
    t = pl.program_id(1)
    nheads, hd = c_scr.shape[0], c_scr.shape[1]

    @pl.when(t == 0)
    def _():
        if zero_init:
            c_scr[...] = jnp.zeros_like(c_scr)
            n_scr[...] = jnp.zeros_like(n_scr)
            m_scr[...] = jnp.zeros_like(m_scr)
        else:
            c_scr[...] = c0_ref[...]
            n_scr[...] = n0_ref[...]
            m_scr[...] = m0_ref[...]

    row = lax.broadcasted_iota(jnp.int32, (blk, blk), 0)
    col = lax.broadcasted_iota(jnp.int32, (blk, blk), 1)
    causal = row >= col
    tril = causal.astype(F32)
    lanes = g_ref.shape[1]

    def bmm(a, b, lhs_c, rhs_c, precision=None):
        return lax.dot_general(a, b, (((lhs_c,), (rhs_c,)), ((0,), (0,))), precision=precision,
                               preferred_element_type=F32)

    shared = {}
    for c, h0 in [(c, h0) for c in range(nchunks) for h0 in range(0, nheads, ML_HEAD_BATCH)]:
        heads = range(h0, h0 + ML_HEAD_BATCH)
        hsl = slice(h0, h0 + ML_HEAD_BATCH)
        sl = pl.ds(c * blk, blk)
        q = jnp.stack([q_ref[sl, h * hd:(h + 1) * hd] for h in heads])
        kf = jnp.stack([k_ref[sl, h * hd:(h + 1) * hd] for h in heads])
        v = jnp.stack([v_ref[sl, h * hd:(h + 1) * hd] for h in heads])
        cmat = c_scr[hsl]
        nvec = n_scr[hsl]
        m = m_scr[hsl][:, :, 0:1]
        if c not in shared:
            g = g_ref[sl, :]
            gcum = jnp.dot(tril, g, precision=HIGHEST, preferred_element_type=F32)
            diff_t = (g - pltpu.roll(gcum, lanes - nheads, axis=1)).T if blk % lanes == 0 else None
            shared[c] = (g, gcum, diff_t)
        g, gcum, diff_t = shared[c]
        i_col = jnp.stack([g[:, h:h + 1] for h in heads])
        f_col = jnp.stack([gcum[:, h + nheads:h + nheads + 1] for h in heads])
        if diff_t is not None:
            w_row = jnp.stack([diff_t[h:h + 1, :] for h in heads])
        else:
            eye = row == col
            w_row = bmm(jnp.ones((ML_HEAD_BATCH, blk, blk), F32), jnp.where(eye, i_col - f_col, 0.0), 2, 1,
                        precision=HIGHEST)
        log_d = jnp.where(causal, f_col + w_row, NEG_INF)
        inter = f_col + m
        mt = jnp.maximum(inter, jnp.max(log_d, axis=-1, keepdims=True))
        sc = bmm(q, kf.astype(BF16), 2, 2) * jnp.exp(log_d - mt)
        a = jnp.exp(inter - mt)
        num = a * bmm(q, cmat.astype(BF16), 2, 1) + bmm(sc.astype(BF16), v, 2, 1)
        qn = jnp.sum(q.astype(F32) * nvec, axis=-1, keepdims=True)
        den = a * qn + jnp.sum(sc, axis=-1, keepdims=True)
        hh = num / jnp.maximum(jnp.abs(den), jnp.exp(-mt))
        m_new = mt[:, blk - 1:blk, :]
        f_last = f_col[:, blk - 1:blk, :]
        w_end = jnp.exp(f_last - f_col + i_col - m_new)
        a_end = jnp.exp(f_last + m - m_new)
        kw = kf * w_end
        c_scr[hsl] = a_end * cmat + bmm(kw.astype(BF16), v, 1, 1)
        n_scr[hsl] = a_end * nvec + jnp.sum(kw, axis=1, keepdims=True)
        m_scr[hsl] = jnp.broadcast_to(m_new, (ML_HEAD_BATCH, 1, M_LANES))
        mu = jnp.mean(hh, axis=-1, keepdims=True)
        var = jnp.mean(jnp.square(hh - mu), axis=-1, keepdims=True)
        hn = (hh - mu) * lax.rsqrt(var + EPS)
        for i, h in enumerate(heads):
            h_ref[sl, h * hd:(h + 1) * hd] = hn[i]

    @pl.when(t == pl.num_programs(1) - 1)
    def _():
        cout_ref[...] = c_scr[...]
        nout_ref[...] = n_scr[...]
        mout_ref[...] = m_scr[...]


M_LANES = 128
ML_SCAN_BLOCK = 256
ML_HEAD_BATCH = 2


def _ml_scan(q, k, v, gates, state, nseq, seqlen):
    n, di = q.shape
    hd = di // ML_HEADS
    tt = _pick_block(seqlen, (256, 128, 64))
    blk = min(tt, ML_SCAN_BLOCK)
    nt = seqlen // tt
    zero_init = state is None
    tmap = lambda s, t: (s * nt + t, 0)
    smap = lambda s, t: (s, 0, 0, 0)
    ng = gates.shape[1]
    nh = ML_HEADS
    in_specs = [pl.BlockSpec((tt, di), tmap), pl.BlockSpec((tt, di), tmap),
                pl.BlockSpec((tt, di), tmap), pl.BlockSpec((tt, ng), tmap)]
    args = [q, k, v, gates]
    state_specs = [pl.BlockSpec((None, nh, hd, hd), smap), pl.BlockSpec((None, nh, 1, hd), smap),
                   pl.BlockSpec((None, nh, 1, M_LANES), smap)]
    if not zero_init:
        in_specs += state_specs
        args += list(state)
    return pl.pallas_call(
        functools.partial(_ml_scan_kernel, nchunks=tt // blk, blk=blk, zero_init=zero_init),
        grid=(nseq, nt),
        in_specs=in_specs,
        out_specs=(pl.BlockSpec((tt, di), tmap), *state_specs),
        out_shape=(jax.ShapeDtypeStruct((n, di), F32),
                   jax.ShapeDtypeStruct((nseq, nh, hd, hd), F32),
                   jax.ShapeDtypeStruct((nseq, nh, 1, hd), F32),
                   jax.ShapeDtypeStruct((nseq, nh, 1, M_LANES), F32)),
        scratch_shapes=[pltpu.VMEM((nh, hd, hd), F32), pltpu.VMEM((nh, 1, hd), F32),
                        pltpu.VMEM((nh, 1, M_LANES), F32)],
        compiler_params=_cparams(("parallel", "arbitrary")),
        name="ml_scan",
    )(*args)


def _ml_post_kernel(hn_ref, xc_ref, z_ref, h_ref, gn_ref, skip_ref, wdown_ref, gffn_ref, wqt_ref,
                    h1_ref, xt_ref, qt_ref):
    y = (hn_ref[...] * gn_ref[...] + skip_ref[...] * xc_ref[...]) * _silu(z_ref[...])
    h1 = h_ref[...] + _dot(y.astype(BF16), wdown_ref[...])
    h1_ref[...] = h1
    _peer_pre_tail(h1, gffn_ref, wqt_ref, xt_ref, qt_ref)


def _ml_post(hn, xc, z, h, g_norm, skip, w_down, g_ffn, w_qt):
    n, d = h.shape
    di = hn.shape[1]
    tb = _pick_block(n, (512, 256, 128))
    dq = w_qt.shape[0]
    return pl.pallas_call(
        _ml_post_kernel,
        grid=(n // tb,),
        in_specs=[_tok(tb, di), _tok(tb, di), _tok(tb, di), _tok(tb, d), _full(g_norm.shape),
                  _full(skip.shape), _full(w_down.shape), _full(g_ffn.shape), _full(w_qt.shape)],
        out_specs=(_tok(tb, d), pl.BlockSpec((d, tb), lambda i: (0, i)),
                   pl.BlockSpec((dq, tb), lambda i: (0, i))),
        out_shape=(jax.ShapeDtypeStruct((n, d), F32), jax.ShapeDtypeStruct((d, n), BF16),
                   jax.ShapeDtypeStruct((dq, n), BF16)),
        compiler_params=_cparams(("parallel",)),
        name="ml_post",
    )(hn, xc, z, h, g_norm, skip, w_down, g_ffn, w_qt)


NCAND_ROWS = 72


def _first_max(vals, ids, big):
    m = jnp.max(vals, axis=0, keepdims=True)
    sel = jnp.min(jnp.where(vals == m, ids, big), axis=0, keepdims=True)
    return m, sel


def _sorting_network(n):
    size = 1
    while size < n:
        size *= 2
    pairs = []
    p = 1
    while p < size:
        k = p
        while k >= 1:
            for j in range(k % p, size - k, 2 * k):
                for i in range(min(k, size - j - k)):
                    if (i + j) // (2 * p) == (i + j + k) // (2 * p):
                        pairs.append((i + j, i + j + k))
            k //= 2
        p *= 2
    return [(i, j) for i, j in pairs if j < n]


SUBLANES = 8
PACKED_SUBLANES = 16


def _sorted_best16(s):
    k16 = PEER_TOPK
    rows, lanes = s.shape
    nslab = rows // SUBLANES
    v = [s[SUBLANES * j:SUBLANES * (j + 1), :] for j in range(nslab)]
    for i, j in _sorting_network(nslab):
        v[i], v[j] = jnp.maximum(v[i], v[j]), jnp.minimum(v[i], v[j])
    v = v[:k16] + [jnp.full((SUBLANES, lanes), NEG_INF, F32)] * (k16 - nslab)
    for shift in (1, 2, 4):
        m = [jnp.maximum(v[i], pltpu.roll(v[k16 - 1 - i], shift, axis=0)) for i in range(k16)]
        d = k16 // 2
        while d >= 1:
            for i in range(k16):
                if (i // d) % 2 == 0:
                    m[i], m[i + d] = jnp.maximum(m[i], m[i + d]), jnp.minimum(m[i], m[i + d])
            d //= 2
        v = m
    return v


def _tie_flag(tops, member):
    tie = jnp.zeros_like(tops[0])
    for r in range(len(tops) - 1):
        tie = jnp.maximum(tie, (tops[r] == tops[r + 1]).astype(F32))
    count = jnp.sum(member.astype(F32), axis=0, keepdims=True)
    return jnp.maximum(jnp.max(tie, axis=0, keepdims=True), (count != float(PEER_TOPK)).astype(F32))


def _peer_route_kernel(qt_ref, keys_ref, r2_ref, e2_ref, n_ref, e1_ref, s_scr, rank_scr, top_scr, *, tk):
    nk, k16 = PEER_NKEYS, PEER_TOPK
    key_id = lax.broadcasted_iota(jnp.int32, (nk, tk), 0)
    slot_id = lax.broadcasted_iota(jnp.int32, (k16, tk), 0)

    def half_body(hc, carry):
        start = pl.multiple_of(hc * PEER_HALF, PEER_HALF)
        s = _dot(keys_ref[hc], qt_ref[pl.ds(start, PEER_HALF), :])
        s_scr[hc] = s

        tops = _sorted_best16(s)
        s3 = s.reshape(nk // SUBLANES, SUBLANES, tk)
        member = s3 >= tops[k16 - 1][None]
        tie = _tie_flag(tops, member.reshape(nk, tk))

        def by_value():
            rank = jnp.where(member, float(k16 - 1), float(k16))
            for r in range(k16 - 2, -1, -1):
                rank = jnp.where(s3 >= tops[r][None], float(r), rank)
            sub = lax.broadcasted_iota(jnp.int32, (SUBLANES, tk), 0)
            halves = []
            for base in range(0, k16, SUBLANES):
                slab = tops[base]
                for r in range(1, SUBLANES):
                    slab = jnp.where(sub == r, tops[base + r], slab)
                halves.append(slab)
            return rank.reshape(nk, tk), jnp.concatenate(halves, axis=0)

        def by_value_and_index():
            def pick(r, st):
                s_cur, rank, top = st
                m, sel = _first_max(s_cur, key_id, nk)
                hit = key_id == sel
                return (jnp.where(hit, NEG_INF, s_cur), jnp.where(hit, jnp.asarray(r, F32), rank),
                        jnp.where(slot_id == r, m, top))

            _, rank, top = lax.fori_loop(
                0, k16, pick, (s, jnp.full((nk, tk), float(k16), F32), jnp.zeros((k16, tk), F32)))
            return rank, top

        rank, top = lax.cond(jnp.max(tie) > 0.0, by_value_and_index, by_value)
        rank_scr[hc] = rank
        top_scr[hc] = top
        return carry

    lax.fori_loop(0, 2 * PEER_HEADS, half_body, 0)

    def cand_ids():
        i16 = lax.broadcasted_iota(jnp.int32, (16, tk), 0)
        i8 = lax.broadcasted_iota(jnp.int32, (8, tk), 0)
        parts = [i16, 16 + i8, 32 + i8, 48 + i8, 16 * i16, 16 * i8 + 1, 16 * i8 + 2]
        return jnp.concatenate(parts, axis=0)

    cand_id = cand_ids()
    row72 = lax.broadcasted_iota(jnp.int32, (NCAND_ROWS, tk), 0)
    cand_ok = (row72 < 40) | (cand_id >= 64)
    big_id = 16 * 16

    def head_body(h, carry):
        a = top_scr[2 * h]
        b = top_scr[2 * h + 1]
        parts = [a[0:1, :] + b, a[1:2, :] + b[0:8, :], a[2:3, :] + b[0:8, :], a[3:4, :] + b[0:8, :],
                 a + b[0:1, :], a[0:8, :] + b[1:2, :], a[0:8, :] + b[2:3, :]]
        cand = jnp.where(cand_ok, jnp.concatenate(parts, axis=0), NEG_INF)
        tops = _sorted_best16(cand)
        cand3 = cand.reshape(NCAND_ROWS // SUBLANES, SUBLANES, tk)
        sel = (cand3 >= tops[k16 - 1][None]).reshape(NCAND_ROWS, tk)
        tie = _tie_flag(tops, sel)

        def by_value():
            self = sel.astype(F32)
            cnt = self[40:56, :] + jnp.concatenate(
                [self[56:64, :] + self[64:72, :], jnp.zeros((8, tk), F32)], axis=0)
            for r1, (lo, hi) in enumerate(((0, 16), (16, 24), (24, 32), (32, 40))):
                row = jnp.sum(self[lo:hi, :], axis=0, keepdims=True)
                cnt = cnt + jnp.where(slot_id == r1, row, 0.0)
            ex = jnp.exp(cand3 - tops[0][None]).reshape(NCAND_ROWS, tk)
            z = jnp.sum(jnp.where(sel, ex, 0.0), axis=0, keepdims=True)
            return cnt, z

        def by_value_and_index():
            def pick(r, st):
                cand_cur, cnt, tsel = st
                m, sel = _first_max(cand_cur, cand_id, big_id)
                cand_cur = jnp.where(cand_id == sel, NEG_INF, cand_cur)
                cnt = cnt + (slot_id == (sel >> 4)).astype(F32)
                return cand_cur, cnt, jnp.where(slot_id == r, m, tsel)

            _, cnt, tsel = lax.fori_loop(
                0, k16, pick, (cand, jnp.zeros((k16, tk), F32), jnp.zeros((k16, tk), F32)))
            return cnt, jnp.sum(jnp.exp(tsel - tsel[0:1, :]), axis=0, keepdims=True)

        cnt, z = lax.cond(jnp.max(tie) > 0.0, by_value_and_index, by_value)
        rank1 = rank_scr[2 * h]
        rank2 = rank_scr[2 * h + 1]
        pk = PACKED_SUBLANES
        rank1b = rank1.astype(BF16).reshape(nk // pk, pk, tk)
        n_sel = jnp.zeros((nk // pk, pk, tk), BF16)
        for r in range(k16):
            row = jnp.broadcast_to(cnt[r:r + 1, :], (pk, tk)).astype(BF16)
            n_sel = jnp.where(rank1b == r, row[None], n_sel)
        n_ref[h] = n_sel.reshape(nk, tk).astype(F32)
        e1_ref[h] = jnp.where(rank1 < k16, jnp.exp(s_scr[2 * h] - a[0:1, :]) * (GELU_OUT_SCALE / z), 0.0)
        r2_ref[h] = rank2.astype(BF16)
        e2_ref[h] = jnp.where(rank2 < k16, jnp.exp(s_scr[2 * h + 1] - b[0:1, :]), 0.0).astype(BF16)
        return carry

    lax.fori_loop(0, PEER_HEADS, head_body, 0)


def _peer_route(qt, keys):
    dq, n = qt.shape
    tk = _pick_block(n, (512, 256, 128))
    nk = PEER_NKEYS
    shp = jax.ShapeDtypeStruct((PEER_HEADS, nk, n), F32)
    shp16 = jax.ShapeDtypeStruct((PEER_HEADS, nk, n), BF16)
    ospec = pl.BlockSpec((PEER_HEADS, nk, tk), lambda i: (0, 0, i))
    return pl.pallas_call(
        functools.partial(_peer_route_kernel, tk=tk),
        grid=(n // tk,),
        in_specs=[pl.BlockSpec((dq, tk), lambda i: (0, i)), _full(keys.shape)],
        out_specs=(ospec, ospec, ospec, ospec),
        out_shape=(shp16, shp16, shp, shp),
        scratch_shapes=[pltpu.VMEM((2 * PEER_HEADS, nk, tk), F32), pltpu.VMEM((2 * PEER_HEADS, nk, tk), F32),
                        pltpu.VMEM((2 * PEER_HEADS, PEER_TOPK, tk), F32)],
        compiler_params=_cparams(("parallel",)),
        name="peer_route",
    )(qt, keys)


def _peer_gate_stage(row0, act_ref, coef_ref, r2_ref, e2_ref, n_ref, e1_ref):
    nk = PEER_NKEYS
    t = act_ref.shape[1]
    pk = PACKED_SUBLANES
    zero = jnp.zeros((), BF16)
    for al in range(PEER_EXPERT_PIECE // nk):
        a = row0 + al
        rows = slice(al * nk, (al + 1) * nk)
        y = act_ref[rows, :]
        ge = (y * (1.0 + lax.erf(y))).astype(BF16).reshape(nk // pk, pk, t)
        g = None
        for h in range(PEER_HEADS):
            n16 = jnp.broadcast_to(n_ref[h, a:a + 1, :], (pk, t)).astype(BF16)
            e16 = jnp.broadcast_to(e1_ref[h, a:a + 1, :], (pk, t)).astype(BF16)
            r2h = r2_ref[h].reshape(nk // pk, pk, t)
            e2h = e2_ref[h].reshape(nk // pk, pk, t)
            term = jnp.where(r2h < n16[None], e2h, zero) * e16[None]
            g = term if g is None else g + term
        coef_ref[rows, :] = (g * ge).reshape(nk, t)


def _peer_expert_kernel(xt_ref, r2_ref, e2_ref, n_ref, e1_ref, u_ref, vt_ref, out_ref, act_scr, coef_scr):
    k = pl.program_id(1)
    ep = PEER_EXPERT_PIECE
    npiece = act_scr.shape[0]
    rows = ep // PEER_NKEYS

    @pl.when(k == 0)
    def _():
        out_ref[...] = jnp.zeros_like(out_ref)

    xt = xt_ref[...]
    for p in range(npiece):
        act_scr[p] = _dot(u_ref[p * ep:(p + 1) * ep, :], xt)
    for p in range(npiece):
        _peer_gate_stage(p * rows, act_scr.at[p], coef_scr.at[p], r2_ref, e2_ref, n_ref, e1_ref)
    for p in range(npiece):
        out_ref[...] += _dot(vt_ref[:, p * ep:(p + 1) * ep], coef_scr[p])


def _peer_expert(xt, r2, e2, nsel, e1, u, vt):
    d, n = xt.shape
    ne = u.shape[0]
    t = _pick_block(n, (512, 256, 128))
    et = PEER_EXPERT_STEP
    npiece = et // PEER_EXPERT_PIECE
    rspec = pl.BlockSpec((PEER_HEADS, PEER_NKEYS, t), lambda i, k: (0, 0, i))
    row_spec = pl.BlockSpec((PEER_HEADS, et // PEER_NKEYS, t), lambda i, k: (0, k, i))
    return pl.pallas_call(
        _peer_expert_kernel,
        grid=(n // t, ne // et),
        in_specs=[pl.BlockSpec((d, t), lambda i, k: (0, i)), rspec, rspec, row_spec, row_spec,
                  pl.BlockSpec((et, d), lambda i, k: (k, 0)),
                  pl.BlockSpec((d, et), lambda i, k: (0, k))],
        out_specs=pl.BlockSpec((d, t), lambda i, k: (0, i)),
        out_shape=jax.ShapeDtypeStruct((d, n), F32),
        scratch_shapes=[pltpu.VMEM((npiece, PEER_EXPERT_PIECE, t), F32),
                        pltpu.VMEM((npiece, PEER_EXPERT_PIECE, t), BF16)],
        compiler_params=_cparams(("parallel", "arbitrary")),
        name="peer_expert",
    )(xt, r2, e2, nsel, e1, u, vt)


def _ple_kernel(ot_ref, h1_ref, p_ref, gple_ref, wgate_ref, wproj_ref, *rest, final):
    h2 = h1_ref[...] + ot_ref[...].T
    gate = jax.nn.sigmoid(_dot(_rms(h2, gple_ref[...]).astype(BF16), wgate_ref[...]))
    h3 = h2 + gate * _dot(p_ref[...].astype(BF16), wproj_ref[...])
    if final:
        gfinal_ref, y_ref = rest
        y_ref[...] = _rms(h3, gfinal_ref[...])
    else:
        (h_ref,) = rest
        h_ref[...] = h3


def _ple(out_t, h1, p, g_ple, w_gate, w_proj, g_final=None):
    n, d = h1.shape
    tb = _pick_block(n, (512, 256, 128))
    final = g_final is not None
    in_specs = [pl.BlockSpec((d, tb), lambda i: (0, i)), _tok(tb, d), _tok(tb, p.shape[1]),
                _full(g_ple.shape), _full(w_gate.shape), _full(w_proj.shape)]
    args = [out_t, h1, p, g_ple, w_gate, w_proj]
    if final:
        in_specs.append(_full(g_final.shape))
        args.append(g_final)
    return pl.pallas_call(
        functools.partial(_ple_kernel, final=final),
        grid=(n // tb,),
        in_specs=in_specs,
        out_specs=_tok(tb, d),
        out_shape=jax.ShapeDtypeStruct((n, d), F32),
        compiler_params=_cparams(("parallel",)),
        name="ple",
    )(*args)


def _block_diag_tiles(w):
    g, bi, bo = w.shape
    per = MXU_TILE // bi
    wt = w.reshape(g // per, per, bi, bo)
    eye = jnp.eye(per, dtype=w.dtype)
    dense = jnp.einsum("tgio,gh->tgiho", wt, eye)
    return dense.reshape(g // per, per * bi, per * bo)


def _row(x):
    return x.reshape(1, -1)


def _prep_weights(W, depth):
    P = {}
    n_gla = W["w_gla_in"].shape[0]
    dk = W["w_gla_gate"].shape[2]
    rank = W["w_gla_gate"].shape[1]
    dv = (W["w_gla_in"].shape[2] - rank - 2 * dk) // 2
    P["gla_dims"] = (dk, dv)
    pad = 128 - rank
    P["w_gla_qkvr"] = W["w_gla_in"][:, :, :2 * dk + 2 * dv].astype(BF16)
    P["w_gla_gl"] = jnp.pad(W["w_gla_in"][:, :, 2 * dk + 2 * dv:], ((0, 0), (0, 0), (0, pad))).astype(BF16)
    P["w_gla_gate"] = jnp.pad(W["w_gla_gate"], ((0, 0), (0, pad), (0, 0))).astype(BF16)
    P["w_gla_out"] = W["w_gla_out"].astype(BF16)
    P["w_ml_up"] = W["w_ml_up"].astype(BF16)
    n_ml = W["w_ml_up"].shape[0]
    for nm in ("w_ml_q", "w_ml_k", "w_ml_v"):
        P[nm] = jnp.stack([_block_diag_tiles(W[nm][j]) for j in range(n_ml)]).astype(BF16)
    wif = jnp.concatenate([W["w_ml_igate"], W["w_ml_fgate"]], axis=-1)
    P["w_ml_if"] = jnp.pad(wif, ((0, 0), (0, 0), (0, 128 - wif.shape[-1]))).astype(BF16)
    bif = jnp.concatenate([W["b_ml_igate"], W["b_ml_fgate"]], axis=-1)
    P["b_ml_if"] = jnp.pad(bif, ((0, 0), (0, 128 - bif.shape[-1])))
    P["w_ml_down"] = W["w_ml_down"].astype(BF16)
    P["w_peer_qt"] = jnp.swapaxes(W["w_peer_query"], 1, 2).astype(BF16)
    keys = W["peer_keys"]
    P["peer_keys"] = jnp.swapaxes(keys, 1, 2).reshape(depth, 2 * PEER_HEADS, PEER_NKEYS, PEER_HALF).astype(BF16)
    P["peer_u"] = (W["peer_u"] * GELU_ARG_SCALE).astype(BF16)
    P["peer_vt"] = jnp.swapaxes(W["peer_v"], 1, 2).astype(BF16)
    P["w_ple_gate"] = W["w_ple_gate"].astype(BF16)
    P["w_ple_proj"] = W["w_ple_proj"].astype(BF16)
    del n_gla
    return P


def _run_group(x, p, s_gla, c_ml, n_ml, m_ml, buf_ml, W, P, depth):
    bsz, seqlen, d = x.shape
    n = bsz * seqlen
    h = x.reshape(n, d)
    dk, dv = P["gla_dims"]
    new_s, new_c, new_n, new_m, new_buf = [], [], [], [], []
    for i in range(depth):
        j = i // 2
        if i % 2 == 0:
            q, k, v, r, lg = _gla_pre(h, _row(W["norm_mix"][i]), P["w_gla_qkvr"][j], P["w_gla_gl"][j],
                                      P["w_gla_gate"][j], _row(W["b_gla_gate"][j]), dk, dv)
            s0 = None if s_gla is None else s_gla[j]
            o, s_fin = _gla_scan(q, k, v, lg, _row(W["g_gla_norm"][j]), s0, bsz, seqlen)
            new_s.append(s_fin)
            h1, xt, qt = _gla_post(o, r, h, P["w_gla_out"][j], _row(W["norm_ffn"][i]), P["w_peer_qt"][i])
        else:
            di = P["w_ml_up"][j].shape[1] // 2
            assert seqlen >= HALO, "the new conv state is read from one sequence's last rows"
            if buf_ml is None:
                buf = jnp.zeros((bsz, HALO, di), F32)
            else:
                buf = jnp.pad(buf_ml[j], ((0, 0), (HALO - (ML_CONV - 1), 0), (0, 0)))
            qm, km, vm, xc, z, gates, tail = _ml_in(
                h, _row(W["norm_mix"][i]), P["w_ml_up"][j], buf, W["ml_conv_w"][j], _row(W["ml_conv_b"][j]),
                P["w_ml_q"][j], P["w_ml_k"][j], P["w_ml_v"][j], P["w_ml_if"][j], _row(P["b_ml_if"][j]),
                bsz, seqlen)
            if c_ml is None:
                state = None
            else:
                state = (c_ml[j], n_ml[j][:, :, None, :],
                         jnp.broadcast_to(m_ml[j][:, :, None, None], m_ml[j].shape + (1, M_LANES)))
            hn, c_fin, n_fin, m_fin = _ml_scan(qm, km, vm, gates, state, bsz, seqlen)
            new_c.append(c_fin)
            new_n.append(n_fin[:, :, 0, :])
            new_m.append(m_fin[:, :, 0, 0])
            new_buf.append(tail[:, HALO - (ML_CONV - 1):, :])
            h1, xt, qt = _ml_post(hn, xc, z, h, _row(W["g_ml_norm"][j]), _row(W["ml_skip"][j]),
                                  P["w_ml_down"][j], _row(W["norm_ffn"][i]), P["w_peer_qt"][i])
        r2, e2, nsel, e1 = _peer_route(qt, P["peer_keys"][i])
        out_t = _peer_expert(xt, r2, e2, nsel, e1, P["peer_u"][i], P["peer_vt"][i])
        g_final = _row(W["norm_final"]) if i == depth - 1 else None
        h = _ple(out_t, h1, p[i].reshape(n, -1), _row(W["norm_ple"][i]), P["w_ple_gate"][i],
                 P["w_ple_proj"][i], g_final)
    y = h
    return (y.reshape(bsz, seqlen, d), jnp.stack(new_s), jnp.stack(new_c), jnp.stack(new_n),
            jnp.stack(new_m), jnp.stack(new_buf))


def kernel(x_prompt, x_sample, state_gla_S, state_mlstm_C, state_mlstm_n, state_mlstm_m, state_mlstm_conv,
           p_prompt, p_sample, w_gla_in, w_gla_gate, b_gla_gate, g_gla_norm, w_gla_out,
           w_ml_up, ml_conv_w, ml_conv_b, w_ml_q, w_ml_k, w_ml_v, w_ml_igate, b_ml_igate,
           w_ml_fgate, b_ml_fgate, g_ml_norm, ml_skip, w_ml_down,
           w_peer_query, peer_keys, peer_u, peer_v, norm_mix, norm_ffn, norm_ple,
           w_ple_gate, w_ple_proj, norm_final):
    W = dict(w_gla_in=w_gla_in, w_gla_gate=w_gla_gate, b_gla_gate=b_gla_gate, g_gla_norm=g_gla_norm,
             w_gla_out=w_gla_out, w_ml_up=w_ml_up, ml_conv_w=ml_conv_w, ml_conv_b=ml_conv_b,
             w_ml_q=w_ml_q, w_ml_k=w_ml_k, w_ml_v=w_ml_v, w_ml_igate=w_ml_igate, b_ml_igate=b_ml_igate,
             w_ml_fgate=w_ml_fgate, b_ml_fgate=b_ml_fgate, g_ml_norm=g_ml_norm, ml_skip=ml_skip,
             w_ml_down=w_ml_down, w_peer_query=w_peer_query, peer_keys=peer_keys, peer_u=peer_u,
             peer_v=peer_v, norm_mix=norm_mix, norm_ffn=norm_ffn, norm_ple=norm_ple,
             w_ple_gate=w_ple_gate, w_ple_proj=w_ple_proj, norm_final=norm_final)
    depth = norm_mix.shape[0]
    P = _prep_weights(W, depth)
    y_p, s_p, c_p, n_p, m_p, buf_p = _run_group(x_prompt, p_prompt, None, None, None, None, None, W, P, depth)
    y_s, s_s, c_s, n_s, m_s, buf_s = _run_group(x_sample, p_sample, state_gla_S, state_mlstm_C,
                                                state_mlstm_n, state_mlstm_m, state_mlstm_conv, W, P, depth)
    return (y_p, y_s, s_p, s_s, c_p, c_s, n_p, n_s, m_p, m_s, buf_p, buf_s)
```

```python
import functools

import jax
import jax.numpy as jnp
from jax import lax
from jax.experimental import pallas as pl
from jax.experimental.pallas import tpu as pltpu

F32 = jnp.float32
BF16 = jnp.bfloat16
EPS = 1e-6
CHUNK = 64
HIGHEST = lax.Precision.HIGHEST
NEG_INF = float("-inf")

GLA_HEADS = 4
GLA_GATE_NORMALIZER = 16.0
ML_HEADS = 4
ML_CONV = 4
PEER_HEADS = 8
PEER_NKEYS = 128
PEER_HALF = 64
PEER_TOPK = 16

VMEM_LIMIT_BYTES = 52 * 1024 * 1024
MXU_TILE = 256
PEER_EXPERT_STEP = 2048
PEER_EXPERT_PIECE = MXU_TILE
GELU_ARG_SCALE = 0.7071067811865476
GELU_OUT_SCALE = 0.5 / GELU_ARG_SCALE


def _cparams(sem):
    return pltpu.CompilerParams(dimension_semantics=sem, vmem_limit_bytes=VMEM_LIMIT_BYTES)


def _pick_block(n, candidates):
    for c in candidates:
        if n % c == 0:
            return c
    raise ValueError(f"no block size in {candidates} divides {n}")


def _rms(x, g):
    ms = jnp.mean(x * x, axis=-1, keepdims=True)
    return x * lax.rsqrt(ms + EPS) * g


def _log_sigmoid(x):
    return jnp.minimum(x, 0.0) - jnp.log1p(jnp.exp(-jnp.abs(x)))


def _silu(x):
    return x * jax.nn.sigmoid(x)


def _dot(a, b):
    return jnp.dot(a, b, preferred_element_type=F32)


def _tok(tb, d):
    return pl.BlockSpec((tb, d), lambda i: (i, 0))


def _full(shape):
    nd = len(shape)
    return pl.BlockSpec(shape, lambda *_: (0,) * nd)


def _gla_pre_kernel(h_ref, g_ref, w_ref, wgl_ref, wgate_ref, bgate_ref,
                    q_ref, k_ref, v_ref, r_ref, lg_ref, *, dk, dv, hk):
    xn = _rms(h_ref[...], g_ref[...]).astype(BF16)
    proj = _dot(xn, w_ref[...])
    q_ref[...] = proj[:, :dk] * (hk ** -0.5)
    k_ref[...] = proj[:, dk:2 * dk]
    v_ref[...] = proj[:, 2 * dk:2 * dk + dv].astype(BF16)
    r_ref[...] = proj[:, 2 * dk + dv:]
    gl = _dot(xn, wgl_ref[...])
    gate = _dot(gl.astype(BF16), wgate_ref[...]) + bgate_ref[...]
    lg_ref[...] = _log_sigmoid(gate) * (1.0 / GLA_GATE_NORMALIZER)


def _gla_pre(h, g, w_qkvr, w_gl, w_gate, b_gate, dk, dv):
    n, d = h.shape
    tb = _pick_block(n, (512, 256, 128, 64))
    hk = dk // GLA_HEADS
    outs = (jax.ShapeDtypeStruct((n, dk), F32), jax.ShapeDtypeStruct((n, dk), F32),
            jax.ShapeDtypeStruct((n, dv), BF16), jax.ShapeDtypeStruct((n, dv), F32),
            jax.ShapeDtypeStruct((n, dk), F32))
    return pl.pallas_call(
        functools.partial(_gla_pre_kernel, dk=dk, dv=dv, hk=hk),
        grid=(n // tb,),
        in_specs=[_tok(tb, d), _full(g.shape), _full(w_qkvr.shape), _full(w_gl.shape),
                  _full(w_gate.shape), _full(b_gate.shape)],
        out_specs=(_tok(tb, dk), _tok(tb, dk), _tok(tb, dv), _tok(tb, dv), _tok(tb, dk)),
        out_shape=outs,
        compiler_params=_cparams(("parallel",)),
        name="gla_pre",
    )(h, g, w_qkvr, w_gl, w_gate, b_gate)


GLA_SUB = 16


def _gla_scan_kernel(*refs, nchunks, zero_init):
    if zero_init:
        q_ref, k_ref, v_ref, lg_ref, r_ref, gn_ref, y_ref, sout_ref, st_ref = refs
        s0_ref = None
    else:
        q_ref, k_ref, v_ref, lg_ref, r_ref, gn_ref, s0_ref, y_ref, sout_ref, st_ref = refs
    t = pl.program_id(1)
    nheads, hv, hk = st_ref.shape

    @pl.when(t == 0)
    def _():
        if zero_init:
            st_ref[...] = jnp.zeros_like(st_ref)
        else:
            for h in range(nheads):
                st_ref[h] = s0_ref[h].T

    row = lax.broadcasted_iota(jnp.int32, (CHUNK, CHUNK), 0)
    col = lax.broadcasted_iota(jnp.int32, (CHUNK, CHUNK), 1)
    tril = (row >= col).astype(F32)
    gn = gn_ref[...]
    nsub = CHUNK // GLA_SUB

    def bmm(a, b, lhs_c, rhs_c):
        return lax.dot_general(a, b, (((lhs_c,), (rhs_c,)), ((0,), (0,))), preferred_element_type=F32)

    pairs = [(c, h) for c in range(nchunks) for h in range(nheads)]
    b_chunks = [jnp.dot(tril, lg_ref[pl.ds(c * CHUNK, CHUNK), :], precision=HIGHEST,
                        preferred_element_type=F32) for c in range(nchunks)]
    b = jnp.stack([b_chunks[c][:, h * hk:(h + 1) * hk] for c, h in pairs])
    q = jnp.stack([q_ref[pl.ds(c * CHUNK, CHUNK), h * hk:(h + 1) * hk] for c, h in pairs])
    k = jnp.stack([k_ref[pl.ds(c * CHUNK, CHUNK), h * hk:(h + 1) * hk] for c, h in pairs])
    v = jnp.stack([v_ref[pl.ds(c * CHUNK, CHUNK), h * hv:(h + 1) * hv] for c, h in pairs])
    b_last = b[:, CHUNK - 1:CHUNK, :]
    intra_parts = []
    for i in range(nsub):
        lo, hi = i * GLA_SUB, (i + 1) * GLA_SUB
        b_ref_row = b[:, lo:lo + 1, :]
        qe = (q[:, lo:hi, :] * jnp.exp(b[:, lo:hi, :] - b_ref_row)).astype(BF16)
        ke = (k[:, :hi, :] * jnp.exp(b_ref_row - b[:, :hi, :])).astype(BF16)
        att = bmm(qe, ke, 2, 2)
        r_i = lax.broadcasted_iota(jnp.int32, (GLA_SUB, hi), 0) + lo
        c_i = lax.broadcasted_iota(jnp.int32, (GLA_SUB, hi), 1)
        att = jnp.where(r_i >= c_i, att, 0.0)
        intra_parts.append(bmm(att.astype(BF16), v[:, :hi, :], 2, 1))
    intra = jnp.concatenate(intra_parts, axis=1)
    kv = bmm(v, (k * jnp.exp(b_last - b)).astype(BF16), 1, 1)
    q_dec = (q * jnp.exp(b)).astype(BF16)
    decay = jnp.exp(b_last)
    st = st_ref[...]
    for c in range(nchunks):
        ps = slice(c * nheads, (c + 1) * nheads)
        o = bmm(q_dec[ps], st.astype(BF16), 2, 2) + intra[ps]
        st = st * decay[ps] + kv[ps]
        o = o * lax.rsqrt(jnp.mean(o * o, axis=-1, keepdims=True) + EPS) * gn
        for h in range(nheads):
            cs = (pl.ds(c * CHUNK, CHUNK), slice(h * hv, (h + 1) * hv))
            y_ref[cs] = (o[h] * _silu(r_ref[cs])).astype(BF16)
    st_ref[...] = st

    @pl.when(t == pl.num_programs(1) - 1)
    def _():
        for h in range(nheads):
            sout_ref[h] = st_ref[h].T


def _gla_scan(q, k, v, lg, r, gn, s0, nseq, seqlen):
    n, dk = q.shape
    dv = v.shape[1]
    hk, hv = dk // GLA_HEADS, dv // GLA_HEADS
    tt = _pick_block(seqlen, (512, 256, 128, 64))
    nt = seqlen // tt
    zero_init = s0 is None
    tmap = lambda s, t: (s * nt + t, 0)
    smap = lambda s, t: (s, 0, 0, 0)
    in_specs = [pl.BlockSpec((tt, dk), tmap), pl.BlockSpec((tt, dk), tmap),
                pl.BlockSpec((tt, dv), tmap), pl.BlockSpec((tt, dk), tmap), pl.BlockSpec((tt, dv), tmap),
                pl.BlockSpec((1, hv), lambda s, t: (0, 0))]
    args = [q, k, v, lg, r, gn]
    if not zero_init:
        in_specs.append(pl.BlockSpec((None, GLA_HEADS, hk, hv), smap))
        args.append(s0)
    return pl.pallas_call(
        functools.partial(_gla_scan_kernel, nchunks=tt // CHUNK, zero_init=zero_init),
        grid=(nseq, nt),
        in_specs=in_specs,
        out_specs=(pl.BlockSpec((tt, dv), tmap), pl.BlockSpec((None, GLA_HEADS, hk, hv), smap)),
        out_shape=(jax.ShapeDtypeStruct((n, dv), BF16),
                   jax.ShapeDtypeStruct((nseq, GLA_HEADS, hk, hv), F32)),
        scratch_shapes=[pltpu.VMEM((GLA_HEADS, hv, hk), F32)],
        compiler_params=_cparams(("parallel", "arbitrary")),
        name="gla_scan",
    )(*args)


def _peer_pre_tail(h1, gffn_ref, wqt_ref, xt_ref, qt_ref):
    xn = _rms(h1, gffn_ref[...])
    xt = xn.T.astype(BF16)
    xt_ref[...] = xt
    qt_ref[...] = _dot(wqt_ref[...], xt).astype(BF16)


def _mix_post_kernel(y_ref, h_ref, w_ref, gffn_ref, wqt_ref, h1_ref, xt_ref, qt_ref):
    h1 = h_ref[...] + _dot(y_ref[...], w_ref[...])
    h1_ref[...] = h1
    _peer_pre_tail(h1, gffn_ref, wqt_ref, xt_ref, qt_ref)


def _mix_post(y, h, w_out, g_ffn, w_qt):
    n, d = h.shape
    tb = _pick_block(n, (512, 256, 128))
    dq = w_qt.shape[0]
    return pl.pallas_call(
        _mix_post_kernel,
        grid=(n // tb,),
        in_specs=[_tok(tb, y.shape[1]), _tok(tb, d), _full(w_out.shape), _full(g_ffn.shape), _full(w_qt.shape)],
        out_specs=(_tok(tb, d), pl.BlockSpec((d, tb), lambda i: (0, i)),
                   pl.BlockSpec((dq, tb), lambda i: (0, i))),
        out_shape=(jax.ShapeDtypeStruct((n, d), F32), jax.ShapeDtypeStruct((d, n), BF16),
                   jax.ShapeDtypeStruct((dq, n), BF16)),
        compiler_params=_cparams(("parallel",)),
        name="mix_post",
    )(y, h, w_out, g_ffn, w_qt)


HALO = 8


def _ml_in_kernel(h_ref, g_ref, wup_ref, buf_ref, cw_ref, cb_ref, wq_ref, wk_ref, wv_ref, wif_ref, bif_ref,
                  q_ref, k_ref, v_ref, xc_ref, z_ref, gates_ref, tail_ref, xp_scr, *, tt, di, hd):
    t = pl.program_id(1)

    @pl.when(t == 0)
    def _():
        xp_scr[0:HALO, :] = buf_ref[...]

    @pl.when(t > 0)
    def _():
        xp_scr[0:HALO, :] = xp_scr[tt:tt + HALO, :]

    xn = _rms(h_ref[...], g_ref[...]).astype(BF16)
    up = _dot(xn, wup_ref[...])
    x = up[:, :di]
    z_ref[...] = up[:, di:]
    xp_scr[HALO:HALO + tt, :] = x
    tail_ref[...] = x[tt - HALO:, :]
    y = cb_ref[...]
    for j in range(ML_CONV):
        y = y + cw_ref[j:j + 1, :] * xp_scr[pl.ds(HALO - (ML_CONV - 1) + j, tt), :]
    xc = _silu(y)
    xc_ref[...] = xc
    xcb = xc.astype(BF16)
    xmb = x.astype(BF16)
    gates = jnp.zeros((tt, gates_ref.shape[1]), F32) + bif_ref[...]
    ntile = di // MXU_TILE
    for j in range(ntile):
        cs = slice(j * MXU_TILE, (j + 1) * MXU_TILE)
        qj = _dot(xcb[:, cs], wq_ref[j])
        kj = _dot(xcb[:, cs], wk_ref[j])
        vj = _dot(xmb[:, cs], wv_ref[j])
        q_ref[:, cs] = qj.astype(BF16)
        k_ref[:, cs] = kj * (hd ** -0.5)
        v_ref[:, cs] = vj.astype(BF16)
        gates = gates + _dot(qj.astype(BF16), wif_ref[j * MXU_TILE:(j + 1) * MXU_TILE, :])
        gates = gates + _dot(kj.astype(BF16), wif_ref[di + j * MXU_TILE:di + (j + 1) * MXU_TILE, :])
        gates = gates + _dot(vj.astype(BF16), wif_ref[2 * di + j * MXU_TILE:2 * di + (j + 1) * MXU_TILE, :])
    lane = lax.broadcasted_iota(jnp.int32, gates.shape, 1)
    is_f = (lane >= ML_HEADS) & (lane < 2 * ML_HEADS)
    gates_ref[...] = jnp.where(is_f, _log_sigmoid(gates), gates)


def _const(shape):
    nd = len(shape)
    return pl.BlockSpec(shape, lambda *_: (0,) * nd, pipeline_mode=pl.Buffered(1))


def _ml_in(h, g, w_up, buf, conv_w, conv_b, wq, wk, wv, wif, bif, nseq, seqlen):
    n, d = h.shape
    di = w_up.shape[1] // 2
    hd = di // ML_HEADS
    tt = _pick_block(seqlen, (256, 128, 64))
    nt = seqlen // tt
    tmap = lambda s, t: (s * nt + t, 0)
    smap = lambda s, t: (s, 0, 0)
    ng = wif.shape[1]
    tok = lambda width: pl.BlockSpec((tt, width), tmap)
    return pl.pallas_call(
        functools.partial(_ml_in_kernel, tt=tt, di=di, hd=hd),
        grid=(nseq, nt),
        in_specs=[tok(d), _const(g.shape), _const(w_up.shape), pl.BlockSpec((None, HALO, di), smap),
                  _const(conv_w.shape), _const(conv_b.shape), _const(wq.shape), _const(wk.shape),
                  _const(wv.shape), _const(wif.shape), _const(bif.shape)],
        out_specs=(tok(di), tok(di), tok(di), tok(di), tok(di), tok(ng),
                   pl.BlockSpec((None, HALO, di), smap)),
        out_shape=(jax.ShapeDtypeStruct((n, di), BF16), jax.ShapeDtypeStruct((n, di), F32),
                   jax.ShapeDtypeStruct((n, di), BF16), jax.ShapeDtypeStruct((n, di), F32),
                   jax.ShapeDtypeStruct((n, di), F32), jax.ShapeDtypeStruct((n, ng), F32),
                   jax.ShapeDtypeStruct((nseq, HALO, di), F32)),
        scratch_shapes=[pltpu.VMEM((HALO + tt, di), F32)],
        compiler_params=_cparams(("parallel", "arbitrary")),
        name="ml_in",
    )(h, g, w_up, buf, conv_w, conv_b, wq, wk, wv, wif, bif)


def _ml_scan_kernel(*refs, nchunks, blk, zero_init):
    if zero_init:
        (q_ref, k_ref, v_ref, g_ref, xc_ref, z_ref, gn_ref, skip_ref,
         y_ref, cout_ref, nout_ref, mout_ref, c_scr, n_scr, m_scr) = refs
    else:
        (q_ref, k_ref, v_ref, g_ref, xc_ref, z_ref, gn_ref, skip_ref, c0_ref, n0_ref, m0_ref,
         y_ref, cout_ref, nout_ref, mout_ref, c_scr, n_scr, m_scr) = refs
    t = pl.program_id(1)
    nheads, hd = c_scr.shape[0], c_scr.shape[1]

    @pl.when(t == 0)
    def _():
        if zero_init:
            c_scr[...] = jnp.zeros_like(c_scr)
            n_scr[...] = jnp.zeros_like(n_scr)
            m_scr[...] = jnp.zeros_like(m_scr)
        else:
            c_scr[...] = c0_ref[...]
            n_scr[...] = n0_ref[...]
            m_scr[...] = m0_ref[...]

    row = lax.broadcasted_iota(jnp.int32, (blk, blk), 0)
    col = lax.broadcasted_iota(jnp.int32, (blk, blk), 1)
    causal = row >= col
    tril = causal.astype(F32)
    lanes = g_ref.shape[1]

    def bmm(a, b, lhs_c, rhs_c, precision=None):
        return lax.dot_general(a, b, (((lhs_c,), (rhs_c,)), ((0,), (0,))), precision=precision,
                               preferred_element_type=F32)

    shared = {}
    for c, h0 in [(c, h0) for c in range(nchunks) for h0 in range(0, nheads, ML_HEAD_BATCH)]:
        heads = range(h0, h0 + ML_HEAD_BATCH)
        hsl = slice(h0, h0 + ML_HEAD_BATCH)
        sl = pl.ds(c * blk, blk)
        q = jnp.stack([q_ref[sl, h * hd:(h + 1) * hd] for h in heads])
        kf = jnp.stack([k_ref[sl, h * hd:(h + 1) * hd] for h in heads])
        v = jnp.stack([v_ref[sl, h * hd:(h + 1) * hd] for h in heads])
        cmat = c_scr[hsl]
        nvec = n_scr[hsl]
        m = m_scr[hsl][:, :, 0:1]
        if c not in shared:
            g = g_ref[sl, :]
            gcum = jnp.dot(tril, g, precision=HIGHEST, preferred_element_type=F32)
            diff_t = (g - pltpu.roll(gcum, lanes - nheads, axis=1)).T if blk % lanes == 0 else None
            shared[c] = (g, gcum, diff_t)
        g, gcum, diff_t = shared[c]
        i_col = jnp.stack([g[:, h:h + 1] for h in heads])
        f_col = jnp.stack([gcum[:, h + nheads:h + nheads + 1] for h in heads])
        if diff_t is not None:
            w_row = jnp.stack([diff_t[h:h + 1, :] for h in heads])
        else:
            eye = row == col
            w_row = bmm(jnp.ones((ML_HEAD_BATCH, blk, blk), F32), jnp.where(eye, i_col - f_col, 0.0), 2, 1,
                        precision=HIGHEST)
        log_d = jnp.where(causal, f_col + w_row, NEG_INF)
        inter = f_col + m
        mt = jnp.maximum(inter, jnp.max(log_d, axis=-1, keepdims=True))
        sc = bmm(q, kf.astype(BF16), 2, 2) * jnp.exp(log_d - mt)
        a = jnp.exp(inter - mt)
        num = a * bmm(q, cmat.astype(BF16), 2, 1) + bmm(sc.astype(BF16), v, 2, 1)
        qn = jnp.sum(q.astype(F32) * nvec, axis=-1, keepdims=True)
        den = a * qn + jnp.sum(sc, axis=-1, keepdims=True)
        hh = num / jnp.maximum(jnp.abs(den), jnp.exp(-mt))
        m_new = mt[:, blk - 1:blk, :]
        f_last = f_col[:, blk - 1:blk, :]
        w_end = jnp.exp(f_last - f_col + i_col - m_new)
        a_end = jnp.exp(f_last + m - m_new)
        kw = kf * w_end
        c_scr[hsl] = a_end * cmat + bmm(kw.astype(BF16), v, 1, 1)
        n_scr[hsl] = a_end * nvec + jnp.sum(kw, axis=1, keepdims=True)
        m_scr[hsl] = jnp.broadcast_to(m_new, (ML_HEAD_BATCH, 1, M_LANES))
        mu = jnp.mean(hh, axis=-1, keepdims=True)
        var = jnp.mean(jnp.square(hh - mu), axis=-1, keepdims=True)
        hn = (hh - mu) * lax.rsqrt(var + EPS)
        for i, h in enumerate(heads):
            cs = (sl, slice(h * hd, (h + 1) * hd))
            cols = cs[1]
            y = (hn[i] * gn_ref[:, cols] + skip_ref[:, cols] * xc_ref[cs]) * _silu(z_ref[cs])
            y_ref[cs] = y.astype(BF16)

    @pl.when(t == pl.num_programs(1) - 1)
    def _():
        cout_ref[...] = c_scr[...]
        nout_ref[...] = n_scr[...]
        mout_ref[...] = m_scr[...]


M_LANES = 128
ML_SCAN_BLOCK = 256
ML_HEAD_BATCH = 2


def _ml_scan(q, k, v, gates, xc, z, g_norm, skip, state, nseq, seqlen):
    n, di = q.shape
    hd = di // ML_HEADS
    tt = _pick_block(seqlen, (256, 128, 64))
    blk = min(tt, ML_SCAN_BLOCK)
    nt = seqlen // tt
    zero_init = state is None
    tmap = lambda s, t: (s * nt + t, 0)
    smap = lambda s, t: (s, 0, 0, 0)
    ng = gates.shape[1]
    nh = ML_HEADS
    tok = pl.BlockSpec((tt, di), tmap)
    in_specs = [tok, tok, tok, pl.BlockSpec((tt, ng), tmap), tok, tok,
                _const(g_norm.shape), _const(skip.shape)]
    args = [q, k, v, gates, xc, z, g_norm, skip]
    state_specs = [pl.BlockSpec((None, nh, hd, hd), smap), pl.BlockSpec((None, nh, 1, hd), smap),
                   pl.BlockSpec((None, nh, 1, M_LANES), smap)]
    if not zero_init:
        in_specs += state_specs
        args += list(state)
    return pl.pallas_call(
        functools.partial(_ml_scan_kernel, nchunks=tt // blk, blk=blk, zero_init=zero_init),
        grid=(nseq, nt),
        in_specs=in_specs,
        out_specs=(tok, *state_specs),
        out_shape=(jax.ShapeDtypeStruct((n, di), BF16),
                   jax.ShapeDtypeStruct((nseq, nh, hd, hd), F32),
                   jax.ShapeDtypeStruct((nseq, nh, 1, hd), F32),
                   jax.ShapeDtypeStruct((nseq, nh, 1, M_LANES), F32)),
        scratch_shapes=[pltpu.VMEM((nh, hd, hd), F32), pltpu.VMEM((nh, 1, hd), F32),
                        pltpu.VMEM((nh, 1, M_LANES), F32)],
        compiler_params=_cparams(("parallel", "arbitrary")),
        name="ml_scan",
    )(*args)


NCAND_ROWS = 72


def _first_max(vals, ids, big):
    m = jnp.max(vals, axis=0, keepdims=True)
    sel = jnp.min(jnp.where(vals == m, ids, big), axis=0, keepdims=True)
    return m, sel


def _sorting_network(n):
    size = 1
    while size < n:
        size *= 2
    pairs = []
    p = 1
    while p < size:
        k = p
        while k >= 1:
            for j in range(k % p, size - k, 2 * k):
                for i in range(min(k, size - j - k)):
                    if (i + j) // (2 * p) == (i + j + k) // (2 * p):
                        pairs.append((i + j, i + j + k))
            k //= 2
        p *= 2
    return [(i, j) for i, j in pairs if j < n]


SUBLANES = 8
PACKED_SUBLANES = 16


def _sorted_best16(s):
    k16 = PEER_TOPK
    rows, lanes = s.shape
    nslab = rows // SUBLANES
    v = [s[SUBLANES * j:SUBLANES * (j + 1), :] for j in range(nslab)]
    for i, j in _sorting_network(nslab):
        v[i], v[j] = jnp.maximum(v[i], v[j]), jnp.minimum(v[i], v[j])
    v = v[:k16] + [jnp.full((SUBLANES, lanes), NEG_INF, F32)] * (k16 - nslab)
    for shift in (1, 2, 4):
        m = [jnp.maximum(v[i], pltpu.roll(v[k16 - 1 - i], shift, axis=0)) for i in range(k16)]
        d = k16 // 2
        while d >= 1:
            for i in range(k16):
                if (i // d) % 2 == 0:
                    m[i], m[i + d] = jnp.maximum(m[i], m[i + d]), jnp.minimum(m[i], m[i + d])
            d //= 2
        v = m
    return v


def _tie_flag(tops, member):
    tie = jnp.zeros_like(tops[0])
    for r in range(len(tops) - 1):
        tie = jnp.maximum(tie, (tops[r] == tops[r + 1]).astype(F32))
    count = jnp.sum(member.astype(F32), axis=0, keepdims=True)
    return jnp.maximum(jnp.max(tie, axis=0, keepdims=True), (count != float(PEER_TOPK)).astype(F32))


def _peer_route_kernel(qt_ref, keys_ref, r2_ref, e2_ref, n_ref, e1_ref, s_scr, rank_scr, top_scr, *, tk):
    nk, k16 = PEER_NKEYS, PEER_TOPK
    key_id = lax.broadcasted_iota(jnp.int32, (nk, tk), 0)
    slot_id = lax.broadcasted_iota(jnp.int32, (k16, tk), 0)

    def half_body(hc, carry):
        start = pl.multiple_of(hc * PEER_HALF, PEER_HALF)
        s = _dot(keys_ref[hc], qt_ref[pl.ds(start, PEER_HALF), :])
        s_scr[hc] = s

        tops = _sorted_best16(s)
        s3 = s.reshape(nk // SUBLANES, SUBLANES, tk)
        member = s3 >= tops[k16 - 1][None]
        tie = _tie_flag(tops, member.reshape(nk, tk))

        def by_value():
            rank = jnp.where(member, float(k16 - 1), float(k16))
            for r in range(k16 - 2, -1, -1):
                rank = jnp.where(s3 >= tops[r][None], float(r), rank)
            sub = lax.broadcasted_iota(jnp.int32, (SUBLANES, tk), 0)
            halves = []
            for base in range(0, k16, SUBLANES):
                slab = tops[base]
                for r in range(1, SUBLANES):
                    slab = jnp.where(sub == r, tops[base + r], slab)
                halves.append(slab)
            return rank.reshape(nk, tk), jnp.concatenate(halves, axis=0)

        def by_value_and_index():
            def pick(r, st):
                s_cur, rank, top = st
                m, sel = _first_max(s_cur, key_id, nk)
                hit = key_id == sel
                return (jnp.where(hit, NEG_INF, s_cur), jnp.where(hit, jnp.asarray(r, F32), rank),
                        jnp.where(slot_id == r, m, top))

            _, rank, top = lax.fori_loop(
                0, k16, pick, (s, jnp.full((nk, tk), float(k16), F32), jnp.zeros((k16, tk), F32)))
            return rank, top

        rank, top = lax.cond(jnp.max(tie) > 0.0, by_value_and_index, by_value)
        rank_scr[hc] = rank
        top_scr[hc] = top
        return carry

    lax.fori_loop(0, 2 * PEER_HEADS, half_body, 0)

    def cand_ids():
        i16 = lax.broadcasted_iota(jnp.int32, (16, tk), 0)
        i8 = lax.broadcasted_iota(jnp.int32, (8, tk), 0)
        parts = [i16, 16 + i8, 32 + i8, 48 + i8, 16 * i16, 16 * i8 + 1, 16 * i8 + 2]
        return jnp.concatenate(parts, axis=0)

    cand_id = cand_ids()
    row72 = lax.broadcasted_iota(jnp.int32, (NCAND_ROWS, tk), 0)
    cand_ok = (row72 < 40) | (cand_id >= 64)
    big_id = 16 * 16

    def head_body(h, carry):
        a = top_scr[2 * h]
        b = top_scr[2 * h + 1]
        parts = [a[0:1, :] + b, a[1:2, :] + b[0:8, :], a[2:3, :] + b[0:8, :], a[3:4, :] + b[0:8, :],
                 a + b[0:1, :], a[0:8, :] + b[1:2, :], a[0:8, :] + b[2:3, :]]
        cand = jnp.where(cand_ok, jnp.concatenate(parts, axis=0), NEG_INF)
        tops = _sorted_best16(cand)
        cand3 = cand.reshape(NCAND_ROWS // SUBLANES, SUBLANES, tk)
        sel = (cand3 >= tops[k16 - 1][None]).reshape(NCAND_ROWS, tk)
        tie = _tie_flag(tops, sel)

        def by_value():
            self = sel.astype(F32)
            cnt = self[40:56, :] + jnp.concatenate(
                [self[56:64, :] + self[64:72, :], jnp.zeros((8, tk), F32)], axis=0)
            for r1, (lo, hi) in enumerate(((0, 16), (16, 24), (24, 32), (32, 40))):
                row = jnp.sum(self[lo:hi, :], axis=0, keepdims=True)
                cnt = cnt + jnp.where(slot_id == r1, row, 0.0)
            ex = jnp.exp(cand3 - tops[0][None]).reshape(NCAND_ROWS, tk)
            z = jnp.sum(jnp.where(sel, ex, 0.0), axis=0, keepdims=True)
            return cnt, z

        def by_value_and_index():
            def pick(r, st):
                cand_cur, cnt, tsel = st
                m, sel = _first_max(cand_cur, cand_id, big_id)
                cand_cur = jnp.where(cand_id == sel, NEG_INF, cand_cur)
                cnt = cnt + (slot_id == (sel >> 4)).astype(F32)
                return cand_cur, cnt, jnp.where(slot_id == r, m, tsel)

            _, cnt, tsel = lax.fori_loop(
                0, k16, pick, (cand, jnp.zeros((k16, tk), F32), jnp.zeros((k16, tk), F32)))
            return cnt, jnp.sum(jnp.exp(tsel - tsel[0:1, :]), axis=0, keepdims=True)

        cnt, z = lax.cond(jnp.max(tie) > 0.0, by_value_and_index, by_value)
        rank1 = rank_scr[2 * h]
        rank2 = rank_scr[2 * h + 1]
        pk = PACKED_SUBLANES
        rank1b = rank1.astype(BF16).reshape(nk // pk, pk, tk)
        n_sel = jnp.zeros((nk // pk, pk, tk), BF16)
        for r in range(k16):
            row = jnp.broadcast_to(cnt[r:r + 1, :], (pk, tk)).astype(BF16)
            n_sel = jnp.where(rank1b == r, row[None], n_sel)
        n_ref[h] = n_sel.reshape(nk, tk).astype(F32)
        e1_ref[h] = jnp.where(rank1 < k16, jnp.exp(s_scr[2 * h] - a[0:1, :]) * (GELU_OUT_SCALE / z), 0.0)
        r2_ref[h] = rank2.astype(BF16)
        e2_ref[h] = jnp.where(rank2 < k16, jnp.exp(s_scr[2 * h + 1] - b[0:1, :]), 0.0).astype(BF16)
        return carry

    lax.fori_loop(0, PEER_HEADS, head_body, 0)


def _peer_route(qt, keys):
    dq, n = qt.shape
    tk = _pick_block(n, (512, 256, 128))
    nk = PEER_NKEYS
    shp = jax.ShapeDtypeStruct((PEER_HEADS, nk, n), F32)
    shp16 = jax.ShapeDtypeStruct((PEER_HEADS, nk, n), BF16)
    ospec = pl.BlockSpec((PEER_HEADS, nk, tk), lambda i: (0, 0, i))
    return pl.pallas_call(
        functools.partial(_peer_route_kernel, tk=tk),
        grid=(n // tk,),
        in_specs=[pl.BlockSpec((dq, tk), lambda i: (0, i)), _full(keys.shape)],
        out_specs=(ospec, ospec, ospec, ospec),
        out_shape=(shp16, shp16, shp, shp),
        scratch_shapes=[pltpu.VMEM((2 * PEER_HEADS, nk, tk), F32), pltpu.VMEM((2 * PEER_HEADS, nk, tk), F32),
                        pltpu.VMEM((2 * PEER_HEADS, PEER_TOPK, tk), F32)],
        compiler_params=_cparams(("parallel",)),
        name="peer_route",
    )(qt, keys)


def _peer_gate_stage(row0, act_ref, coef_ref, r2_ref, e2_ref, n_ref, e1_ref):
    nk = PEER_NKEYS
    t = act_ref.shape[1]
    pk = PACKED_SUBLANES
    zero = jnp.zeros((), BF16)
    for al in range(PEER_EXPERT_PIECE // nk):
        a = row0 + al
        rows = slice(al * nk, (al + 1) * nk)
        y = act_ref[rows, :]
        ge = (y * (1.0 + lax.erf(y))).astype(BF16).reshape(nk // pk, pk, t)
        g = None
        for h in range(PEER_HEADS):
            n16 = jnp.broadcast_to(n_ref[h, a:a + 1, :], (pk, t)).astype(BF16)
            e16 = jnp.broadcast_to(e1_ref[h, a:a + 1, :], (pk, t)).astype(BF16)
            r2h = r2_ref[h].reshape(nk // pk, pk, t)
            e2h = e2_ref[h].reshape(nk // pk, pk, t)
            term = jnp.where(r2h < n16[None], e2h, zero) * e16[None]
            g = term if g is None else g + term
        coef_ref[rows, :] = (g * ge).reshape(nk, t)


def _peer_expert_kernel(xt_ref, r2_ref, e2_ref, n_ref, e1_ref, u_ref, vt_ref, out_ref, act_scr, coef_scr):
    k = pl.program_id(1)
    ep = PEER_EXPERT_PIECE
    npiece = act_scr.shape[0]
    rows = ep // PEER_NKEYS

    @pl.when(k == 0)
    def _():
        out_ref[...] = jnp.zeros_like(out_ref)

    xt = xt_ref[...]
    for p in range(npiece):
        act_scr[p] = _dot(u_ref[p * ep:(p + 1) * ep, :], xt)
    for p in range(npiece):
        _peer_gate_stage(p * rows, act_scr.at[p], coef_scr.at[p], r2_ref, e2_ref, n_ref, e1_ref)
    for p in range(npiece):
        out_ref[...] += _dot(vt_ref[:, p * ep:(p + 1) * ep], coef_scr[p])


def _peer_expert(xt, r2, e2, nsel, e1, u, vt):
    d, n = xt.shape
    ne = u.shape[0]
    t = _pick_block(n, (512, 256, 128))
    et = PEER_EXPERT_STEP
    npiece = et // PEER_EXPERT_PIECE
    rspec = pl.BlockSpec((PEER_HEADS, PEER_NKEYS, t), lambda i, k: (0, 0, i))
    row_spec = pl.BlockSpec((PEER_HEADS, et // PEER_NKEYS, t), lambda i, k: (0, k, i))
    return pl.pallas_call(
        _peer_expert_kernel,
        grid=(n // t, ne // et),
        in_specs=[pl.BlockSpec((d, t), lambda i, k: (0, i)), rspec, rspec, row_spec, row_spec,
                  pl.BlockSpec((et, d), lambda i, k: (k, 0)),
                  pl.BlockSpec((d, et), lambda i, k: (0, k))],
        out_specs=pl.BlockSpec((d, t), lambda i, k: (0, i)),
        out_shape=jax.ShapeDtypeStruct((d, n), F32),
        scratch_shapes=[pltpu.VMEM((npiece, PEER_EXPERT_PIECE, t), F32),
                        pltpu.VMEM((npiece, PEER_EXPERT_PIECE, t), BF16)],
        compiler_params=_cparams(("parallel", "arbitrary")),
        name="peer_expert",
    )(xt, r2, e2, nsel, e1, u, vt)


def _ple_kernel(ot_ref, h1_ref, p_ref, gple_ref, wgate_ref, wproj_ref, *rest, final):
    h2 = h1_ref[...] + ot_ref[...].T
    gate = jax.nn.sigmoid(_dot(_rms(h2, gple_ref[...]).astype(BF16), wgate_ref[...]))
    h3 = h2 + gate * _dot(p_ref[...].astype(BF16), wproj_ref[...])
    if final:
        gfinal_ref, y_ref = rest
        y_ref[...] = _rms(h3, gfinal_ref[...])
    else:
        (h_ref,) = rest
        h_ref[...] = h3


def _ple(out_t, h1, p, g_ple, w_gate, w_proj, g_final=None):
    n, d = h1.shape
    tb = _pick_block(n, (512, 256, 128))
    final = g_final is not None
    in_specs = [pl.BlockSpec((d, tb), lambda i: (0, i)), _tok(tb, d), _tok(tb, p.shape[1]),
                _full(g_ple.shape), _full(w_gate.shape), _full(w_proj.shape)]
    args = [out_t, h1, p, g_ple, w_gate, w_proj]
    if final:
        in_specs.append(_full(g_final.shape))
        args.append(g_final)
    return pl.pallas_call(
        functools.partial(_ple_kernel, final=final),
        grid=(n // tb,),
        in_specs=in_specs,
        out_specs=_tok(tb, d),
        out_shape=jax.ShapeDtypeStruct((n, d), F32),
        compiler_params=_cparams(("parallel",)),
        name="ple",
    )(*args)


def _block_diag_tiles(w):
    g, bi, bo = w.shape
    per = MXU_TILE // bi
    wt = w.reshape(g // per, per, bi, bo)
    eye = jnp.eye(per, dtype=w.dtype)
    dense = jnp.einsum("tgio,gh->tgiho", wt, eye)
    return dense.reshape(g // per, per * bi, per * bo)


def _row(x):
    return x.reshape(1, -1)


def _prep_weights(W, depth):
    P = {}
    n_gla = W["w_gla_in"].shape[0]
    dk = W["w_gla_gate"].shape[2]
    rank = W["w_gla_gate"].shape[1]
    dv = (W["w_gla_in"].shape[2] - rank - 2 * dk) // 2
    P["gla_dims"] = (dk, dv)
    pad = 128 - rank
    P["w_gla_qkvr"] = W["w_gla_in"][:, :, :2 * dk + 2 * dv].astype(BF16)
    P["w_gla_gl"] = jnp.pad(W["w_gla_in"][:, :, 2 * dk + 2 * dv:], ((0, 0), (0, 0), (0, pad))).astype(BF16)
    P["w_gla_gate"] = jnp.pad(W["w_gla_gate"], ((0, 0), (0, pad), (0, 0))).astype(BF16)
    P["w_gla_out"] = W["w_gla_out"].astype(BF16)
    P["w_ml_up"] = W["w_ml_up"].astype(BF16)
    n_ml = W["w_ml_up"].shape[0]
    for nm in ("w_ml_q", "w_ml_k", "w_ml_v"):
        P[nm] = jnp.stack([_block_diag_tiles(W[nm][j]) for j in range(n_ml)]).astype(BF16)
    wif = jnp.concatenate([W["w_ml_igate"], W["w_ml_fgate"]], axis=-1)
    P["w_ml_if"] = jnp.pad(wif, ((0, 0), (0, 0), (0, 128 - wif.shape[-1]))).astype(BF16)
    bif = jnp.concatenate([W["b_ml_igate"], W["b_ml_fgate"]], axis=-1)
    P["b_ml_if"] = jnp.pad(bif, ((0, 0), (0, 128 - bif.shape[-1])))
    P["w_ml_down"] = W["w_ml_down"].astype(BF16)
    P["w_peer_qt"] = jnp.swapaxes(W["w_peer_query"], 1, 2).astype(BF16)
    keys = W["peer_keys"]
    P["peer_keys"] = jnp.swapaxes(keys, 1, 2).reshape(depth, 2 * PEER_HEADS, PEER_NKEYS, PEER_HALF).astype(BF16)
    P["peer_u"] = (W["peer_u"] * GELU_ARG_SCALE).astype(BF16)
    P["peer_vt"] = jnp.swapaxes(W["peer_v"], 1, 2).astype(BF16)
    P["w_ple_gate"] = W["w_ple_gate"].astype(BF16)
    P["w_ple_proj"] = W["w_ple_proj"].astype(BF16)
    del n_gla
    return P


def _run_group(x, p, s_gla, c_ml, n_ml, m_ml, buf_ml, W, P, depth):
    bsz, seqlen, d = x.shape
    n = bsz * seqlen
    h = x.reshape(n, d)
    dk, dv = P["gla_dims"]
    new_s, new_c, new_n, new_m, new_buf = [], [], [], [], []
    for i in range(depth):
        j = i // 2
        if i % 2 == 0:
            q, k, v, r, lg = _gla_pre(h, _row(W["norm_mix"][i]), P["w_gla_qkvr"][j], P["w_gla_gl"][j],
                                      P["w_gla_gate"][j], _row(W["b_gla_gate"][j]), dk, dv)
            s0 = None if s_gla is None else s_gla[j]
            y, s_fin = _gla_scan(q, k, v, lg, r, _row(W["g_gla_norm"][j]), s0, bsz, seqlen)
            new_s.append(s_fin)
            h1, xt, qt = _mix_post(y, h, P["w_gla_out"][j], _row(W["norm_ffn"][i]), P["w_peer_qt"][i])
        else:
            di = P["w_ml_up"][j].shape[1] // 2
            assert seqlen >= HALO, "the new conv state is read from one sequence's last rows"
            if buf_ml is None:
                buf = jnp.zeros((bsz, HALO, di), F32)
            else:
                buf = jnp.pad(buf_ml[j], ((0, 0), (HALO - (ML_CONV - 1), 0), (0, 0)))
            qm, km, vm, xc, z, gates, tail = _ml_in(
                h, _row(W["norm_mix"][i]), P["w_ml_up"][j], buf, W["ml_conv_w"][j], _row(W["ml_conv_b"][j]),
                P["w_ml_q"][j], P["w_ml_k"][j], P["w_ml_v"][j], P["w_ml_if"][j], _row(P["b_ml_if"][j]),
                bsz, seqlen)
            if c_ml is None:
                state = None
            else:
                state = (c_ml[j], n_ml[j][:, :, None, :],
                         jnp.broadcast_to(m_ml[j][:, :, None, None], m_ml[j].shape + (1, M_LANES)))
            y, c_fin, n_fin, m_fin = _ml_scan(qm, km, vm, gates, xc, z, _row(W["g_ml_norm"][j]),
                                              _row(W["ml_skip"][j]), state, bsz, seqlen)
            new_c.append(c_fin)
            new_n.append(n_fin[:, :, 0, :])
            new_m.append(m_fin[:, :, 0, 0])
            new_buf.append(tail[:, HALO - (ML_CONV - 1):, :])
            h1, xt, qt = _mix_post(y, h, P["w_ml_down"][j], _row(W["norm_ffn"][i]), P["w_peer_qt"][i])
        r2, e2, nsel, e1 = _peer_route(qt, P["peer_keys"][i])
        out_t = _peer_expert(xt, r2, e2, nsel, e1, P["peer_u"][i], P["peer_vt"][i])
        g_final = _row(W["norm_final"]) if i == depth - 1 else None
        h = _ple(out_t, h1, p[i].reshape(n, -1), _row(W["norm_ple"][i]), P["w_ple_gate"][i],
                 P["w_ple_proj"][i], g_final)
    y = h
    return (y.reshape(bsz, seqlen, d), jnp.stack(new_s), jnp.stack(new_c), jnp.stack(new_n),
            jnp.stack(new_m), jnp.stack(new_buf))


def kernel(x_prompt, x_sample, state_gla_S, state_mlstm_C, state_mlstm_n, state_mlstm_m, state_mlstm_conv,
           p_prompt, p_sample, w_gla_in, w_gla_gate, b_gla_gate, g_gla_norm, w_gla_out,
           w_ml_up, ml_conv_w, ml_conv_b, w_ml_q, w_ml_k, w_ml_v, w_ml_igate, b_ml_igate,
           w_ml_fgate, b_ml_fgate, g_ml_norm, ml_skip, w_ml_down,
           w_peer_query, peer_keys, peer_u, peer_v, norm_mix, norm_ffn, norm_ple,
           w_ple_gate, w_ple_proj, norm_final):
    W = dict(w_gla_in=w_gla_in, w_gla_gate=w_gla_gate, b_gla_gate=b_gla_gate, g_gla_norm=g_gla_norm,
             w_gla_out=w_gla_out, w_ml_up=w_ml_up, ml_conv_w=ml_conv_w, ml_conv_b=ml_conv_b,
             w_ml_q=w_ml_q, w_ml_k=w_ml_k, w_ml_v=w_ml_v, w_ml_igate=w_ml_igate, b_ml_igate=b_ml_igate,
             w_ml_fgate=w_ml_fgate, b_ml_fgate=b_ml_fgate, g_ml_norm=g_ml_norm, ml_skip=ml_skip,
             w_ml_down=w_ml_down, w_peer_query=w_peer_query, peer_keys=peer_keys, peer_u=peer_u,
             peer_v=peer_v, norm_mix=norm_mix, norm_ffn=norm_ffn, norm_ple=norm_ple,
             w_ple_gate=w_ple_gate, w_ple_proj=w_ple_proj, norm_final=norm_final)
    depth = norm_mix.shape[0]
    P = _prep_weights(W, depth)
    y_p, s_p, c_p, n_p, m_p, buf_p = _run_group(x_prompt, p_prompt, None, None, None, None, None, W, P, depth)
    y_s, s_s, c_s, n_s, m_s, buf_s = _run_group(x_sample, p_sample, state_gla_S, state_mlstm_C,
                                                state_mlstm_n, state_mlstm_m, state_mlstm_conv, W, P, depth)
    return (y_p, y_s, s_p, s_s, c_p, c_s, n_p, n_s, m_p, m_s, buf_p, buf_s)
```

```python
import functools

import jax
import jax.numpy as jnp
from jax import lax
from jax.experimental import pallas as pl
from jax.experimental.pallas import tpu as pltpu

F32 = jnp.float32
BF16 = jnp.bfloat16
EPS = 1e-6
CHUNK = 64
HIGHEST = lax.Precision.HIGHEST
NEG_INF = float("-inf")

GLA_HEADS = 4
GLA_GATE_NORMALIZER = 16.0
ML_HEADS = 4
ML_CONV = 4
PEER_HEADS = 8
PEER_NKEYS = 128
PEER_HALF = 64
PEER_TOPK = 16

VMEM_LIMIT_BYTES = 52 * 1024 * 1024
MXU_TILE = 256
PEER_EXPERT_STEP = 2048
PEER_EXPERT_PIECE = MXU_TILE
GELU_ARG_SCALE = 0.7071067811865476
GELU_OUT_SCALE = 0.5 / GELU_ARG_SCALE


def _cparams(sem):
    return pltpu.CompilerParams(dimension_semantics=sem, vmem_limit_bytes=VMEM_LIMIT_BYTES)


def _pick_block(n, candidates):
    for c in candidates:
        if n % c == 0:
            return c
    raise ValueError(f"no block size in {candidates} divides {n}")


def _rms(x, g):
    ms = jnp.mean(x * x, axis=-1, keepdims=True)
    return x * lax.rsqrt(ms + EPS) * g


def _log_sigmoid(x):
    return jnp.minimum(x, 0.0) - jnp.log1p(jnp.exp(-jnp.abs(x)))


def _silu(x):
    return x * jax.nn.sigmoid(x)


def _dot(a, b):
    return jnp.dot(a, b, preferred_element_type=F32)


def _tok(tb, d):
    return pl.BlockSpec((tb, d), lambda i: (i, 0))


def _full(shape):
    nd = len(shape)
    return pl.BlockSpec(shape, lambda *_: (0,) * nd)


def _gla_pre_kernel(h_ref, g_ref, w_ref, wgl_ref, wgate_ref, bgate_ref,
                    q_ref, k_ref, v_ref, r_ref, lg_ref, *, dk, dv, hk):
    xn = _rms(h_ref[...], g_ref[...]).astype(BF16)
    proj = _dot(xn, w_ref[...])
    q_ref[...] = proj[:, :dk] * (hk ** -0.5)
    k_ref[...] = proj[:, dk:2 * dk]
    v_ref[...] = proj[:, 2 * dk:2 * dk + dv].astype(BF16)
    r_ref[...] = proj[:, 2 * dk + dv:]
    gl = _dot(xn, wgl_ref[...])
    gate = _dot(gl.astype(BF16), wgate_ref[...]) + bgate_ref[...]
    lg_ref[...] = _log_sigmoid(gate) * (1.0 / GLA_GATE_NORMALIZER)


def _gla_pre(h, g, w_qkvr, w_gl, w_gate, b_gate, dk, dv):
    n, d = h.shape
    tb = _pick_block(n, (512, 256, 128, 64))
    hk = dk // GLA_HEADS
    outs = (jax.ShapeDtypeStruct((n, dk), F32), jax.ShapeDtypeStruct((n, dk), F32),
            jax.ShapeDtypeStruct((n, dv), BF16), jax.ShapeDtypeStruct((n, dv), F32),
            jax.ShapeDtypeStruct((n, dk), F32))
    return pl.pallas_call(
        functools.partial(_gla_pre_kernel, dk=dk, dv=dv, hk=hk),
        grid=(n // tb,),
        in_specs=[_tok(tb, d), _full(g.shape), _full(w_qkvr.shape), _full(w_gl.shape),
                  _full(w_gate.shape), _full(b_gate.shape)],
        out_specs=(_tok(tb, dk), _tok(tb, dk), _tok(tb, dv), _tok(tb, dv), _tok(tb, dk)),
        out_shape=outs,
        compiler_params=_cparams(("parallel",)),
        name="gla_pre",
    )(h, g, w_qkvr, w_gl, w_gate, b_gate)


GLA_SUB = 16


def _gla_scan_kernel(*refs, nchunks, zero_init):
    if zero_init:
        q_ref, k_ref, v_ref, lg_ref, r_ref, gn_ref, y_ref, sout_ref, st_ref = refs
        s0_ref = None
    else:
        q_ref, k_ref, v_ref, lg_ref, r_ref, gn_ref, s0_ref, y_ref, sout_ref, st_ref = refs
    t = pl.program_id(1)
    nheads, hv, hk = st_ref.shape

    @pl.when(t == 0)
    def _():
        if zero_init:
            st_ref[...] = jnp.zeros_like(st_ref)
        else:
            for h in range(nheads):
                st_ref[h] = s0_ref[h].T

    row = lax.broadcasted_iota(jnp.int32, (CHUNK, CHUNK), 0)
    col = lax.broadcasted_iota(jnp.int32, (CHUNK, CHUNK), 1)
    tril = (row >= col).astype(F32)
    gn = gn_ref[...]
    nsub = CHUNK // GLA_SUB

    def bmm(a, b, lhs_c, rhs_c):
        return lax.dot_general(a, b, (((lhs_c,), (rhs_c,)), ((0,), (0,))), preferred_element_type=F32)

    pairs = [(c, h) for c in range(nchunks) for h in range(nheads)]
    b_chunks = [jnp.dot(tril, lg_ref[pl.ds(c * CHUNK, CHUNK), :], precision=HIGHEST,
                        preferred_element_type=F32) for c in range(nchunks)]
    b = jnp.stack([b_chunks[c][:, h * hk:(h + 1) * hk] for c, h in pairs])
    q = jnp.stack([q_ref[pl.ds(c * CHUNK, CHUNK), h * hk:(h + 1) * hk] for c, h in pairs])
    k = jnp.stack([k_ref[pl.ds(c * CHUNK, CHUNK), h * hk:(h + 1) * hk] for c, h in pairs])
    v = jnp.stack([v_ref[pl.ds(c * CHUNK, CHUNK), h * hv:(h + 1) * hv] for c, h in pairs])
    b_last = b[:, CHUNK - 1:CHUNK, :]
    intra_parts = []
    for i in range(nsub):
        lo, hi = i * GLA_SUB, (i + 1) * GLA_SUB
        b_ref_row = b[:, lo:lo + 1, :]
        qe = (q[:, lo:hi, :] * jnp.exp(b[:, lo:hi, :] - b_ref_row)).astype(BF16)
        ke = (k[:, :hi, :] * jnp.exp(b_ref_row - b[:, :hi, :])).astype(BF16)
        att = bmm(qe, ke, 2, 2)
        r_i = lax.broadcasted_iota(jnp.int32, (GLA_SUB, hi), 0) + lo
        c_i = lax.broadcasted_iota(jnp.int32, (GLA_SUB, hi), 1)
        att = jnp.where(r_i >= c_i, att, 0.0)
        intra_parts.append(bmm(att.astype(BF16), v[:, :hi, :], 2, 1))
    intra = jnp.concatenate(intra_parts, axis=1)
    kv = bmm(v, (k * jnp.exp(b_last - b)).astype(BF16), 1, 1)
    q_dec = (q * jnp.exp(b)).astype(BF16)
    decay = jnp.exp(b_last)
    st = st_ref[...]
    for c in range(nchunks):
        ps = slice(c * nheads, (c + 1) * nheads)
        o = bmm(q_dec[ps], st.astype(BF16), 2, 2) + intra[ps]
        st = st * decay[ps] + kv[ps]
        o = o * lax.rsqrt(jnp.mean(o * o, axis=-1, keepdims=True) + EPS) * gn
        for h in range(nheads):
            cs = (pl.ds(c * CHUNK, CHUNK), slice(h * hv, (h + 1) * hv))
            y_ref[cs] = (o[h] * _silu(r_ref[cs])).astype(BF16)
    st_ref[...] = st

    @pl.when(t == pl.num_programs(1) - 1)
    def _():
        for h in range(nheads):
            sout_ref[h] = st_ref[h].T


def _gla_scan(q, k, v, lg, r, gn, s0, nseq, seqlen):
    n, dk = q.shape
    dv = v.shape[1]
    hk, hv = dk // GLA_HEADS, dv // GLA_HEADS
    tt = _pick_block(seqlen, (512, 256, 128, 64))
    nt = seqlen // tt
    zero_init = s0 is None
    tmap = lambda s, t: (s * nt + t, 0)
    smap = lambda s, t: (s, 0, 0, 0)
    in_specs = [pl.BlockSpec((tt, dk), tmap), pl.BlockSpec((tt, dk), tmap),
                pl.BlockSpec((tt, dv), tmap), pl.BlockSpec((tt, dk), tmap), pl.BlockSpec((tt, dv), tmap),
                pl.BlockSpec((1, hv), lambda s, t: (0, 0))]
    args = [q, k, v, lg, r, gn]
    if not zero_init:
        in_specs.append(pl.BlockSpec((None, GLA_HEADS, hk, hv), smap))
        args.append(s0)
    return pl.pallas_call(
        functools.partial(_gla_scan_kernel, nchunks=tt // CHUNK, zero_init=zero_init),
        grid=(nseq, nt),
        in_specs=in_specs,
        out_specs=(pl.BlockSpec((tt, dv), tmap), pl.BlockSpec((None, GLA_HEADS, hk, hv), smap)),
        out_shape=(jax.ShapeDtypeStruct((n, dv), BF16),
                   jax.ShapeDtypeStruct((nseq, GLA_HEADS, hk, hv), F32)),
        scratch_shapes=[pltpu.VMEM((GLA_HEADS, hv, hk), F32)],
        compiler_params=_cparams(("parallel", "arbitrary")),
        name="gla_scan",
    )(*args)


def _peer_pre_tail(h1, gffn_ref, wqt_ref, xt_ref, qt_ref):
    xn = _rms(h1, gffn_ref[...])
    xt = xn.T.astype(BF16)
    xt_ref[...] = xt
    qt_ref[...] = _dot(wqt_ref[...], xt).astype(BF16)


def _mix_post_kernel(y_ref, h_ref, w_ref, gffn_ref, wqt_ref, h1_ref, xt_ref, qt_ref):
    h1 = h_ref[...] + _dot(y_ref[...], w_ref[...])
    h1_ref[...] = h1
    _peer_pre_tail(h1, gffn_ref, wqt_ref, xt_ref, qt_ref)


def _mix_post(y, h, w_out, g_ffn, w_qt):
    n, d = h.shape
    tb = _pick_block(n, (512, 256, 128))
    dq = w_qt.shape[0]
    return pl.pallas_call(
        _mix_post_kernel,
        grid=(n // tb,),
        in_specs=[_tok(tb, y.shape[1]), _tok(tb, d), _full(w_out.shape), _full(g_ffn.shape), _full(w_qt.shape)],
        out_specs=(_tok(tb, d), pl.BlockSpec((d, tb), lambda i: (0, i)),
                   pl.BlockSpec((dq, tb), lambda i: (0, i))),
        out_shape=(jax.ShapeDtypeStruct((n, d), F32), jax.ShapeDtypeStruct((d, n), BF16),
                   jax.ShapeDtypeStruct((dq, n), BF16)),
        compiler_params=_cparams(("parallel",)),
        name="mix_post",
    )(y, h, w_out, g_ffn, w_qt)


HALO = 8


def _ml_in_kernel(h_ref, g_ref, wup_ref, buf_ref, cw_ref, cb_ref, wq_ref, wk_ref, wv_ref, wif_ref, bif_ref,
                  q_ref, k_ref, v_ref, xc_ref, z_ref, gates_ref, tail_ref, xp_scr, *, tt, di, hd):
    t = pl.program_id(1)

    @pl.when(t == 0)
    def _():
        xp_scr[0:HALO, :] = buf_ref[...]

    @pl.when(t > 0)
    def _():
        xp_scr[0:HALO, :] = xp_scr[tt:tt + HALO, :]

    xn = _rms(h_ref[...], g_ref[...]).astype(BF16)
    up = _dot(xn, wup_ref[...])
    x = up[:, :di]
    z_ref[...] = up[:, di:]
    xp_scr[HALO:HALO + tt, :] = x
    tail_ref[...] = x[tt - HALO:, :]
    y = cb_ref[...]
    for j in range(ML_CONV):
        y = y + cw_ref[j:j + 1, :] * xp_scr[pl.ds(HALO - (ML_CONV - 1) + j, tt), :]
    xc = _silu(y)
    xc_ref[...] = xc
    xcb = xc.astype(BF16)
    xmb = x.astype(BF16)
    gates = jnp.zeros((tt, gates_ref.shape[1]), F32) + bif_ref[...]
    ntile = di // MXU_TILE
    for j in range(ntile):
        cs = slice(j * MXU_TILE, (j + 1) * MXU_TILE)
        qj = _dot(xcb[:, cs], wq_ref[j])
        kj = _dot(xcb[:, cs], wk_ref[j])
        vj = _dot(xmb[:, cs], wv_ref[j])
        q_ref[:, cs] = qj.astype(BF16)
        k_ref[:, cs] = kj * (hd ** -0.5)
        v_ref[:, cs] = vj.astype(BF16)
        gates = gates + _dot(qj.astype(BF16), wif_ref[j * MXU_TILE:(j + 1) * MXU_TILE, :])
        gates = gates + _dot(kj.astype(BF16), wif_ref[di + j * MXU_TILE:di + (j + 1) * MXU_TILE, :])
        gates = gates + _dot(vj.astype(BF16), wif_ref[2 * di + j * MXU_TILE:2 * di + (j + 1) * MXU_TILE, :])
    lane = lax.broadcasted_iota(jnp.int32, gates.shape, 1)
    is_f = (lane >= ML_HEADS) & (lane < 2 * ML_HEADS)
    gates_ref[...] = jnp.where(is_f, _log_sigmoid(gates), gates)


def _const(shape):
    nd = len(shape)
    return pl.BlockSpec(shape, lambda *_: (0,) * nd, pipeline_mode=pl.Buffered(1))


def _ml_in(h, g, w_up, buf, conv_w, conv_b, wq, wk, wv, wif, bif, nseq, seqlen):
    n, d = h.shape
    di = w_up.shape[1] // 2
    hd = di // ML_HEADS
    tt = _pick_block(seqlen, (256, 128, 64))
    nt = seqlen // tt
    tmap = lambda s, t: (s * nt + t, 0)
    smap = lambda s, t: (s, 0, 0)
    ng = wif.shape[1]
    tok = lambda width: pl.BlockSpec((tt, width), tmap)
    return pl.pallas_call(
        functools.partial(_ml_in_kernel, tt=tt, di=di, hd=hd),
        grid=(nseq, nt),
        in_specs=[tok(d), _const(g.shape), _const(w_up.shape), pl.BlockSpec((None, HALO, di), smap),
                  _const(conv_w.shape), _const(conv_b.shape), _const(wq.shape), _const(wk.shape),
                  _const(wv.shape), _const(wif.shape), _const(bif.shape)],
        out_specs=(tok(di), tok(di), tok(di), tok(di), tok(di), tok(ng),
                   pl.BlockSpec((None, HALO, di), smap)),
        out_shape=(jax.ShapeDtypeStruct((n, di), BF16), jax.ShapeDtypeStruct((n, di), F32),
                   jax.ShapeDtypeStruct((n, di), BF16), jax.ShapeDtypeStruct((n, di), F32),
                   jax.ShapeDtypeStruct((n, di), F32), jax.ShapeDtypeStruct((n, ng), F32),
                   jax.ShapeDtypeStruct((nseq, HALO, di), F32)),
        scratch_shapes=[pltpu.VMEM((HALO + tt, di), F32)],
        compiler_params=_cparams(("parallel", "arbitrary")),
        name="ml_in",
    )(h, g, w_up, buf, conv_w, conv_b, wq, wk, wv, wif, bif)


def _ml_scan_kernel(*refs, nchunks, blk, zero_init):
    if zero_init:
        (q_ref, k_ref, v_ref, g_ref, xc_ref, z_ref, gn_ref, skip_ref,
         y_ref, cout_ref, nout_ref, mout_ref, c_scr, n_scr, m_scr) = refs
    else:
        (q_ref, k_ref, v_ref, g_ref, xc_ref, z_ref, gn_ref, skip_ref, c0_ref, n0_ref, m0_ref,
         y_ref, cout_ref, nout_ref, mout_ref, c_scr, n_scr, m_scr) = refs
    t = pl.program_id(1)
    nheads, hd = c_scr.shape[0], c_scr.shape[1]

    @pl.when(t == 0)
    def _():
        if zero_init:
            c_scr[...] = jnp.zeros_like(c_scr)
            n_scr[...] = jnp.zeros_like(n_scr)
            m_scr[...] = jnp.zeros_like(m_scr)
        else:
            c_scr[...] = c0_ref[...]
            n_scr[...] = n0_ref[...]
            m_scr[...] = m0_ref[...]

    row = lax.broadcasted_iota(jnp.int32, (blk, blk), 0)
    col = lax.broadcasted_iota(jnp.int32, (blk, blk), 1)
    causal = row >= col
    tril = causal.astype(F32)
    lanes = g_ref.shape[1]

    def bmm(a, b, lhs_c, rhs_c, precision=None):
        return lax.dot_general(a, b, (((lhs_c,), (rhs_c,)), ((0,), (0,))), precision=precision,
                               preferred_element_type=F32)

    shared = {}
    for c, h0 in [(c, h0) for c in range(nchunks) for h0 in range(0, nheads, ML_HEAD_BATCH)]:
        heads = range(h0, h0 + ML_HEAD_BATCH)
        hsl = slice(h0, h0 + ML_HEAD_BATCH)
        sl = pl.ds(c * blk, blk)
        q = jnp.stack([q_ref[sl, h * hd:(h + 1) * hd] for h in heads])
        kf = jnp.stack([k_ref[sl, h * hd:(h + 1) * hd] for h in heads])
        v = jnp.stack([v_ref[sl, h * hd:(h + 1) * hd] for h in heads])
        cmat = c_scr[hsl]
        nvec = n_scr[hsl]
        m = m_scr[hsl][:, :, 0:1]
        if c not in shared:
            g = g_ref[sl, :]
            gcum = jnp.dot(tril, g, precision=HIGHEST, preferred_element_type=F32)
            diff_t = (g - pltpu.roll(gcum, lanes - nheads, axis=1)).T if blk % lanes == 0 else None
            shared[c] = (g, gcum, diff_t)
        g, gcum, diff_t = shared[c]
        i_col = jnp.stack([g[:, h:h + 1] for h in heads])
        f_col = jnp.stack([gcum[:, h + nheads:h + nheads + 1] for h in heads])
        if diff_t is not None:
            w_row = jnp.stack([diff_t[h:h + 1, :] for h in heads])
        else:
            eye = row == col
            w_row = bmm(jnp.ones((ML_HEAD_BATCH, blk, blk), F32), jnp.where(eye, i_col - f_col, 0.0), 2, 1,
                        precision=HIGHEST)
        log_d = jnp.where(causal, f_col + w_row, NEG_INF)
        inter = f_col + m
        mt = jnp.maximum(inter, jnp.max(log_d, axis=-1, keepdims=True))
        sc = bmm(q, kf.astype(BF16), 2, 2) * jnp.exp(log_d - mt)
        a = jnp.exp(inter - mt)
        num = a * bmm(q, cmat.astype(BF16), 2, 1) + bmm(sc.astype(BF16), v, 2, 1)
        qn = jnp.sum(q.astype(F32) * nvec, axis=-1, keepdims=True)
        den = a * qn + jnp.sum(sc, axis=-1, keepdims=True)
        hh = num / jnp.maximum(jnp.abs(den), jnp.exp(-mt))
        m_new = mt[:, blk - 1:blk, :]
        f_last = f_col[:, blk - 1:blk, :]
        w_end = jnp.exp(f_last - f_col + i_col - m_new)
        a_end = jnp.exp(f_last + m - m_new)
        kw = kf * w_end
        c_scr[hsl] = a_end * cmat + bmm(kw.astype(BF16), v, 1, 1)
        n_scr[hsl] = a_end * nvec + jnp.sum(kw, axis=1, keepdims=True)
        m_scr[hsl] = jnp.broadcast_to(m_new, (ML_HEAD_BATCH, 1, M_LANES))
        mu = jnp.mean(hh, axis=-1, keepdims=True)
        var = jnp.mean(jnp.square(hh - mu), axis=-1, keepdims=True)
        hn = (hh - mu) * lax.rsqrt(var + EPS)
        for i, h in enumerate(heads):
            cs = (sl, slice(h * hd, (h + 1) * hd))
            cols = cs[1]
            y = (hn[i] * gn_ref[:, cols] + skip_ref[:, cols] * xc_ref[cs]) * _silu(z_ref[cs])
            y_ref[cs] = y.astype(BF16)

    @pl.when(t == pl.num_programs(1) - 1)
    def _():
        cout_ref[...] = c_scr[...]
        nout_ref[...] = n_scr[...]
        mout_ref[...] = m_scr[...]


M_LANES = 128
ML_SCAN_BLOCK = 256
ML_HEAD_BATCH = 2


def _ml_scan(q, k, v, gates, xc, z, g_norm, skip, state, nseq, seqlen):
    n, di = q.shape
    hd = di // ML_HEADS
    tt = _pick_block(seqlen, (256, 128, 64))
    blk = min(tt, ML_SCAN_BLOCK)
    nt = seqlen // tt
    zero_init = state is None
    tmap = lambda s, t: (s * nt + t, 0)
    smap = lambda s, t: (s, 0, 0, 0)
    ng = gates.shape[1]
    nh = ML_HEADS
    tok = pl.BlockSpec((tt, di), tmap)
    in_specs = [tok, tok, tok, pl.BlockSpec((tt, ng), tmap), tok, tok,
                _const(g_norm.shape), _const(skip.shape)]
    args = [q, k, v, gates, xc, z, g_norm, skip]
    state_specs = [pl.BlockSpec((None, nh, hd, hd), smap), pl.BlockSpec((None, nh, 1, hd), smap),
                   pl.BlockSpec((None, nh, 1, M_LANES), smap)]
    if not zero_init:
        in_specs += state_specs
        args += list(state)
    return pl.pallas_call(
        functools.partial(_ml_scan_kernel, nchunks=tt // blk, blk=blk, zero_init=zero_init),
        grid=(nseq, nt),
        in_specs=in_specs,
        out_specs=(tok, *state_specs),
        out_shape=(jax.ShapeDtypeStruct((n, di), BF16),
                   jax.ShapeDtypeStruct((nseq, nh, hd, hd), F32),
                   jax.ShapeDtypeStruct((nseq, nh, 1, hd), F32),
                   jax.ShapeDtypeStruct((nseq, nh, 1, M_LANES), F32)),
        scratch_shapes=[pltpu.VMEM((nh, hd, hd), F32), pltpu.VMEM((nh, 1, hd), F32),
                        pltpu.VMEM((nh, 1, M_LANES), F32)],
        compiler_params=_cparams(("parallel", "arbitrary")),
        name="ml_scan",
    )(*args)


NCAND_ROWS = 72


def _first_max(vals, ids, big):
    m = jnp.max(vals, axis=0, keepdims=True)
    sel = jnp.min(jnp.where(vals == m, ids, big), axis=0, keepdims=True)
    return m, sel


def _sorting_network(n):
    size = 1
    while size < n:
        size *= 2
    pairs = []
    p = 1
    while p < size:
        k = p
        while k >= 1:
            for j in range(k % p, size - k, 2 * k):
                for i in range(min(k, size - j - k)):
                    if (i + j) // (2 * p) == (i + j + k) // (2 * p):
                        pairs.append((i + j, i + j + k))
            k //= 2
        p *= 2
    return [(i, j) for i, j in pairs if j < n]


SUBLANES = 8
PACKED_SUBLANES = 16


def _sorted_best16(s):
    k16 = PEER_TOPK
    rows, lanes = s.shape
    nslab = rows // SUBLANES
    v = [s[SUBLANES * j:SUBLANES * (j + 1), :] for j in range(nslab)]
    for i, j in _sorting_network(nslab):
        v[i], v[j] = jnp.maximum(v[i], v[j]), jnp.minimum(v[i], v[j])
    v = v[:k16] + [jnp.full((SUBLANES, lanes), NEG_INF, F32)] * (k16 - nslab)
    for shift in (1, 2, 4):
        m = [jnp.maximum(v[i], pltpu.roll(v[k16 - 1 - i], shift, axis=0)) for i in range(k16)]
        d = k16 // 2
        while d >= 1:
            for i in range(k16):
                if (i // d) % 2 == 0:
                    m[i], m[i + d] = jnp.maximum(m[i], m[i + d]), jnp.minimum(m[i], m[i + d])
            d //= 2
        v = m
    return v


def _tie_flag(tops, member):
    tie = jnp.zeros_like(tops[0])
    for r in range(len(tops) - 1):
        tie = jnp.maximum(tie, (tops[r] == tops[r + 1]).astype(F32))
    count = jnp.sum(member.astype(F32), axis=0, keepdims=True)
    return jnp.maximum(jnp.max(tie, axis=0, keepdims=True), (count != float(PEER_TOPK)).astype(F32))


def _peer_route_kernel(qt_ref, keys_ref, r2_ref, e2_ref, n_ref, e1_ref, s_scr, rank_scr, top_scr, *, tk):
    nk, k16 = PEER_NKEYS, PEER_TOPK
    key_id = lax.broadcasted_iota(jnp.int32, (nk, tk), 0)
    slot_id = lax.broadcasted_iota(jnp.int32, (k16, tk), 0)

    def half_body(hc, carry):
        start = pl.multiple_of(hc * PEER_HALF, PEER_HALF)
        s = _dot(keys_ref[hc], qt_ref[pl.ds(start, PEER_HALF), :])
        s_scr[hc] = s

        tops = _sorted_best16(s)
        s3 = s.reshape(nk // SUBLANES, SUBLANES, tk)
        member = s3 >= tops[k16 - 1][None]
        tie = _tie_flag(tops, member.reshape(nk, tk))

        def by_value():
            rank = jnp.where(member, float(k16 - 1), float(k16))
            for r in range(k16 - 2, -1, -1):
                rank = jnp.where(s3 >= tops[r][None], float(r), rank)
            sub = lax.broadcasted_iota(jnp.int32, (SUBLANES, tk), 0)
            halves = []
            for base in range(0, k16, SUBLANES):
                slab = tops[base]
                for r in range(1, SUBLANES):
                    slab = jnp.where(sub == r, tops[base + r], slab)
                halves.append(slab)
            return rank.reshape(nk, tk), jnp.concatenate(halves, axis=0)

        def by_value_and_index():
            def pick(r, st):
                s_cur, rank, top = st
                m, sel = _first_max(s_cur, key_id, nk)
                hit = key_id == sel
                return (jnp.where(hit, NEG_INF, s_cur), jnp.where(hit, jnp.asarray(r, F32), rank),
                        jnp.where(slot_id == r, m, top))

            _, rank, top = lax.fori_loop(
                0, k16, pick, (s, jnp.full((nk, tk), float(k16), F32), jnp.zeros((k16, tk), F32)))
            return rank, top

        rank, top = lax.cond(jnp.max(tie) > 0.0, by_value_and_index, by_value)
        rank_scr[hc] = rank
        top_scr[hc] = top
        return carry

    lax.fori_loop(0, 2 * PEER_HEADS, half_body, 0)

    def cand_ids():
        i16 = lax.broadcasted_iota(jnp.int32, (16, tk), 0)
        i8 = lax.broadcasted_iota(jnp.int32, (8, tk), 0)
        parts = [i16, 16 + i8, 32 + i8, 48 + i8, 16 * i16, 16 * i8 + 1, 16 * i8 + 2]
        return jnp.concatenate(parts, axis=0)

    cand_id = cand_ids()
    row72 = lax.broadcasted_iota(jnp.int32, (NCAND_ROWS, tk), 0)
    cand_ok = (row72 < 40) | (cand_id >= 64)
    big_id = 16 * 16

    def head_body(h, carry):
        a = top_scr[2 * h]
        b = top_scr[2 * h + 1]
        parts = [a[0:1, :] + b, a[1:2, :] + b[0:8, :], a[2:3, :] + b[0:8, :], a[3:4, :] + b[0:8, :],
                 a + b[0:1, :], a[0:8, :] + b[1:2, :], a[0:8, :] + b[2:3, :]]
        cand = jnp.where(cand_ok, jnp.concatenate(parts, axis=0), NEG_INF)
        tops = _sorted_best16(cand)
        cand3 = cand.reshape(NCAND_ROWS // SUBLANES, SUBLANES, tk)
        sel = (cand3 >= tops[k16 - 1][None]).reshape(NCAND_ROWS, tk)
        tie = _tie_flag(tops, sel)

        def by_value():
            self = sel.astype(F32)
            cnt = self[40:56, :] + jnp.concatenate(
                [self[56:64, :] + self[64:72, :], jnp.zeros((8, tk), F32)], axis=0)
            for r1, (lo, hi) in enumerate(((0, 16), (16, 24), (24, 32), (32, 40))):
                row = jnp.sum(self[lo:hi, :], axis=0, keepdims=True)
                cnt = cnt + jnp.where(slot_id == r1, row, 0.0)
            ex = jnp.exp(cand3 - tops[0][None]).reshape(NCAND_ROWS, tk)
            z = jnp.sum(jnp.where(sel, ex, 0.0), axis=0, keepdims=True)
            return cnt, z

        def by_value_and_index():
            def pick(r, st):
                cand_cur, cnt, tsel = st
                m, sel = _first_max(cand_cur, cand_id, big_id)
                cand_cur = jnp.where(cand_id == sel, NEG_INF, cand_cur)
                cnt = cnt + (slot_id == (sel >> 4)).astype(F32)
                return cand_cur, cnt, jnp.where(slot_id == r, m, tsel)

            _, cnt, tsel = lax.fori_loop(
                0, k16, pick, (cand, jnp.zeros((k16, tk), F32), jnp.zeros((k16, tk), F32)))
            return cnt, jnp.sum(jnp.exp(tsel - tsel[0:1, :]), axis=0, keepdims=True)

        cnt, z = lax.cond(jnp.max(tie) > 0.0, by_value_and_index, by_value)
        rank1 = rank_scr[2 * h]
        rank2 = rank_scr[2 * h + 1]
        pk = PACKED_SUBLANES
        rank1b = rank1.astype(BF16).reshape(nk // pk, pk, tk)
        n_sel = jnp.zeros((nk // pk, pk, tk), BF16)
        for r in range(k16):
            row = jnp.broadcast_to(cnt[r:r + 1, :], (pk, tk)).astype(BF16)
            n_sel = jnp.where(rank1b == r, row[None], n_sel)
        n_ref[h] = n_sel.reshape(nk, tk).astype(F32)
        e1_ref[h] = jnp.where(rank1 < k16, jnp.exp(s_scr[2 * h] - a[0:1, :]) * (GELU_OUT_SCALE / z), 0.0)
        r2_ref[h] = rank2.astype(BF16)
        e2_ref[h] = jnp.where(rank2 < k16, jnp.exp(s_scr[2 * h + 1] - b[0:1, :]), 0.0).astype(BF16)
        return carry

    lax.fori_loop(0, PEER_HEADS, head_body, 0)


def _peer_route(qt, keys):
    dq, n = qt.shape
    tk = _pick_block(n, (512, 256, 128))
    nk = PEER_NKEYS
    shp = jax.ShapeDtypeStruct((PEER_HEADS, nk, n), F32)
    shp16 = jax.ShapeDtypeStruct((PEER_HEADS, nk, n), BF16)
    ospec = pl.BlockSpec((PEER_HEADS, nk, tk), lambda i: (0, 0, i))
    return pl.pallas_call(
        functools.partial(_peer_route_kernel, tk=tk),
        grid=(n // tk,),
        in_specs=[pl.BlockSpec((dq, tk), lambda i: (0, i)), _full(keys.shape)],
        out_specs=(ospec, ospec, ospec, ospec),
        out_shape=(shp16, shp16, shp, shp),
        scratch_shapes=[pltpu.VMEM((2 * PEER_HEADS, nk, tk), F32), pltpu.VMEM((2 * PEER_HEADS, nk, tk), F32),
                        pltpu.VMEM((2 * PEER_HEADS, PEER_TOPK, tk), F32)],
        compiler_params=_cparams(("parallel",)),
        name="peer_route",
    )(qt, keys)


def _peer_gate_stage(row0, act_ref, coef_ref, r2_ref, e2_ref, n_ref, e1_ref):
    nk = PEER_NKEYS
    t = act_ref.shape[1]
    pk = PACKED_SUBLANES
    zero = jnp.zeros((), BF16)
    for al in range(PEER_EXPERT_PIECE // nk):
        a = row0 + al
        rows = slice(al * nk, (al + 1) * nk)
        y = act_ref[rows, :]
        ge = (y * (1.0 + lax.erf(y))).astype(BF16).reshape(nk // pk, pk, t)
        g = None
        for h in range(PEER_HEADS):
            n16 = jnp.broadcast_to(n_ref[h, a:a + 1, :], (pk, t)).astype(BF16)
            e16 = jnp.broadcast_to(e1_ref[h, a:a + 1, :], (pk, t)).astype(BF16)
            r2h = r2_ref[h].reshape(nk // pk, pk, t)
            e2h = e2_ref[h].reshape(nk // pk, pk, t)
            term = jnp.where(r2h < n16[None], e2h, zero) * e16[None]
            g = term if g is None else g + term
        coef_ref[rows, :] = (g * ge).reshape(nk, t)


def _peer_expert_kernel(xt_ref, r2_ref, e2_ref, n_ref, e1_ref, u_ref, vt_ref, out_ref, act_scr, coef_scr):
    k = pl.program_id(1)
    ep = PEER_EXPERT_PIECE
    npiece = act_scr.shape[0]
    rows = ep // PEER_NKEYS

    @pl.when(k == 0)
    def _():
        out_ref[...] = jnp.zeros_like(out_ref)

    xt = xt_ref[...]
    nk = PEER_NKEYS
    for p in range(npiece):
        for al in range(rows):
            act_scr[p, al * nk:(al + 1) * nk, :] = _dot(u_ref[p * ep + al * nk:p * ep + (al + 1) * nk, :], xt)
    for p in range(npiece):
        _peer_gate_stage(p * rows, act_scr.at[p], coef_scr.at[p], r2_ref, e2_ref, n_ref, e1_ref)
    for p in range(npiece):
        out_ref[...] += _dot(vt_ref[:, p * ep:(p + 1) * ep], coef_scr[p])


def _peer_expert(xt, r2, e2, nsel, e1, u, vt):
    d, n = xt.shape
    ne = u.shape[0]
    t = _pick_block(n, (512, 256, 128))
    et = PEER_EXPERT_STEP
    npiece = et // PEER_EXPERT_PIECE
    rspec = pl.BlockSpec((PEER_HEADS, PEER_NKEYS, t), lambda i, k: (0, 0, i))
    row_spec = pl.BlockSpec((PEER_HEADS, et // PEER_NKEYS, t), lambda i, k: (0, k, i))
    return pl.pallas_call(
        _peer_expert_kernel,
        grid=(n // t, ne // et),
        in_specs=[pl.BlockSpec((d, t), lambda i, k: (0, i)), rspec, rspec, row_spec, row_spec,
                  pl.BlockSpec((et, d), lambda i, k: (k, 0)),
                  pl.BlockSpec((d, et), lambda i, k: (0, k))],
        out_specs=pl.BlockSpec((d, t), lambda i, k: (0, i)),
        out_shape=jax.ShapeDtypeStruct((d, n), F32),
        scratch_shapes=[pltpu.VMEM((npiece, PEER_EXPERT_PIECE, t), F32),
                        pltpu.VMEM((npiece, PEER_EXPERT_PIECE, t), BF16)],
        compiler_params=_cparams(("parallel", "arbitrary")),
        name="peer_expert",
    )(xt, r2, e2, nsel, e1, u, vt)


def _ple_kernel(ot_ref, h1_ref, p_ref, gple_ref, wgate_ref, wproj_ref, *rest, final):
    h2 = h1_ref[...] + ot_ref[...].T
    gate = jax.nn.sigmoid(_dot(_rms(h2, gple_ref[...]).astype(BF16), wgate_ref[...]))
    h3 = h2 + gate * _dot(p_ref[...].astype(BF16), wproj_ref[...])
    if final:
        gfinal_ref, y_ref = rest
        y_ref[...] = _rms(h3, gfinal_ref[...])
    else:
        (h_ref,) = rest
        h_ref[...] = h3


def _ple(out_t, h1, p, g_ple, w_gate, w_proj, g_final=None):
    n, d = h1.shape
    tb = _pick_block(n, (512, 256, 128))
    final = g_final is not None
    in_specs = [pl.BlockSpec((d, tb), lambda i: (0, i)), _tok(tb, d), _tok(tb, p.shape[1]),
                _full(g_ple.shape), _full(w_gate.shape), _full(w_proj.shape)]
    args = [out_t, h1, p, g_ple, w_gate, w_proj]
    if final:
        in_specs.append(_full(g_final.shape))
        args.append(g_final)
    return pl.pallas_call(
        functools.partial(_ple_kernel, final=final),
        grid=(n // tb,),
        in_specs=in_specs,
        out_specs=_tok(tb, d),
        out_shape=jax.ShapeDtypeStruct((n, d), F32),
        compiler_params=_cparams(("parallel",)),
        name="ple",
    )(*args)


def _block_diag_tiles(w):
    g, bi, bo = w.shape
    per = MXU_TILE // bi
    wt = w.reshape(g // per, per, bi, bo)
    eye = jnp.eye(per, dtype=w.dtype)
    dense = jnp.einsum("tgio,gh->tgiho", wt, eye)
    return dense.reshape(g // per, per * bi, per * bo)


def _row(x):
    return x.reshape(1, -1)


def _prep_weights(W, depth):
    P = {}
    n_gla = W["w_gla_in"].shape[0]
    dk = W["w_gla_gate"].shape[2]
    rank = W["w_gla_gate"].shape[1]
    dv = (W["w_gla_in"].shape[2] - rank - 2 * dk) // 2
    P["gla_dims"] = (dk, dv)
    pad = 128 - rank
    P["w_gla_qkvr"] = W["w_gla_in"][:, :, :2 * dk + 2 * dv].astype(BF16)
    P["w_gla_gl"] = jnp.pad(W["w_gla_in"][:, :, 2 * dk + 2 * dv:], ((0, 0), (0, 0), (0, pad))).astype(BF16)
    P["w_gla_gate"] = jnp.pad(W["w_gla_gate"], ((0, 0), (0, pad), (0, 0))).astype(BF16)
    P["w_gla_out"] = W["w_gla_out"].astype(BF16)
    P["w_ml_up"] = W["w_ml_up"].astype(BF16)
    n_ml = W["w_ml_up"].shape[0]
    for nm in ("w_ml_q", "w_ml_k", "w_ml_v"):
        P[nm] = jnp.stack([_block_diag_tiles(W[nm][j]) for j in range(n_ml)]).astype(BF16)
    wif = jnp.concatenate([W["w_ml_igate"], W["w_ml_fgate"]], axis=-1)
    P["w_ml_if"] = jnp.pad(wif, ((0, 0), (0, 0), (0, 128 - wif.shape[-1]))).astype(BF16)
    bif = jnp.concatenate([W["b_ml_igate"], W["b_ml_fgate"]], axis=-1)
    P["b_ml_if"] = jnp.pad(bif, ((0, 0), (0, 128 - bif.shape[-1])))
    P["w_ml_down"] = W["w_ml_down"].astype(BF16)
    P["w_peer_qt"] = jnp.swapaxes(W["w_peer_query"], 1, 2).astype(BF16)
    keys = W["peer_keys"]
    P["peer_keys"] = jnp.swapaxes(keys, 1, 2).reshape(depth, 2 * PEER_HEADS, PEER_NKEYS, PEER_HALF).astype(BF16)
    P["peer_u"] = (W["peer_u"] * GELU_ARG_SCALE).astype(BF16)
    P["peer_vt"] = jnp.swapaxes(W["peer_v"], 1, 2).astype(BF16)
    P["w_ple_gate"] = W["w_ple_gate"].astype(BF16)
    P["w_ple_proj"] = W["w_ple_proj"].astype(BF16)
    del n_gla
    return P


def _run_group(x, p, s_gla, c_ml, n_ml, m_ml, buf_ml, W, P, depth):
    bsz, seqlen, d = x.shape
    n = bsz * seqlen
    h = x.reshape(n, d)
    dk, dv = P["gla_dims"]
    new_s, new_c, new_n, new_m, new_buf = [], [], [], [], []
    for i in range(depth):
        j = i // 2
        if i % 2 == 0:
            q, k, v, r, lg = _gla_pre(h, _row(W["norm_mix"][i]), P["w_gla_qkvr"][j], P["w_gla_gl"][j],
                                      P["w_gla_gate"][j], _row(W["b_gla_gate"][j]), dk, dv)
            s0 = None if s_gla is None else s_gla[j]
            y, s_fin = _gla_scan(q, k, v, lg, r, _row(W["g_gla_norm"][j]), s0, bsz, seqlen)
            new_s.append(s_fin)
            h1, xt, qt = _mix_post(y, h, P["w_gla_out"][j], _row(W["norm_ffn"][i]), P["w_peer_qt"][i])
        else:
            di = P["w_ml_up"][j].shape[1] // 2
            assert seqlen >= HALO, "the new conv state is read from one sequence's last rows"
            if buf_ml is None:
                buf = jnp.zeros((bsz, HALO, di), F32)
            else:
                buf = jnp.pad(buf_ml[j], ((0, 0), (HALO - (ML_CONV - 1), 0), (0, 0)))
            qm, km, vm, xc, z, gates, tail = _ml_in(
                h, _row(W["norm_mix"][i]), P["w_ml_up"][j], buf, W["ml_conv_w"][j], _row(W["ml_conv_b"][j]),
                P["w_ml_q"][j], P["w_ml_k"][j], P["w_ml_v"][j], P["w_ml_if"][j], _row(P["b_ml_if"][j]),
                bsz, seqlen)
            if c_ml is None:
                state = None
            else:
                state = (c_ml[j], n_ml[j][:, :, None, :],
                         jnp.broadcast_to(m_ml[j][:, :, None, None], m_ml[j].shape + (1, M_LANES)))
            y, c_fin, n_fin, m_fin = _ml_scan(qm, km, vm, gates, xc, z, _row(W["g_ml_norm"][j]),
                                              _row(W["ml_skip"][j]), state, bsz, seqlen)
            new_c.append(c_fin)
            new_n.append(n_fin[:, :, 0, :])
            new_m.append(m_fin[:, :, 0, 0])
            new_buf.append(tail[:, HALO - (ML_CONV - 1):, :])
            h1, xt, qt = _mix_post(y, h, P["w_ml_down"][j], _row(W["norm_ffn"][i]), P["w_peer_qt"][i])
        r2, e2, nsel, e1 = _peer_route(qt, P["peer_keys"][i])
        out_t = _peer_expert(xt, r2, e2, nsel, e1, P["peer_u"][i], P["peer_vt"][i])
        g_final = _row(W["norm_final"]) if i == depth - 1 else None
        h = _ple(out_t, h1, p[i].reshape(n, -1), _row(W["norm_ple"][i]), P["w_ple_gate"][i],
                 P["w_ple_proj"][i], g_final)
    y = h
    return (y.reshape(bsz, seqlen, d), jnp.stack(new_s), jnp.stack(new_c), jnp.stack(new_n),
            jnp.stack(new_m), jnp.stack(new_buf))


def kernel(x_prompt, x_sample, state_gla_S, state_mlstm_C, state_mlstm_n, state_mlstm_m, state_mlstm_conv,
           p_prompt, p_sample, w_gla_in, w_gla_gate, b_gla_gate, g_gla_norm, w_gla_out,
           w_ml_up, ml_conv_w, ml_conv_b, w_ml_q, w_ml_k, w_ml_v, w_ml_igate, b_ml_igate,
           w_ml_fgate, b_ml_fgate, g_ml_norm, ml_skip, w_ml_down,
           w_peer_query, peer_keys, peer_u, peer_v, norm_mix, norm_ffn, norm_ple,
           w_ple_gate, w_ple_proj, norm_final):
    W = dict(w_gla_in=w_gla_in, w_gla_gate=w_gla_gate, b_gla_gate=b_gla_gate, g_gla_norm=g_gla_norm,
             w_gla_out=w_gla_out, w_ml_up=w_ml_up, ml_conv_w=ml_conv_w, ml_conv_b=ml_conv_b,
             w_ml_q=w_ml_q, w_ml_k=w_ml_k, w_ml_v=w_ml_v, w_ml_igate=w_ml_igate, b_ml_igate=b_ml_igate,
             w_ml_fgate=w_ml_fgate, b_ml_fgate=b_ml_fgate, g_ml_norm=g_ml_norm, ml_skip=ml_skip,
             w_ml_down=w_ml_down, w_peer_query=w_peer_query, peer_keys=peer_keys, peer_u=peer_u,
             peer_v=peer_v, norm_mix=norm_mix, norm_ffn=norm_ffn, norm_ple=norm_ple,
             w_ple_gate=w_ple_gate, w_ple_proj=w_ple_proj, norm_final=norm_final)
    depth = norm_mix.shape[0]
    P = _prep_weights(W, depth)
    y_p, s_p, c_p, n_p, m_p, buf_p = _run_group(x_prompt, p_prompt, None, None, None, None, None, W, P, depth)
    y_s, s_s, c_s, n_s, m_s, buf_s = _run_group(x_sample, p_sample, state_gla_S, state_mlstm_C,
                                                state_mlstm_n, state_mlstm_m, state_mlstm_conv, W, P, depth)
    return (y_p, y_s, s_p, s_s, c_p, c_s, n_p, n_s, m_p, m_s, buf_p, buf_s)
```

```python
import functools

import jax
import jax.numpy as jnp
from jax import lax
from jax.experimental import pallas as pl
from jax.experimental.pallas import tpu as pltpu

F32 = jnp.float32
BF16 = jnp.bfloat16
EPS = 1e-6
CHUNK = 64
HIGHEST = lax.Precision.HIGHEST
NEG_INF = float("-inf")

GLA_HEADS = 4
GLA_GATE_NORMALIZER = 16.0
ML_HEADS = 4
ML_CONV = 4
PEER_HEADS = 8
PEER_NKEYS = 128
PEER_HALF = 64
PEER_TOPK = 16

VMEM_LIMIT_BYTES = 52 * 1024 * 1024
MXU_TILE = 256
PEER_EXPERT_STEP = 2048
PEER_EXPERT_PIECE = MXU_TILE
GELU_ARG_SCALE = 0.7071067811865476
GELU_OUT_SCALE = 0.5 / GELU_ARG_SCALE


def _cparams(sem):
    return pltpu.CompilerParams(dimension_semantics=sem, vmem_limit_bytes=VMEM_LIMIT_BYTES)


def _pick_block(n, candidates):
    for c in candidates:
        if n % c == 0:
            return c
    raise ValueError(f"no block size in {candidates} divides {n}")


def _rms(x, g):
    ms = jnp.mean(x * x, axis=-1, keepdims=True)
    return x * lax.rsqrt(ms + EPS) * g


def _log_sigmoid(x):
    return jnp.minimum(x, 0.0) - jnp.log1p(jnp.exp(-jnp.abs(x)))


def _silu(x):
    return x * jax.nn.sigmoid(x)


def _dot(a, b):
    return jnp.dot(a, b, preferred_element_type=F32)


def _tok(tb, d):
    return pl.BlockSpec((tb, d), lambda i: (i, 0))


def _full(shape):
    nd = len(shape)
    return pl.BlockSpec(shape, lambda *_: (0,) * nd)


def _gla_pre_kernel(h_ref, g_ref, w_ref, wgl_ref, wgate_ref, bgate_ref,
                    q_ref, k_ref, v_ref, r_ref, lg_ref, *, dk, dv, hk):
    xn = _rms(h_ref[...], g_ref[...]).astype(BF16)
    proj = _dot(xn, w_ref[...])
    q_ref[...] = proj[:, :dk] * (hk ** -0.5)
    k_ref[...] = proj[:, dk:2 * dk]
    v_ref[...] = proj[:, 2 * dk:2 * dk + dv].astype(BF16)
    r_ref[...] = proj[:, 2 * dk + dv:]
    gl = _dot(xn, wgl_ref[...])
    gate = _dot(gl.astype(BF16), wgate_ref[...]) + bgate_ref[...]
    lg_ref[...] = _log_sigmoid(gate) * (1.0 / GLA_GATE_NORMALIZER)


def _gla_pre(h, g, w_qkvr, w_gl, w_gate, b_gate, dk, dv):
    n, d = h.shape
    tb = _pick_block(n, (512, 256, 128, 64))
    hk = dk // GLA_HEADS
    outs = (jax.ShapeDtypeStruct((n, dk), F32), jax.ShapeDtypeStruct((n, dk), F32),
            jax.ShapeDtypeStruct((n, dv), BF16), jax.ShapeDtypeStruct((n, dv), F32),
            jax.ShapeDtypeStruct((n, dk), F32))
    return pl.pallas_call(
        functools.partial(_gla_pre_kernel, dk=dk, dv=dv, hk=hk),
        grid=(n // tb,),
        in_specs=[_tok(tb, d), _full(g.shape), _full(w_qkvr.shape), _full(w_gl.shape),
                  _full(w_gate.shape), _full(b_gate.shape)],
        out_specs=(_tok(tb, dk), _tok(tb, dk), _tok(tb, dv), _tok(tb, dv), _tok(tb, dk)),
        out_shape=outs,
        compiler_params=_cparams(("parallel",)),
        name="gla_pre",
    )(h, g, w_qkvr, w_gl, w_gate, b_gate)


GLA_SUB = 16
GLA_CHUNK = 128


def _gla_scan_kernel(*refs, nchunks, chunk, zero_init):
    CHUNK = chunk
    if zero_init:
        q_ref, k_ref, v_ref, lg_ref, r_ref, gn_ref, y_ref, sout_ref, st_ref = refs
        s0_ref = None
    else:
        q_ref, k_ref, v_ref, lg_ref, r_ref, gn_ref, s0_ref, y_ref, sout_ref, st_ref = refs
    t = pl.program_id(1)
    nheads, hv, hk = st_ref.shape

    @pl.when(t == 0)
    def _():
        if zero_init:
            st_ref[...] = jnp.zeros_like(st_ref)
        else:
            for h in range(nheads):
                st_ref[h] = s0_ref[h].T

    row = lax.broadcasted_iota(jnp.int32, (CHUNK, CHUNK), 0)
    col = lax.broadcasted_iota(jnp.int32, (CHUNK, CHUNK), 1)
    tril = (row >= col).astype(F32)
    gn = gn_ref[...]
    nsub = CHUNK // GLA_SUB

    def bmm(a, b, lhs_c, rhs_c):
        return lax.dot_general(a, b, (((lhs_c,), (rhs_c,)), ((0,), (0,))), preferred_element_type=F32)

    pairs = [(c, h) for c in range(nchunks) for h in range(nheads)]
    b_chunks = [jnp.dot(tril, lg_ref[pl.ds(c * CHUNK, CHUNK), :], precision=HIGHEST,
                        preferred_element_type=F32) for c in range(nchunks)]
    b = jnp.stack([b_chunks[c][:, h * hk:(h + 1) * hk] for c, h in pairs])
    q = jnp.stack([q_ref[pl.ds(c * CHUNK, CHUNK), h * hk:(h + 1) * hk] for c, h in pairs])
    k = jnp.stack([k_ref[pl.ds(c * CHUNK, CHUNK), h * hk:(h + 1) * hk] for c, h in pairs])
    v = jnp.stack([v_ref[pl.ds(c * CHUNK, CHUNK), h * hv:(h + 1) * hv] for c, h in pairs])
    b_last = b[:, CHUNK - 1:CHUNK, :]
    intra_parts = []
    for i in range(nsub):
        lo, hi = i * GLA_SUB, (i + 1) * GLA_SUB
        b_ref_row = b[:, lo:lo + 1, :]
        qe = (q[:, lo:hi, :] * jnp.exp(b[:, lo:hi, :] - b_ref_row)).astype(BF16)
        ke = (k[:, :hi, :] * jnp.exp(b_ref_row - b[:, :hi, :])).astype(BF16)
        att = bmm(qe, ke, 2, 2)
        r_i = lax.broadcasted_iota(jnp.int32, (GLA_SUB, hi), 0) + lo
        c_i = lax.broadcasted_iota(jnp.int32, (GLA_SUB, hi), 1)
        att = jnp.where(r_i >= c_i, att, 0.0)
        intra_parts.append(bmm(att.astype(BF16), v[:, :hi, :], 2, 1))
    intra = jnp.concatenate(intra_parts, axis=1)
    kv = bmm(v, (k * jnp.exp(b_last - b)).astype(BF16), 1, 1)
    q_dec = (q * jnp.exp(b)).astype(BF16)
    decay = jnp.exp(b_last)
    st = st_ref[...]
    for c in range(nchunks):
        ps = slice(c * nheads, (c + 1) * nheads)
        o = bmm(q_dec[ps], st.astype(BF16), 2, 2) + intra[ps]
        st = st * decay[ps] + kv[ps]
        o = o * lax.rsqrt(jnp.mean(o * o, axis=-1, keepdims=True) + EPS) * gn
        for h in range(nheads):
            cs = (pl.ds(c * CHUNK, CHUNK), slice(h * hv, (h + 1) * hv))
            y_ref[cs] = (o[h] * _silu(r_ref[cs])).astype(BF16)
    st_ref[...] = st

    @pl.when(t == pl.num_programs(1) - 1)
    def _():
        for h in range(nheads):
            sout_ref[h] = st_ref[h].T


def _gla_scan(q, k, v, lg, r, gn, s0, nseq, seqlen):
    n, dk = q.shape
    dv = v.shape[1]
    hk, hv = dk // GLA_HEADS, dv // GLA_HEADS
    tt = _pick_block(seqlen, (512, 256, 128, 64))
    chunk = min(tt, GLA_CHUNK)
    nt = seqlen // tt
    zero_init = s0 is None
    tmap = lambda s, t: (s * nt + t, 0)
    smap = lambda s, t: (s, 0, 0, 0)
    in_specs = [pl.BlockSpec((tt, dk), tmap), pl.BlockSpec((tt, dk), tmap),
                pl.BlockSpec((tt, dv), tmap), pl.BlockSpec((tt, dk), tmap), pl.BlockSpec((tt, dv), tmap),
                pl.BlockSpec((1, hv), lambda s, t: (0, 0))]
    args = [q, k, v, lg, r, gn]
    if not zero_init:
        in_specs.append(pl.BlockSpec((None, GLA_HEADS, hk, hv), smap))
        args.append(s0)
    return pl.pallas_call(
        functools.partial(_gla_scan_kernel, nchunks=tt // chunk, chunk=chunk, zero_init=zero_init),
        grid=(nseq, nt),
        in_specs=in_specs,
        out_specs=(pl.BlockSpec((tt, dv), tmap), pl.BlockSpec((None, GLA_HEADS, hk, hv), smap)),
        out_shape=(jax.ShapeDtypeStruct((n, dv), BF16),
                   jax.ShapeDtypeStruct((nseq, GLA_HEADS, hk, hv), F32)),
        scratch_shapes=[pltpu.VMEM((GLA_HEADS, hv, hk), F32)],
        compiler_params=_cparams(("parallel", "arbitrary")),
        name="gla_scan",
    )(*args)


def _peer_pre_tail(h1, gffn_ref, wqt_ref, xt_ref, qt_ref):
    xn = _rms(h1, gffn_ref[...])
    xt = xn.T.astype(BF16)
    xt_ref[...] = xt
    qt_ref[...] = _dot(wqt_ref[...], xt).astype(BF16)


def _mix_post_kernel(y_ref, h_ref, w_ref, gffn_ref, wqt_ref, h1_ref, xt_ref, qt_ref):
    h1 = h_ref[...] + _dot(y_ref[...], w_ref[...])
    h1_ref[...] = h1
    _peer_pre_tail(h1, gffn_ref, wqt_ref, xt_ref, qt_ref)


def _mix_post(y, h, w_out, g_ffn, w_qt):
    n, d = h.shape
    tb = _pick_block(n, (512, 256, 128))
    dq = w_qt.shape[0]
    return pl.pallas_call(
        _mix_post_kernel,
        grid=(n // tb,),
        in_specs=[_tok(tb, y.shape[1]), _tok(tb, d), _full(w_out.shape), _full(g_ffn.shape), _full(w_qt.shape)],
        out_specs=(_tok(tb, d), pl.BlockSpec((d, tb), lambda i: (0, i)),
                   pl.BlockSpec((dq, tb), lambda i: (0, i))),
        out_shape=(jax.ShapeDtypeStruct((n, d), F32), jax.ShapeDtypeStruct((d, n), BF16),
                   jax.ShapeDtypeStruct((dq, n), BF16)),
        compiler_params=_cparams(("parallel",)),
        name="mix_post",
    )(y, h, w_out, g_ffn, w_qt)


HALO = 8


def _ml_in_kernel(h_ref, g_ref, wup_ref, buf_ref, cw_ref, cb_ref, wq_ref, wk_ref, wv_ref, wif_ref, bif_ref,
                  q_ref, k_ref, v_ref, xc_ref, z_ref, gates_ref, tail_ref, xp_scr, *, tt, di, hd):
    t = pl.program_id(1)

    @pl.when(t == 0)
    def _():
        xp_scr[0:HALO, :] = buf_ref[...]

    @pl.when(t > 0)
    def _():
        xp_scr[0:HALO, :] = xp_scr[tt:tt + HALO, :]

    xn = _rms(h_ref[...], g_ref[...]).astype(BF16)
    up = _dot(xn, wup_ref[...])
    x = up[:, :di]
    z_ref[...] = up[:, di:]
    xp_scr[HALO:HALO + tt, :] = x
    tail_ref[...] = x[tt - HALO:, :]
    y = cb_ref[...]
    for j in range(ML_CONV):
        y = y + cw_ref[j:j + 1, :] * xp_scr[pl.ds(HALO - (ML_CONV - 1) + j, tt), :]
    xc = _silu(y)
    xc_ref[...] = xc
    xcb = xc.astype(BF16)
    xmb = x.astype(BF16)
    gates = jnp.zeros((tt, gates_ref.shape[1]), F32) + bif_ref[...]
    ntile = di // MXU_TILE
    for j in range(ntile):
        cs = slice(j * MXU_TILE, (j + 1) * MXU_TILE)
        qj = _dot(xcb[:, cs], wq_ref[j])
        kj = _dot(xcb[:, cs], wk_ref[j])
        vj = _dot(xmb[:, cs], wv_ref[j])
        q_ref[:, cs] = qj.astype(BF16)
        k_ref[:, cs] = kj * (hd ** -0.5)
        v_ref[:, cs] = vj.astype(BF16)
        gates = gates + _dot(qj.astype(BF16), wif_ref[j * MXU_TILE:(j + 1) * MXU_TILE, :])
        gates = gates + _dot(kj.astype(BF16), wif_ref[di + j * MXU_TILE:di + (j + 1) * MXU_TILE, :])
        gates = gates + _dot(vj.astype(BF16), wif_ref[2 * di + j * MXU_TILE:2 * di + (j + 1) * MXU_TILE, :])
    lane = lax.broadcasted_iota(jnp.int32, gates.shape, 1)
    is_f = (lane >= ML_HEADS) & (lane < 2 * ML_HEADS)
    gates_ref[...] = jnp.where(is_f, _log_sigmoid(gates), gates)


def _const(shape):
    nd = len(shape)
    return pl.BlockSpec(shape, lambda *_: (0,) * nd, pipeline_mode=pl.Buffered(1))


def _ml_in(h, g, w_up, buf, conv_w, conv_b, wq, wk, wv, wif, bif, nseq, seqlen):
    n, d = h.shape
    di = w_up.shape[1] // 2
    hd = di // ML_HEADS
    tt = _pick_block(seqlen, (256, 128, 64))
    nt = seqlen // tt
    tmap = lambda s, t: (s * nt + t, 0)
    smap = lambda s, t: (s, 0, 0)
    ng = wif.shape[1]
    tok = lambda width: pl.BlockSpec((tt, width), tmap)
    return pl.pallas_call(
        functools.partial(_ml_in_kernel, tt=tt, di=di, hd=hd),
        grid=(nseq, nt),
        in_specs=[tok(d), _const(g.shape), _const(w_up.shape), pl.BlockSpec((None, HALO, di), smap),
                  _const(conv_w.shape), _const(conv_b.shape), _const(wq.shape), _const(wk.shape),
                  _const(wv.shape), _const(wif.shape), _const(bif.shape)],
        out_specs=(tok(di), tok(di), tok(di), tok(di), tok(di), tok(ng),
                   pl.BlockSpec((None, HALO, di), smap)),
        out_shape=(jax.ShapeDtypeStruct((n, di), BF16), jax.ShapeDtypeStruct((n, di), F32),
                   jax.ShapeDtypeStruct((n, di), BF16), jax.ShapeDtypeStruct((n, di), F32),
                   jax.ShapeDtypeStruct((n, di), F32), jax.ShapeDtypeStruct((n, ng), F32),
                   jax.ShapeDtypeStruct((nseq, HALO, di), F32)),
        scratch_shapes=[pltpu.VMEM((HALO + tt, di), F32)],
        compiler_params=_cparams(("parallel", "arbitrary")),
        name="ml_in",
    )(h, g, w_up, buf, conv_w, conv_b, wq, wk, wv, wif, bif)


def _ml_scan_kernel(*refs, nchunks, blk, zero_init):
    if zero_init:
        (q_ref, k_ref, v_ref, g_ref, xc_ref, z_ref, gn_ref, skip_ref,
         y_ref, cout_ref, nout_ref, mout_ref, c_scr, n_scr, m_scr) = refs
    else:
        (q_ref, k_ref, v_ref, g_ref, xc_ref, z_ref, gn_ref, skip_ref, c0_ref, n0_ref, m0_ref,
         y_ref, cout_ref, nout_ref, mout_ref, c_scr, n_scr, m_scr) = refs
    t = pl.program_id(1)
    nheads, hd = c_scr.shape[0], c_scr.shape[1]

    @pl.when(t == 0)
    def _():
        if zero_init:
            c_scr[...] = jnp.zeros_like(c_scr)
            n_scr[...] = jnp.zeros_like(n_scr)
            m_scr[...] = jnp.zeros_like(m_scr)
        else:
            c_scr[...] = c0_ref[...]
            n_scr[...] = n0_ref[...]
            m_scr[...] = m0_ref[...]

    row = lax.broadcasted_iota(jnp.int32, (blk, blk), 0)
    col = lax.broadcasted_iota(jnp.int32, (blk, blk), 1)
    causal = row >= col
    tril = causal.astype(F32)
    lanes = g_ref.shape[1]

    def bmm(a, b, lhs_c, rhs_c, precision=None):
        return lax.dot_general(a, b, (((lhs_c,), (rhs_c,)), ((0,), (0,))), precision=precision,
                               preferred_element_type=F32)

    shared = {}
    for c, h0 in [(c, h0) for c in range(nchunks) for h0 in range(0, nheads, ML_HEAD_BATCH)]:
        heads = range(h0, h0 + ML_HEAD_BATCH)
        hsl = slice(h0, h0 + ML_HEAD_BATCH)
        sl = pl.ds(c * blk, blk)
        q = jnp.stack([q_ref[sl, h * hd:(h + 1) * hd] for h in heads])
        kf = jnp.stack([k_ref[sl, h * hd:(h + 1) * hd] for h in heads])
        v = jnp.stack([v_ref[sl, h * hd:(h + 1) * hd] for h in heads])
        cmat = c_scr[hsl]
        nvec = n_scr[hsl]
        m = m_scr[hsl][:, :, 0:1]
        if c not in shared:
            g = g_ref[sl, :]
            gcum = jnp.dot(tril, g, precision=HIGHEST, preferred_element_type=F32)
            diff_t = (g - pltpu.roll(gcum, lanes - nheads, axis=1)).T if blk % lanes == 0 else None
            shared[c] = (g, gcum, diff_t)
        g, gcum, diff_t = shared[c]
        i_col = jnp.stack([g[:, h:h + 1] for h in heads])
        f_col = jnp.stack([gcum[:, h + nheads:h + nheads + 1] for h in heads])
        if diff_t is not None:
            w_row = jnp.stack([diff_t[h:h + 1, :] for h in heads])
        else:
            eye = row == col
            w_row = bmm(jnp.ones((ML_HEAD_BATCH, blk, blk), F32), jnp.where(eye, i_col - f_col, 0.0), 2, 1,
                        precision=HIGHEST)
        log_d = jnp.where(causal, f_col + w_row, NEG_INF)
        inter = f_col + m
        mt = jnp.maximum(inter, jnp.max(log_d, axis=-1, keepdims=True))
        sc = bmm(q, kf.astype(BF16), 2, 2) * jnp.exp(log_d - mt)
        a = jnp.exp(inter - mt)
        num = a * bmm(q, cmat.astype(BF16), 2, 1) + bmm(sc.astype(BF16), v, 2, 1)
        qn = jnp.sum(q.astype(F32) * nvec, axis=-1, keepdims=True)
        den = a * qn + jnp.sum(sc, axis=-1, keepdims=True)
        hh = num / jnp.maximum(jnp.abs(den), jnp.exp(-mt))
        m_new = mt[:, blk - 1:blk, :]
        f_last = f_col[:, blk - 1:blk, :]
        w_end = jnp.exp(f_last - f_col + i_col - m_new)
        a_end = jnp.exp(f_last + m - m_new)
        kw = kf * w_end
        c_scr[hsl] = a_end * cmat + bmm(kw.astype(BF16), v, 1, 1)
        n_scr[hsl] = a_end * nvec + jnp.sum(kw, axis=1, keepdims=True)
        m_scr[hsl] = jnp.broadcast_to(m_new, (ML_HEAD_BATCH, 1, M_LANES))
        mu = jnp.mean(hh, axis=-1, keepdims=True)
        var = jnp.mean(jnp.square(hh - mu), axis=-1, keepdims=True)
        hn = (hh - mu) * lax.rsqrt(var + EPS)
        for i, h in enumerate(heads):
            cs = (sl, slice(h * hd, (h + 1) * hd))
            cols = cs[1]
            y = (hn[i] * gn_ref[:, cols] + skip_ref[:, cols] * xc_ref[cs]) * _silu(z_ref[cs])
            y_ref[cs] = y.astype(BF16)

    @pl.when(t == pl.num_programs(1) - 1)
    def _():
        cout_ref[...] = c_scr[...]
        nout_ref[...] = n_scr[...]
        mout_ref[...] = m_scr[...]


M_LANES = 128
ML_SCAN_BLOCK = 256
ML_HEAD_BATCH = 2


def _ml_scan(q, k, v, gates, xc, z, g_norm, skip, state, nseq, seqlen):
    n, di = q.shape
    hd = di // ML_HEADS
    tt = _pick_block(seqlen, (256, 128, 64))
    blk = min(tt, ML_SCAN_BLOCK)
    nt = seqlen // tt
    zero_init = state is None
    tmap = lambda s, t: (s * nt + t, 0)
    smap = lambda s, t: (s, 0, 0, 0)
    ng = gates.shape[1]
    nh = ML_HEADS
    tok = pl.BlockSpec((tt, di), tmap)
    in_specs = [tok, tok, tok, pl.BlockSpec((tt, ng), tmap), tok, tok,
                _const(g_norm.shape), _const(skip.shape)]
    args = [q, k, v, gates, xc, z, g_norm, skip]
    state_specs = [pl.BlockSpec((None, nh, hd, hd), smap), pl.BlockSpec((None, nh, 1, hd), smap),
                   pl.BlockSpec((None, nh, 1, M_LANES), smap)]
    if not zero_init:
        in_specs += state_specs
        args += list(state)
    return pl.pallas_call(
        functools.partial(_ml_scan_kernel, nchunks=tt // blk, blk=blk, zero_init=zero_init),
        grid=(nseq, nt),
        in_specs=in_specs,
        out_specs=(tok, *state_specs),
        out_shape=(jax.ShapeDtypeStruct((n, di), BF16),
                   jax.ShapeDtypeStruct((nseq, nh, hd, hd), F32),
                   jax.ShapeDtypeStruct((nseq, nh, 1, hd), F32),
                   jax.ShapeDtypeStruct((nseq, nh, 1, M_LANES), F32)),
        scratch_shapes=[pltpu.VMEM((nh, hd, hd), F32), pltpu.VMEM((nh, 1, hd), F32),
                        pltpu.VMEM((nh, 1, M_LANES), F32)],
        compiler_params=_cparams(("parallel", "arbitrary")),
        name="ml_scan",
    )(*args)


NCAND_ROWS = 72


def _first_max(vals, ids, big):
    m = jnp.max(vals, axis=0, keepdims=True)
    sel = jnp.min(jnp.where(vals == m, ids, big), axis=0, keepdims=True)
    return m, sel


def _sorting_network(n):
    size = 1
    while size < n:
        size *= 2
    pairs = []
    p = 1
    while p < size:
        k = p
        while k >= 1:
            for j in range(k % p, size - k, 2 * k):
                for i in range(min(k, size - j - k)):
                    if (i + j) // (2 * p) == (i + j + k) // (2 * p):
                        pairs.append((i + j, i + j + k))
            k //= 2
        p *= 2
    return [(i, j) for i, j in pairs if j < n]


SUBLANES = 8
PACKED_SUBLANES = 16


def _sorted_best16(s):
    k16 = PEER_TOPK
    rows, lanes = s.shape
    nslab = rows // SUBLANES
    v = [s[SUBLANES * j:SUBLANES * (j + 1), :] for j in range(nslab)]
    for i, j in _sorting_network(nslab):
        v[i], v[j] = jnp.maximum(v[i], v[j]), jnp.minimum(v[i], v[j])
    v = v[:k16] + [jnp.full((SUBLANES, lanes), NEG_INF, F32)] * (k16 - nslab)
    for shift in (1, 2, 4):
        m = [jnp.maximum(v[i], pltpu.roll(v[k16 - 1 - i], shift, axis=0)) for i in range(k16)]
        d = k16 // 2
        while d >= 1:
            for i in range(k16):
                if (i // d) % 2 == 0:
                    m[i], m[i + d] = jnp.maximum(m[i], m[i + d]), jnp.minimum(m[i], m[i + d])
            d //= 2
        v = m
    return v


def _tie_flag(tops, member):
    tie = jnp.zeros_like(tops[0])
    for r in range(len(tops) - 1):
        tie = jnp.maximum(tie, (tops[r] == tops[r + 1]).astype(F32))
    count = jnp.sum(member.astype(F32), axis=0, keepdims=True)
    return jnp.maximum(jnp.max(tie, axis=0, keepdims=True), (count != float(PEER_TOPK)).astype(F32))


def _peer_route_kernel(qt_ref, keys_ref, r2_ref, e2_ref, n_ref, e1_ref, s_scr, rank_scr, top_scr, *, tk):
    nk, k16 = PEER_NKEYS, PEER_TOPK
    key_id = lax.broadcasted_iota(jnp.int32, (nk, tk), 0)
    slot_id = lax.broadcasted_iota(jnp.int32, (k16, tk), 0)

    def half_body(hc, carry):
        start = pl.multiple_of(hc * PEER_HALF, PEER_HALF)
        s = _dot(keys_ref[hc], qt_ref[pl.ds(start, PEER_HALF), :])
        s_scr[hc] = s

        tops = _sorted_best16(s)
        s3 = s.reshape(nk // SUBLANES, SUBLANES, tk)
        member = s3 >= tops[k16 - 1][None]
        tie = _tie_flag(tops, member.reshape(nk, tk))

        def by_value():
            rank = jnp.where(member, float(k16 - 1), float(k16))
            for r in range(k16 - 2, -1, -1):
                rank = jnp.where(s3 >= tops[r][None], float(r), rank)
            sub = lax.broadcasted_iota(jnp.int32, (SUBLANES, tk), 0)
            halves = []
            for base in range(0, k16, SUBLANES):
                slab = tops[base]
                for r in range(1, SUBLANES):
                    slab = jnp.where(sub == r, tops[base + r], slab)
                halves.append(slab)
            return rank.reshape(nk, tk), jnp.concatenate(halves, axis=0)

        def by_value_and_index():
            def pick(r, st):
                s_cur, rank, top = st
                m, sel = _first_max(s_cur, key_id, nk)
                hit = key_id == sel
                return (jnp.where(hit, NEG_INF, s_cur), jnp.where(hit, jnp.asarray(r, F32), rank),
                        jnp.where(slot_id == r, m, top))

            _, rank, top = lax.fori_loop(
                0, k16, pick, (s, jnp.full((nk, tk), float(k16), F32), jnp.zeros((k16, tk), F32)))
            return rank, top

        rank, top = lax.cond(jnp.max(tie) > 0.0, by_value_and_index, by_value)
        rank_scr[hc] = rank
        top_scr[hc] = top
        return carry

    lax.fori_loop(0, 2 * PEER_HEADS, half_body, 0)

    def cand_ids():
        i16 = lax.broadcasted_iota(jnp.int32, (16, tk), 0)
        i8 = lax.broadcasted_iota(jnp.int32, (8, tk), 0)
        parts = [i16, 16 + i8, 32 + i8, 48 + i8, 16 * i16, 16 * i8 + 1, 16 * i8 + 2]
        return jnp.concatenate(parts, axis=0)

    cand_id = cand_ids()
    row72 = lax.broadcasted_iota(jnp.int32, (NCAND_ROWS, tk), 0)
    cand_ok = (row72 < 40) | (cand_id >= 64)
    big_id = 16 * 16

    def head_body(h, carry):
        a = top_scr[2 * h]
        b = top_scr[2 * h + 1]
        parts = [a[0:1, :] + b, a[1:2, :] + b[0:8, :], a[2:3, :] + b[0:8, :], a[3:4, :] + b[0:8, :],
                 a + b[0:1, :], a[0:8, :] + b[1:2, :], a[0:8, :] + b[2:3, :]]
        cand = jnp.where(cand_ok, jnp.concatenate(parts, axis=0), NEG_INF)
        tops = _sorted_best16(cand)
        cand3 = cand.reshape(NCAND_ROWS // SUBLANES, SUBLANES, tk)
        sel = (cand3 >= tops[k16 - 1][None]).reshape(NCAND_ROWS, tk)
        tie = _tie_flag(tops, sel)

        def by_value():
            self = sel.astype(F32)
            cnt = self[40:56, :] + jnp.concatenate(
                [self[56:64, :] + self[64:72, :], jnp.zeros((8, tk), F32)], axis=0)
            for r1, (lo, hi) in enumerate(((0, 16), (16, 24), (24, 32), (32, 40))):
                row = jnp.sum(self[lo:hi, :], axis=0, keepdims=True)
                cnt = cnt + jnp.where(slot_id == r1, row, 0.0)
            ex = jnp.exp(cand3 - tops[0][None]).reshape(NCAND_ROWS, tk)
            z = jnp.sum(jnp.where(sel, ex, 0.0), axis=0, keepdims=True)
            return cnt, z

        def by_value_and_index():
            def pick(r, st):
                cand_cur, cnt, tsel = st
                m, sel = _first_max(cand_cur, cand_id, big_id)
                cand_cur = jnp.where(cand_id == sel, NEG_INF, cand_cur)
                cnt = cnt + (slot_id == (sel >> 4)).astype(F32)
                return cand_cur, cnt, jnp.where(slot_id == r, m, tsel)

            _, cnt, tsel = lax.fori_loop(
                0, k16, pick, (cand, jnp.zeros((k16, tk), F32), jnp.zeros((k16, tk), F32)))
            return cnt, jnp.sum(jnp.exp(tsel - tsel[0:1, :]), axis=0, keepdims=True)

        cnt, z = lax.cond(jnp.max(tie) > 0.0, by_value_and_index, by_value)
        rank1 = rank_scr[2 * h]
        rank2 = rank_scr[2 * h + 1]
        pk = PACKED_SUBLANES
        rank1b = rank1.astype(BF16).reshape(nk // pk, pk, tk)
        n_sel = jnp.zeros((nk // pk, pk, tk), BF16)
        for r in range(k16):
            row = jnp.broadcast_to(cnt[r:r + 1, :], (pk, tk)).astype(BF16)
            n_sel = jnp.where(rank1b == r, row[None], n_sel)
        n_ref[h] = n_sel.reshape(nk, tk).astype(F32)
        e1_ref[h] = jnp.where(rank1 < k16, jnp.exp(s_scr[2 * h] - a[0:1, :]) * (GELU_OUT_SCALE / z), 0.0)
        r2_ref[h] = rank2.astype(BF16)
        e2_ref[h] = jnp.where(rank2 < k16, jnp.exp(s_scr[2 * h + 1] - b[0:1, :]), 0.0).astype(BF16)
        return carry

    lax.fori_loop(0, PEER_HEADS, head_body, 0)


def _peer_route(qt, keys):
    dq, n = qt.shape
    tk = _pick_block(n, (512, 256, 128))
    nk = PEER_NKEYS
    shp = jax.ShapeDtypeStruct((PEER_HEADS, nk, n), F32)
    shp16 = jax.ShapeDtypeStruct((PEER_HEADS, nk, n), BF16)
    ospec = pl.BlockSpec((PEER_HEADS, nk, tk), lambda i: (0, 0, i))
    return pl.pallas_call(
        functools.partial(_peer_route_kernel, tk=tk),
        grid=(n // tk,),
        in_specs=[pl.BlockSpec((dq, tk), lambda i: (0, i)), _full(keys.shape)],
        out_specs=(ospec, ospec, ospec, ospec),
        out_shape=(shp16, shp16, shp, shp),
        scratch_shapes=[pltpu.VMEM((2 * PEER_HEADS, nk, tk), F32), pltpu.VMEM((2 * PEER_HEADS, nk, tk), F32),
                        pltpu.VMEM((2 * PEER_HEADS, PEER_TOPK, tk), F32)],
        compiler_params=_cparams(("parallel",)),
        name="peer_route",
    )(qt, keys)


def _peer_gate_stage(row0, act_ref, coef_ref, r2_ref, e2_ref, n_ref, e1_ref):
    nk = PEER_NKEYS
    t = act_ref.shape[1]
    pk = PACKED_SUBLANES
    zero = jnp.zeros((), BF16)
    for al in range(PEER_EXPERT_PIECE // nk):
        a = row0 + al
        rows = slice(al * nk, (al + 1) * nk)
        y = act_ref[rows, :]
        ge = (y * (1.0 + lax.erf(y))).astype(BF16).reshape(nk // pk, pk, t)
        g = None
        for h in range(PEER_HEADS):
            n16 = jnp.broadcast_to(n_ref[h, a:a + 1, :], (pk, t)).astype(BF16)
            e16 = jnp.broadcast_to(e1_ref[h, a:a + 1, :], (pk, t)).astype(BF16)
            r2h = r2_ref[h].reshape(nk // pk, pk, t)
            e2h = e2_ref[h].reshape(nk // pk, pk, t)
            term = jnp.where(r2h < n16[None], e2h, zero) * e16[None]
            g = term if g is None else g + term
        coef_ref[rows, :] = (g * ge).reshape(nk, t)


def _peer_expert_kernel(xt_ref, r2_ref, e2_ref, n_ref, e1_ref, u_ref, vt_ref, out_ref, act_scr, coef_scr):
    k = pl.program_id(1)
    ep = PEER_EXPERT_PIECE
    npiece = act_scr.shape[0]
    rows = ep // PEER_NKEYS

    @pl.when(k == 0)
    def _():
        out_ref[...] = jnp.zeros_like(out_ref)

    xt = xt_ref[...]
    nk = PEER_NKEYS
    for p in range(npiece):
        for al in range(rows):
            act_scr[p, al * nk:(al + 1) * nk, :] = _dot(u_ref[p * ep + al * nk:p * ep + (al + 1) * nk, :], xt)
    for p in range(npiece):
        _peer_gate_stage(p * rows, act_scr.at[p], coef_scr.at[p], r2_ref, e2_ref, n_ref, e1_ref)
    for p in range(npiece):
        out_ref[...] += _dot(vt_ref[:, p * ep:(p + 1) * ep], coef_scr[p])


def _peer_expert(xt, r2, e2, nsel, e1, u, vt):
    d, n = xt.shape
    ne = u.shape[0]
    t = _pick_block(n, (512, 256, 128))
    et = PEER_EXPERT_STEP
    npiece = et // PEER_EXPERT_PIECE
    rspec = pl.BlockSpec((PEER_HEADS, PEER_NKEYS, t), lambda i, k: (0, 0, i))
    row_spec = pl.BlockSpec((PEER_HEADS, et // PEER_NKEYS, t), lambda i, k: (0, k, i))
    return pl.pallas_call(
        _peer_expert_kernel,
        grid=(n // t, ne // et),
        in_specs=[pl.BlockSpec((d, t), lambda i, k: (0, i)), rspec, rspec, row_spec, row_spec,
                  pl.BlockSpec((et, d), lambda i, k: (k, 0)),
                  pl.BlockSpec((d, et), lambda i, k: (0, k))],
        out_specs=pl.BlockSpec((d, t), lambda i, k: (0, i)),
        out_shape=jax.ShapeDtypeStruct((d, n), F32),
        scratch_shapes=[pltpu.VMEM((npiece, PEER_EXPERT_PIECE, t), F32),
                        pltpu.VMEM((npiece, PEER_EXPERT_PIECE, t), BF16)],
        compiler_params=_cparams(("parallel", "arbitrary")),
        name="peer_expert",
    )(xt, r2, e2, nsel, e1, u, vt)


def _ple_kernel(ot_ref, h1_ref, p_ref, gple_ref, wgate_ref, wproj_ref, *rest, final):
    h2 = h1_ref[...] + ot_ref[...].T
    gate = jax.nn.sigmoid(_dot(_rms(h2, gple_ref[...]).astype(BF16), wgate_ref[...]))
    h3 = h2 + gate * _dot(p_ref[...].astype(BF16), wproj_ref[...])
    if final:
        gfinal_ref, y_ref = rest
        y_ref[...] = _rms(h3, gfinal_ref[...])
    else:
        (h_ref,) = rest
        h_ref[...] = h3


def _ple(out_t, h1, p, g_ple, w_gate, w_proj, g_final=None):
    n, d = h1.shape
    tb = _pick_block(n, (512, 256, 128))
    final = g_final is not None
    in_specs = [pl.BlockSpec((d, tb), lambda i: (0, i)), _tok(tb, d), _tok(tb, p.shape[1]),
                _full(g_ple.shape), _full(w_gate.shape), _full(w_proj.shape)]
    args = [out_t, h1, p, g_ple, w_gate, w_proj]
    if final:
        in_specs.append(_full(g_final.shape))
        args.append(g_final)
    return pl.pallas_call(
        functools.partial(_ple_kernel, final=final),
        grid=(n // tb,),
        in_specs=in_specs,
        out_specs=_tok(tb, d),
        out_shape=jax.ShapeDtypeStruct((n, d), F32),
        compiler_params=_cparams(("parallel",)),
        name="ple",
    )(*args)


def _block_diag_tiles(w):
    g, bi, bo = w.shape
    per = MXU_TILE // bi
    wt = w.reshape(g // per, per, bi, bo)
    eye = jnp.eye(per, dtype=w.dtype)
    dense = jnp.einsum("tgio,gh->tgiho", wt, eye)
    return dense.reshape(g // per, per * bi, per * bo)


def _row(x):
    return x.reshape(1, -1)


def _prep_weights(W, depth):
    P = {}
    n_gla = W["w_gla_in"].shape[0]
    dk = W["w_gla_gate"].shape[2]
    rank = W["w_gla_gate"].shape[1]
    dv = (W["w_gla_in"].shape[2] - rank - 2 * dk) // 2
    P["gla_dims"] = (dk, dv)
    pad = 128 - rank
    P["w_gla_qkvr"] = W["w_gla_in"][:, :, :2 * dk + 2 * dv].astype(BF16)
    P["w_gla_gl"] = jnp.pad(W["w_gla_in"][:, :, 2 * dk + 2 * dv:], ((0, 0), (0, 0), (0, pad))).astype(BF16)
    P["w_gla_gate"] = jnp.pad(W["w_gla_gate"], ((0, 0), (0, pad), (0, 0))).astype(BF16)
    P["w_gla_out"] = W["w_gla_out"].astype(BF16)
    P["w_ml_up"] = W["w_ml_up"].astype(BF16)
    n_ml = W["w_ml_up"].shape[0]
    for nm in ("w_ml_q", "w_ml_k", "w_ml_v"):
        P[nm] = jnp.stack([_block_diag_tiles(W[nm][j]) for j in range(n_ml)]).astype(BF16)
    wif = jnp.concatenate([W["w_ml_igate"], W["w_ml_fgate"]], axis=-1)
    P["w_ml_if"] = jnp.pad(wif, ((0, 0), (0, 0), (0, 128 - wif.shape[-1]))).astype(BF16)
    bif = jnp.concatenate([W["b_ml_igate"], W["b_ml_fgate"]], axis=-1)
    P["b_ml_if"] = jnp.pad(bif, ((0, 0), (0, 128 - bif.shape[-1])))
    P["w_ml_down"] = W["w_ml_down"].astype(BF16)
    P["w_peer_qt"] = jnp.swapaxes(W["w_peer_query"], 1, 2).astype(BF16)
    keys = W["peer_keys"]
    P["peer_keys"] = jnp.swapaxes(keys, 1, 2).reshape(depth, 2 * PEER_HEADS, PEER_NKEYS, PEER_HALF).astype(BF16)
    P["peer_u"] = (W["peer_u"] * GELU_ARG_SCALE).astype(BF16)
    P["peer_vt"] = jnp.swapaxes(W["peer_v"], 1, 2).astype(BF16)
    P["w_ple_gate"] = W["w_ple_gate"].astype(BF16)
    P["w_ple_proj"] = W["w_ple_proj"].astype(BF16)
    del n_gla
    return P


def _run_group(x, p, s_gla, c_ml, n_ml, m_ml, buf_ml, W, P, depth):
    bsz, seqlen, d = x.shape
    n = bsz * seqlen
    h = x.reshape(n, d)
    dk, dv = P["gla_dims"]
    new_s, new_c, new_n, new_m, new_buf = [], [], [], [], []
    for i in range(depth):
        j = i // 2
        if i % 2 == 0:
            q, k, v, r, lg = _gla_pre(h, _row(W["norm_mix"][i]), P["w_gla_qkvr"][j], P["w_gla_gl"][j],
                                      P["w_gla_gate"][j], _row(W["b_gla_gate"][j]), dk, dv)
            s0 = None if s_gla is None else s_gla[j]
            y, s_fin = _gla_scan(q, k, v, lg, r, _row(W["g_gla_norm"][j]), s0, bsz, seqlen)
            new_s.append(s_fin)
            h1, xt, qt = _mix_post(y, h, P["w_gla_out"][j], _row(W["norm_ffn"][i]), P["w_peer_qt"][i])
        else:
            di = P["w_ml_up"][j].shape[1] // 2
            assert seqlen >= HALO, "the new conv state is read from one sequence's last rows"
            if buf_ml is None:
                buf = jnp.zeros((bsz, HALO, di), F32)
            else:
                buf = jnp.pad(buf_ml[j], ((0, 0), (HALO - (ML_CONV - 1), 0), (0, 0)))
            qm, km, vm, xc, z, gates, tail = _ml_in(
                h, _row(W["norm_mix"][i]), P["w_ml_up"][j], buf, W["ml_conv_w"][j], _row(W["ml_conv_b"][j]),
                P["w_ml_q"][j], P["w_ml_k"][j], P["w_ml_v"][j], P["w_ml_if"][j], _row(P["b_ml_if"][j]),
                bsz, seqlen)
            if c_ml is None:
                state = None
            else:
                state = (c_ml[j], n_ml[j][:, :, None, :],
                         jnp.broadcast_to(m_ml[j][:, :, None, None], m_ml[j].shape + (1, M_LANES)))
            y, c_fin, n_fin, m_fin = _ml_scan(qm, km, vm, gates, xc, z, _row(W["g_ml_norm"][j]),
                                              _row(W["ml_skip"][j]), state, bsz, seqlen)
            new_c.append(c_fin)
            new_n.append(n_fin[:, :, 0, :])
            new_m.append(m_fin[:, :, 0, 0])
            new_buf.append(tail[:, HALO - (ML_CONV - 1):, :])
            h1, xt, qt = _mix_post(y, h, P["w_ml_down"][j], _row(W["norm_ffn"][i]), P["w_peer_qt"][i])
        r2, e2, nsel, e1 = _peer_route(qt, P["peer_keys"][i])
        out_t = _peer_expert(xt, r2, e2, nsel, e1, P["peer_u"][i], P["peer_vt"][i])
        g_final = _row(W["norm_final"]) if i == depth - 1 else None
        h = _ple(out_t, h1, p[i].reshape(n, -1), _row(W["norm_ple"][i]), P["w_ple_gate"][i],
                 P["w_ple_proj"][i], g_final)
    y = h
    return (y.reshape(bsz, seqlen, d), jnp.stack(new_s), jnp.stack(new_c), jnp.stack(new_n),
            jnp.stack(new_m), jnp.stack(new_buf))


def kernel(x_prompt, x_sample, state_gla_S, state_mlstm_C, state_mlstm_n, state_mlstm_m, state_mlstm_conv,
           p_prompt, p_sample, w_gla_in, w_gla_gate, b_gla_gate, g_gla_norm, w_gla_out,
           w_ml_up, ml_conv_w, ml_conv_b, w_ml_q, w_ml_k, w_ml_v, w_ml_igate, b_ml_igate,
           w_ml_fgate, b_ml_fgate, g_ml_norm, ml_skip, w_ml_down,
           w_peer_query, peer_keys, peer_u, peer_v, norm_mix, norm_ffn, norm_ple,
           w_ple_gate, w_ple_proj, norm_final):
    W = dict(w_gla_in=w_gla_in, w_gla_gate=w_gla_gate, b_gla_gate=b_gla_gate, g_gla_norm=g_gla_norm,
             w_gla_out=w_gla_out, w_ml_up=w_ml_up, ml_conv_w=ml_conv_w, ml_conv_b=ml_conv_b,
             w_ml_q=w_ml_q, w_ml_k=w_ml_k, w_ml_v=w_ml_v, w_ml_igate=w_ml_igate, b_ml_igate=b_ml_igate,
             w_ml_fgate=w_ml_fgate, b_ml_fgate=b_ml_fgate, g_ml_norm=g_ml_norm, ml_skip=ml_skip,
             w_ml_down=w_ml_down, w_peer_query=w_peer_query, peer_keys=peer_keys, peer_u=peer_u,
             peer_v=peer_v, norm_mix=norm_mix, norm_ffn=norm_ffn, norm_ple=norm_ple,
             w_ple_gate=w_ple_gate, w_ple_proj=w_ple_proj, norm_final=norm_final)
    depth = norm_mix.shape[0]
    P = _prep_weights(W, depth)
    y_p, s_p, c_p, n_p, m_p, buf_p = _run_group(x_prompt, p_prompt, None, None, None, None, None, W, P, depth)
    y_s, s_s, c_s, n_s, m_s, buf_s = _run_group(x_sample, p_sample, state_gla_S, state_mlstm_C,
                                                state_mlstm_n, state_mlstm_m, state_mlstm_conv, W, P, depth)
    return (y_p, y_s, s_p, s_s, c_p, c_s, n_p, n_s, m_p, m_s, buf_p, buf_s)
```
